```python
import functools
import jax
import jax.numpy as jnp
from jax import lax
import numpy as np

D_MODEL = 2048
BATCH = 8
SEQ = 2048
DEPTH = 2

CTX_LEN = 256
GRID_W = 64
HEAD_DIM = 128
ROPE_BASE = 10000.0
NORM_EPS = 1e-6
NEG_INF = -1e30

CHUNK = 128
A_GROUPS = 4
A_GROUP_CH = 128
A_CH = A_GROUPS * A_GROUP_CH

B_HEADS = 8
B_KV_HEADS = 2
B_GROUP = B_HEADS // B_KV_HEADS
B_WINDOW = 128
B_BLOCK = 128
B_Q_W = B_HEADS * HEAD_DIM
B_KV_W = B_KV_HEADS * HEAD_DIM

C_W = 512
C_CONV = 3

D_HEADS = 4
D_W = D_HEADS * HEAD_DIM
NA_KH = 8
NA_KW = 16

N_BRANCH = 4
OFF_AU = 0
OFF_AV = OFF_AU + A_CH
OFF_BQ = OFF_AV + A_CH
OFF_BK = OFF_BQ + B_Q_W
OFF_BV = OFF_BK + B_KV_W
OFF_CB = OFF_BV + B_KV_W
OFF_CC = OFF_CB + C_W
OFF_CH = OFF_CC + C_W
OFF_DQ = OFF_CH + C_W
OFF_DK = OFF_DQ + D_W
OFF_DV = OFF_DK + D_W
OFF_G = OFF_DV + D_W
IN_W = OFF_G + N_BRANCH * D_MODEL

BR_W = (A_CH, B_Q_W, C_W, D_W)
BR_OFF = (0, A_CH, A_CH + B_Q_W, A_CH + B_Q_W + C_W)
MIX_W = A_CH + B_Q_W + C_W + D_W

FFN_DIM = 7168
N_EXPERTS = 8
TOP_K = 2

kernel_name = "hybrid_gated_branch_diffusion_block"


def _rmsnorm(x, g):
    xf = x.astype(jnp.float32)
    y = xf * lax.rsqrt(jnp.mean(xf * xf, axis=-1, keepdims=True) + NORM_EPS)
    return (y * g.astype(jnp.float32)).astype(x.dtype)


def _modulation(cond, w, b):
    m = jax.nn.silu(cond) @ w + b
    return [t[..., None, :] for t in jnp.split(m, 6, axis=-1)]


def _heads(t, n):
    return t.reshape(t.shape[0], t.shape[1], n, HEAD_DIM)


def _rope_1d(x, pos):
    f = x.shape[-1] // 2
    inv = ROPE_BASE ** (-jnp.arange(f, dtype=jnp.float32) / f)
    ang = pos.astype(jnp.float32)[:, None] * inv[None, :]
    cos, sin = jnp.cos(ang)[:, None, :], jnp.sin(ang)[:, None, :]
    x1, x2 = x[..., :f].astype(jnp.float32), x[..., f:].astype(jnp.float32)
    return jnp.concatenate([x1 * cos - x2 * sin, x2 * cos + x1 * sin], axis=-1).astype(x.dtype)


def _rope_2d(x, pos_row, pos_col):
    half = x.shape[-1] // 2
    return jnp.concatenate([_rope_1d(x[..., :half], pos_row), _rope_1d(x[..., half:], pos_col)], axis=-1)


def _chunk_gmlp(u, v, norm_g, ws, bs):
    bn, L, _ = v.shape
    vn = _rmsnorm(v, norm_g).reshape(bn, L // CHUNK, CHUNK, A_GROUPS, A_GROUP_CH)
    s = jnp.einsum('gpq,bnqgc->bnpgc', ws, vn) + bs.T[None, None, :, :, None]
    return u * s.reshape(bn, L, A_CH)


def _short_conv(bg, cg, hh, w_conv):
    z = cg * hh
    L = z.shape[1]
    pad = (C_CONV - 1) // 2
    zp = jnp.pad(z, ((0, 0), (pad, C_CONV - 1 - pad), (0, 0)))
    conv = zp[:, 0:L] * w_conv[0]
    for tap in range(1, C_CONV):
        conv = conv + zp[:, tap:tap + L] * w_conv[tap]
    return bg * conv


def _window_gqa(q, k, v, kc, vc, sink):
    bn, S = q.shape[0], q.shape[1]
    nb = S // B_BLOCK
    lc = kc.shape[1]
    scale = HEAD_DIM ** -0.5
    qb = q.reshape(bn, nb, B_BLOCK, B_KV_HEADS, B_GROUP, HEAD_DIM)

    def band(t):
        tp = jnp.pad(t, ((0, 0), (B_BLOCK, B_BLOCK), (0, 0), (0, 0)))
        tp = tp.reshape(bn, nb + 2, B_BLOCK, B_KV_HEADS, HEAD_DIM)
        return jnp.concatenate([tp[:, :-2], tp[:, 1:-1], tp[:, 2:]], axis=2)

    kb, vb = band(k), band(v)
    s_loc = jnp.einsum('bnqhgd,bnkhd->bhgnqk', qb, kb).astype(jnp.float32) * scale
    blk = jnp.arange(nb)[:, None] * B_BLOCK
    qpos = blk + jnp.arange(B_BLOCK)[None, :]
    kpos = blk - B_BLOCK + jnp.arange(3 * B_BLOCK)[None, :]
    rel = kpos[:, None, :] - qpos[:, :, None]
    valid = (jnp.abs(rel) <= B_WINDOW) & (kpos[:, None, :] >= 0) & (kpos[:, None, :] < S)
    s_loc = jnp.where(valid, s_loc, NEG_INF)
    s_ctx = jnp.einsum('bnqhgd,bchd->bhgnqc', qb, kc).astype(jnp.float32) * scale
    sink_b = jnp.broadcast_to(
        sink.astype(jnp.float32).reshape(B_KV_HEADS, B_GROUP)[None, :, :, None, None, None],
        s_ctx.shape[:-1] + (1,))
    p = jax.nn.softmax(jnp.concatenate([sink_b, s_ctx, s_loc], axis=-1), axis=-1)
    p_ctx = p[..., 1:1 + lc].astype(v.dtype)
    p_loc = p[..., 1 + lc:].astype(v.dtype)
    o = (jnp.einsum('bhgnqc,bchd->bnqhgd', p_ctx, vc)
         + jnp.einsum('bhgnqk,bnkhd->bnqhgd', p_loc, vb))
    return o.reshape(bn, S, B_Q_W)


def _ctx_gqa_sink(q, k, v, sink):
    bn, lc = q.shape[0], q.shape[1]
    qg = q.reshape(bn, lc, B_KV_HEADS, B_GROUP, HEAD_DIM)
    s = jnp.einsum('bqhgd,bkhd->bhgqk', qg, k).astype(jnp.float32) * HEAD_DIM ** -0.5
    sink_b = jnp.broadcast_to(
        sink.astype(jnp.float32).reshape(B_KV_HEADS, B_GROUP)[None, :, :, None, None],
        s.shape[:-1] + (1,))
    p = jax.nn.softmax(jnp.concatenate([sink_b, s], axis=-1), axis=-1)
    o = jnp.einsum('bhgqk,bkhd->bqhgd', p[..., 1:].astype(v.dtype), v)
    return o.reshape(bn, lc, B_Q_W)


def _ctx_mha(q, k, v):
    bn, lc = q.shape[0], q.shape[1]
    s = jnp.einsum('bqhd,bkhd->bhqk', q, k).astype(jnp.float32) * HEAD_DIM ** -0.5
    p = jax.nn.softmax(s, axis=-1).astype(v.dtype)
    return jnp.einsum('bhqk,bkhd->bqhd', p, v).reshape(bn, lc, q.shape[2] * HEAD_DIM)


def _neighborhood_attn(q, k, v, kc, vc, rpb):
    bn, S = q.shape[0], q.shape[1]
    rows = S // GRID_W
    kh = min(NA_KH, rows)
    kw = NA_KW
    lc = kc.shape[1]
    scale = HEAD_DIM ** -0.5
    qg = q.reshape(bn, rows, GRID_W, D_HEADS, HEAD_DIM)
    kg = k.reshape(bn, rows, GRID_W, D_HEADS, HEAD_DIM)
    vg = v.reshape(bn, rows, GRID_W, D_HEADS, HEAD_DIM)
    col = jnp.arange(GRID_W)
    cstart = jnp.clip(col - kw // 2, 0, GRID_W - kw)
    cidx = cstart[:, None] + jnp.arange(kw)[None, :]
    dc = cidx - col[:, None] + (NA_KW - 1)

    def one_row(r):
        rs = jnp.clip(r - kh // 2, 0, rows - kh)
        kr = lax.dynamic_slice_in_dim(kg, rs, kh, axis=1)
        vr = lax.dynamic_slice_in_dim(vg, rs, kh, axis=1)
        kwin = kr[:, :, cidx]
        vwin = vr[:, :, cidx]
        qr = lax.dynamic_index_in_dim(qg, r, axis=1, keepdims=False)
        dr = rs + jnp.arange(kh) - r + (NA_KH - 1)
        bias = rpb[:, dr[:, None, None], dc[None, :, :]].astype(jnp.float32)
        s_loc = jnp.einsum('bqhd,bkqjhd->bhqkj', qr, kwin).astype(jnp.float32) * scale
        s_loc = (s_loc + jnp.transpose(bias, (0, 2, 1, 3))[None]).reshape(bn, D_HEADS, GRID_W, kh * kw)
        s_ctx = jnp.einsum('bqhd,bchd->bhqc', qr, kc).astype(jnp.float32) * scale
        p = jax.nn.softmax(jnp.concatenate([s_ctx, s_loc], axis=-1), axis=-1).astype(v.dtype)
        p_loc = p[..., lc:].reshape(bn, D_HEADS, GRID_W, kh, kw)
        return (jnp.einsum('bhqc,bchd->bqhd', p[..., :lc], vc)
                + jnp.einsum('bhqkj,bkqjhd->bqhd', p_loc, vwin))

    out = lax.map(one_row, jnp.arange(rows))
    return jnp.transpose(out, (1, 0, 2, 3, 4)).reshape(bn, S, D_W)


def _branches_latent(p, kc_b, vc_b, kc_d, vc_d, a_g, a_w, a_b, sink, conv_w, rpb, pos_row, pos_col):
    ya = _chunk_gmlp(p[..., OFF_AU:OFF_AV], p[..., OFF_AV:OFF_BQ], a_g, a_w, a_b)
    qb = _rope_2d(_heads(p[..., OFF_BQ:OFF_BK], B_HEADS), pos_row, pos_col)
    kb = _rope_2d(_heads(p[..., OFF_BK:OFF_BV], B_KV_HEADS), pos_row, pos_col)
    vb = _heads(p[..., OFF_BV:OFF_CB], B_KV_HEADS)
    yb = _window_gqa(qb, kb, vb, kc_b, vc_b, sink)
    yc = _short_conv(p[..., OFF_CB:OFF_CC], p[..., OFF_CC:OFF_CH], p[..., OFF_CH:OFF_DQ], conv_w)
    yd = _neighborhood_attn(_heads(p[..., OFF_DQ:OFF_DK], D_HEADS), _heads(p[..., OFF_DK:OFF_DV], D_HEADS),
                            _heads(p[..., OFF_DV:OFF_G], D_HEADS), kc_d, vc_d, rpb)
    return (ya, yb, yc, yd)


def _branches_context(p, kc_b, vc_b, kc_d, vc_d, a_g, a_w, a_b, sink, conv_w):
    ya = _chunk_gmlp(p[..., OFF_AU:OFF_AV], p[..., OFF_AV:OFF_BQ], a_g, a_w, a_b)
    yb = _ctx_gqa_sink(_heads(p[..., OFF_BQ:OFF_BK], B_HEADS), kc_b, vc_b, sink)
    yc = _short_conv(p[..., OFF_CB:OFF_CC], p[..., OFF_CC:OFF_CH], p[..., OFF_CH:OFF_DQ], conv_w)
    yd = _ctx_mha(_heads(p[..., OFF_DQ:OFF_DK], D_HEADS), kc_d, vc_d)
    return (ya, yb, yc, yd)


def _merge(p, ys, w_branch, w_out):
    bn, L = p.shape[0], p.shape[1]
    gates = jax.nn.sigmoid(p[..., OFF_G:]).reshape(bn, L, N_BRANCH, D_MODEL)
    z = gates[..., 0, :] * (ys[0] @ w_branch[BR_OFF[0]:BR_OFF[0] + BR_W[0]])
    for i in range(1, N_BRANCH):
        z = z + gates[..., i, :] * (ys[i] @ w_branch[BR_OFF[i]:BR_OFF[i] + BR_W[i]])
    return z @ w_out


def _swiglu(x, w1, w3, w2):
    return (jax.nn.silu(x @ w1) * (x @ w3)) @ w2


def _moe_swiglu(x, w_router, w1, w3, w2):
    bn, L, dm = x.shape
    xt = x.reshape(bn * L, dm)
    logits = (xt @ w_router).astype(jnp.float32)
    top_v, top_i = lax.top_k(logits, TOP_K)
    wts = jax.nn.softmax(top_v, axis=-1)
    gate = jnp.sum(jax.nn.one_hot(top_i, N_EXPERTS, dtype=jnp.float32) * wts[..., None], axis=1)
    gate = gate.astype(x.dtype)
    out = gate[:, 0:1] * _swiglu(xt, w1[0], w3[0], w2[0])
    for e in range(1, N_EXPERTS):
        out = out + gate[:, e:e + 1] * _swiglu(xt, w1[e], w3[e], w2[e])
    return out.reshape(bn, L, dm)


def setup_inputs(seed: int = 0) -> dict:
    key = jax.random.key(seed)
    ks = iter(jax.random.split(key, 32))

    def nrm(shape, s):
        return jax.random.normal(next(ks), shape, jnp.float32) * s

    n_dense = (DEPTH + 1) // 2
    n_moe = DEPTH // 2
    br_scale = jnp.concatenate([jnp.full((w, 1), w ** -0.5, jnp.float32) for w in BR_W], axis=0)
    return {
        "x": nrm((BATCH, SEQ, D_MODEL), 1.0),
        "c": nrm((BATCH, D_MODEL), 1.0),
        "ctx": nrm((BATCH, CTX_LEN, D_MODEL), 1.0),
        "c_ctx": nrm((D_MODEL,), 1.0),
        "w_ada": nrm((DEPTH, D_MODEL, 6 * D_MODEL), 0.3 * D_MODEL ** -0.5),
        "b_ada": nrm((DEPTH, 6 * D_MODEL), 0.02),
        "g_mix": 1.0 + nrm((DEPTH, D_MODEL), 0.02),
        "g_ffn": 1.0 + nrm((DEPTH, D_MODEL), 0.02),
        "w_in": nrm((DEPTH, D_MODEL, IN_W), D_MODEL ** -0.5),
        "a_norm_g": 1.0 + nrm((DEPTH, A_CH), 0.02),
        "a_ws": nrm((DEPTH, A_GROUPS, CHUNK, CHUNK), CHUNK ** -0.5),
        "a_bs": 1.0 + nrm((DEPTH, A_GROUPS, CHUNK), 0.02),
        "b_sink": nrm((DEPTH, B_HEADS), 0.5),
        "c_conv": nrm((DEPTH, C_CONV, C_W), C_CONV ** -0.5),
        "d_rpb": nrm((DEPTH, D_HEADS, 2 * NA_KH - 1, 2 * NA_KW - 1), 0.1),
        "w_branch": nrm((DEPTH, MIX_W, D_MODEL), 1.0) * br_scale,
        "w_out": nrm((DEPTH, D_MODEL, D_MODEL), D_MODEL ** -0.5),
        "ffn_w1": nrm((n_dense, D_MODEL, FFN_DIM), D_MODEL ** -0.5),
        "ffn_w3": nrm((n_dense, D_MODEL, FFN_DIM), D_MODEL ** -0.5),
        "ffn_w2": nrm((n_dense, FFN_DIM, D_MODEL), FFN_DIM ** -0.5),
        "w_router": nrm((n_moe, D_MODEL, N_EXPERTS), D_MODEL ** -0.5),
        "moe_w1": nrm((n_moe, N_EXPERTS, D_MODEL, FFN_DIM), D_MODEL ** -0.5),
        "moe_w3": nrm((n_moe, N_EXPERTS, D_MODEL, FFN_DIM), D_MODEL ** -0.5),
        "moe_w2": nrm((n_moe, N_EXPERTS, FFN_DIM, D_MODEL), FFN_DIM ** -0.5),
        "g_final": 1.0 + nrm((D_MODEL,), 0.02),
    }


def reference(x, c, ctx, c_ctx, w_ada, b_ada, g_mix, g_ffn, w_in, a_norm_g, a_ws, a_bs,
              b_sink, c_conv, d_rpb, w_branch, w_out, ffn_w1, ffn_w3, ffn_w2,
              w_router, moe_w1, moe_w3, moe_w2, g_final):
    seq = x.shape[1]
    pos = jnp.arange(seq)
    pos_row, pos_col = pos // GRID_W, pos % GRID_W
    h, hc = x, ctx
    for layer in range(DEPTH):
        last = layer == DEPTH - 1
        sh1, sc1, gt1, sh2, sc2, gt2 = _modulation(c, w_ada[layer], b_ada[layer])
        csh1, csc1, cgt1, csh2, csc2, cgt2 = _modulation(c_ctx, w_ada[layer], b_ada[layer])
        wl = w_in[layer]

        cn = _rmsnorm(hc, g_mix[layer]) * (1 + csc1) + csh1
        if last:
            ckv_b = cn @ wl[:, OFF_BK:OFF_CB]
            ckv_d = cn @ wl[:, OFF_DK:OFF_G]
        else:
            pc = cn @ wl
            ckv_b = pc[..., OFF_BK:OFF_CB]
            ckv_d = pc[..., OFF_DK:OFF_G]
        kc_b = _heads(ckv_b[..., :B_KV_W], B_KV_HEADS)
        vc_b = _heads(ckv_b[..., B_KV_W:], B_KV_HEADS)
        kc_d = _heads(ckv_d[..., :D_W], D_HEADS)
        vc_d = _heads(ckv_d[..., D_W:], D_HEADS)

        xn = _rmsnorm(h, g_mix[layer]) * (1 + sc1) + sh1
        px = xn @ wl
        ys = _branches_latent(px, kc_b, vc_b, kc_d, vc_d, a_norm_g[layer], a_ws[layer], a_bs[layer],
                              b_sink[layer], c_conv[layer], d_rpb[layer], pos_row, pos_col)
        h = h + gt1 * _merge(px, ys, w_branch[layer], w_out[layer])
        if not last:
            ysc = _branches_context(pc, kc_b, vc_b, kc_d, vc_d, a_norm_g[layer], a_ws[layer],
                                    a_bs[layer], b_sink[layer], c_conv[layer])
            hc = hc + cgt1 * _merge(pc, ysc, w_branch[layer], w_out[layer])

        if layer % 2 == 0:
            j = layer // 2
            ffn = functools.partial(_swiglu, w1=ffn_w1[j], w3=ffn_w3[j], w2=ffn_w2[j])
        else:
            j = layer // 2
            ffn = functools.partial(_moe_swiglu, w_router=w_router[j], w1=moe_w1[j], w3=moe_w3[j], w2=moe_w2[j])
        xn = _rmsnorm(h, g_ffn[layer]) * (1 + sc2) + sh2
        h = h + gt2 * ffn(xn)
        if not last:
            cn = _rmsnorm(hc, g_ffn[layer]) * (1 + csc2) + csh2
            hc = hc + cgt2 * ffn(cn)
    return _rmsnorm(h, g_final)
```

```python
import functools

import jax
import jax.numpy as jnp
import numpy as np
from jax import lax
from jax.experimental import pallas as pl
from jax.experimental.pallas import tpu as pltpu

F32 = jnp.float32
BF16 = jnp.bfloat16

D_MODEL = 2048
SEQ = 2048
CTX_LEN = 256
GRID_W = 64
HEAD_DIM = 128
ROPE_BASE = 10000.0
NORM_EPS = 1e-6
NEG_INF = -1e30

CHUNK = 128
A_GROUPS = 4
A_CH = 512
B_HEADS = 8
B_KV_HEADS = 2
B_GROUP = B_HEADS // B_KV_HEADS
B_WINDOW = 128
B_Q_W = B_HEADS * HEAD_DIM
B_KV_W = B_KV_HEADS * HEAD_DIM
C_W = 512
D_HEADS = 4
D_W = D_HEADS * HEAD_DIM
NA_KH = 8
NA_KW = 16

OFF_AU = 0
OFF_AV = OFF_AU + A_CH
OFF_BQ = OFF_AV + A_CH
OFF_BK = OFF_BQ + B_Q_W
OFF_BV = OFF_BK + B_KV_W
OFF_CB = OFF_BV + B_KV_W
OFF_CC = OFF_CB + C_W
OFF_CH = OFF_CC + C_W
OFF_DQ = OFF_CH + C_W
OFF_DK = OFF_DQ + D_W
OFF_DV = OFF_DK + D_W
OFF_G = OFF_DV + D_W
IN_W = OFF_G + 4 * D_MODEL
MIX_W = A_CH + B_Q_W + C_W + D_W

FFN_DIM = 7168
N_EXPERTS = 8
TOP_K = 2

LANES = 128
VMEM_BYTES = 64 * 2**20
MIB = 2**20

TM_PROJ = 1024
TN_PROJ = 512
TM_DOWN = 512
TM_MERGE = 512
TM_MOE = 512
TG_ROWS = 256
MOD_ROWS = 16


def _cparams(sem, vmem_mib):
    return pltpu.CompilerParams(dimension_semantics=sem, vmem_limit_bytes=vmem_mib * MIB)


def _silu(a):
    return a * jax.nn.sigmoid(a)


def _ada_kernel(c_ref, w_ref, b_ref, o_ref):
    c = c_ref[...]
    s = _silu(c).astype(BF16)
    o_ref[...] = jnp.dot(s, w_ref[...].astype(BF16), preferred_element_type=F32) + b_ref[...]


def _ada(cond, w_ada, b_ada):
    depth, _, n = w_ada.shape
    tn = 1024
    return pl.pallas_call(
        _ada_kernel,
        out_shape=jax.ShapeDtypeStruct((depth, MOD_ROWS, n), F32),
        grid=(depth, n // tn),
        in_specs=[
            pl.BlockSpec((MOD_ROWS, D_MODEL), lambda l, j: (0, 0)),
            pl.BlockSpec((None, D_MODEL, tn), lambda l, j: (l, 0, j)),
            pl.BlockSpec((None, 1, tn), lambda l, j: (l, 0, j)),
        ],
        out_specs=pl.BlockSpec((None, MOD_ROWS, tn), lambda l, j: (l, 0, j)),
        compiler_params=_cparams(("arbitrary", "arbitrary"), 40),
        name="ada_modulation",
    )(cond, w_ada, b_ada.reshape(depth, 1, n))


def _norm_mod(x, g, sc, sh):
    y = x * lax.rsqrt(jnp.mean(x * x, axis=-1, keepdims=True) + NORM_EPS) * g
    return y * (1.0 + sc) + sh


def _norm_mm_kernel(x_ref, g_ref, sc_ref, sh_ref, *rest, swiglu):
    n_w = 2 if swiglu else 1
    w_refs, o_ref, xn_ref = rest[:n_w], rest[n_w], rest[n_w + 1]

    @pl.when(pl.program_id(1) == 0)
    def _():
        xn_ref[...] = _norm_mod(x_ref[...], g_ref[...], sc_ref[0], sh_ref[0]).astype(BF16)

    xn = xn_ref[...]
    a = jnp.dot(xn, w_refs[0][...].astype(BF16), preferred_element_type=F32)
    if swiglu:
        b = jnp.dot(xn, w_refs[1][...].astype(BF16), preferred_element_type=F32)
        a = _silu(a) * b
    o_ref[...] = a.astype(o_ref.dtype)


def _norm_mm(x, g, sc, sh, mod_row, weights, layer, col_block, n_out, out_dtype, tm, name):
    m = x.shape[0]
    swiglu = len(weights) == 2
    tn = TN_PROJ
    w_spec = pl.BlockSpec((None, D_MODEL, tn), lambda i, j: (layer, 0, col_block(j)))
    return pl.pallas_call(
        functools.partial(_norm_mm_kernel, swiglu=swiglu),
        out_shape=jax.ShapeDtypeStruct((m, n_out), out_dtype),
        grid=(m // tm, n_out // tn),
        in_specs=[
            pl.BlockSpec((tm, D_MODEL), lambda i, j: (i, 0)),
            pl.BlockSpec((1, D_MODEL), lambda i, j: (0, 0)),
            pl.BlockSpec((1, 1, D_MODEL), lambda i, j: (mod_row(i), 0, 0)),
            pl.BlockSpec((1, 1, D_MODEL), lambda i, j: (mod_row(i), 0, 0)),
        ] + [w_spec] * len(weights),
        out_specs=pl.BlockSpec((tm, tn), lambda i, j: (i, j)),
        scratch_shapes=[pltpu.VMEM((tm, D_MODEL), BF16)],
        compiler_params=_cparams(("arbitrary", "arbitrary"), 56 if swiglu else 48),
        name=name,
    )(x, g, sc, sh, *weights)


def _res_mm_kernel(x_ref, w_ref, res_ref, gt_ref, o_ref):
    acc = jnp.dot(x_ref[...], w_ref[...].astype(BF16), preferred_element_type=F32)
    o_ref[...] = res_ref[...] + gt_ref[0] * acc


def _res_mm(x, w, layer, res, gt, mod_row, tm, name):
    m, k = x.shape
    n = res.shape[1]
    tn = TN_PROJ
    return pl.pallas_call(
        _res_mm_kernel,
        out_shape=jax.ShapeDtypeStruct((m, n), F32),
        grid=(m // tm, n // tn),
        in_specs=[
            pl.BlockSpec((tm, k), lambda i, j: (i, 0)),
            pl.BlockSpec((None, k, tn), lambda i, j: (layer, 0, j)),
            pl.BlockSpec((tm, tn), lambda i, j: (i, j)),
            pl.BlockSpec((1, 1, tn), lambda i, j: (mod_row(i), 0, j)),
        ],
        out_specs=pl.BlockSpec((tm, tn), lambda i, j: (i, j)),
        compiler_params=_cparams(("arbitrary", "arbitrary"), 56),
        name=name,
    )(x, w, res, gt)


def _gmlp_kernel(u_ref, v_ref, g_ref, ws_ref, bs_ref, o_ref):
    v = v_ref[...]
    vn = (v * lax.rsqrt(jnp.mean(v * v, axis=-1, keepdims=True) + NORM_EPS) * g_ref[...]).astype(BF16)
    rows = v.shape[0]
    for gi in range(A_GROUPS):
        w = ws_ref[gi].astype(BF16)
        bias = bs_ref[gi]
        cs = slice(gi * LANES, (gi + 1) * LANES)
        for c in range(rows // CHUNK):
            rs = slice(c * CHUNK, (c + 1) * CHUNK)
            s = jnp.dot(w, vn[rs, cs], preferred_element_type=F32) + bias
            o_ref[rs, cs] = (u_ref[rs, cs] * s).astype(o_ref.dtype)


def _gmlp(p, norm_g, ws, bs, layer):
    m = p.shape[0]
    tm = 512
    return pl.pallas_call(
        _gmlp_kernel,
        out_shape=jax.ShapeDtypeStruct((m, A_CH), BF16),
        grid=(m // tm,),
        in_specs=[
            pl.BlockSpec((tm, A_CH), lambda i: (i, OFF_AU // A_CH)),
            pl.BlockSpec((tm, A_CH), lambda i: (i, OFF_AV // A_CH)),
            pl.BlockSpec((1, A_CH), lambda i: (0, 0)),
            pl.BlockSpec((None, A_GROUPS, CHUNK, CHUNK), lambda i: (layer, 0, 0, 0)),
            pl.BlockSpec((None, A_GROUPS, CHUNK, 1), lambda i: (layer, 0, 0, 0)),
        ],
        out_specs=pl.BlockSpec((tm, A_CH), lambda i: (i, 0)),
        compiler_params=_cparams(("arbitrary",), 32),
        name="mixer_a_gmlp",
    )(p, p, norm_g[layer].reshape(1, A_CH), ws, bs.reshape(bs.shape + (1,)))


CONV_ROWS = 256
HALO = 8


def _conv_kernel(b_ref, c_ref, h_ref, cp_ref, hp_ref, cn_ref, hn_ref, w_ref, o_ref, *, tiles_per_seq):
    i = pl.program_id(0)
    z = c_ref[...] * h_ref[...]
    n = z.shape[0]
    first = (i % tiles_per_seq) == 0
    last = (i % tiles_per_seq) == tiles_per_seq - 1
    zp = jnp.where(first, 0.0, cp_ref[HALO - 1:HALO, :] * hp_ref[HALO - 1:HALO, :])
    zn = jnp.where(last, 0.0, cn_ref[0:1, :] * hn_ref[0:1, :])
    pos = lax.broadcasted_iota(jnp.int32, z.shape, 0)
    z_prev = jnp.where(pos == 0, zp, pltpu.roll(z, 1, axis=0))
    z_next = jnp.where(pos == n - 1, zn, pltpu.roll(z, n - 1, axis=0))
    w = w_ref[...]
    conv = z_prev * w[0:1] + z * w[1:2] + z_next * w[2:3]
    o_ref[...] = (b_ref[...] * conv).astype(o_ref.dtype)


def _short_conv(p, conv_w8, layer, seq_len):
    m = p.shape[0]
    tr = CONV_ROWS
    per = tr // HALO
    n_halo = m // HALO

    def prev_blk(col):
        return lambda i: (jnp.maximum(i * per - 1, 0), col)

    def next_blk(col):
        return lambda i: (jnp.minimum((i + 1) * per, n_halo - 1), col)

    cc, ch = OFF_CC // C_W, OFF_CH // C_W
    return pl.pallas_call(
        functools.partial(_conv_kernel, tiles_per_seq=seq_len // tr),
        out_shape=jax.ShapeDtypeStruct((m, C_W), BF16),
        grid=(m // tr,),
        in_specs=[
            pl.BlockSpec((tr, C_W), lambda i: (i, OFF_CB // C_W)),
            pl.BlockSpec((tr, C_W), lambda i: (i, cc)),
            pl.BlockSpec((tr, C_W), lambda i: (i, ch)),
            pl.BlockSpec((HALO, C_W), prev_blk(cc)),
            pl.BlockSpec((HALO, C_W), prev_blk(ch)),
            pl.BlockSpec((HALO, C_W), next_blk(cc)),
            pl.BlockSpec((HALO, C_W), next_blk(ch)),
            pl.BlockSpec((None, 8, C_W), lambda i: (layer, 0, 0)),
        ],
        out_specs=pl.BlockSpec((tr, C_W), lambda i: (i, 0)),
        compiler_params=_cparams(("arbitrary",), 32),
        name="mixer_c_conv",
    )(p, p, p, p, p, p, p, conv_w8)


def _rope(x, cos, sin_lo, sin_hi):
    return x * cos + pltpu.roll(x, LANES - 32, axis=1) * sin_lo + pltpu.roll(x, 32, axis=1) * sin_hi


def _softmax_av(s_parts, v_parts, extra_logit=None):
    m = s_parts[0].max(axis=-1, keepdims=True)
    for s in s_parts[1:]:
        m = jnp.maximum(m, s.max(axis=-1, keepdims=True))
    if extra_logit is not None:
        m = jnp.maximum(m, extra_logit)
    denom = jnp.exp(extra_logit - m) if extra_logit is not None else 0.0
    acc = None
    for s, v in zip(s_parts, v_parts):
        e = jnp.exp(s - m)
        denom = denom + e.sum(axis=-1, keepdims=True)
        pv = jnp.dot(e.astype(BF16), v, preferred_element_type=F32)
        acc = pv if acc is None else acc + pv
    return acc / denom


def _qk(q, k):
    return lax.dot_general(q, k, (((1,), (1,)), ((), ())), preferred_element_type=F32)


def _sink_column(sink_ref, h, rows_per_head):
    rid = lax.broadcasted_iota(jnp.int32, (B_GROUP * rows_per_head, 1), 0) // rows_per_head
    col = jnp.zeros((B_GROUP * rows_per_head, 1), F32)
    for g in range(B_GROUP):
        col = jnp.where(rid == g, sink_ref[h * B_GROUP + g], col)
    return col


def _win_attn_kernel(sink_ref, q_ref, k_ref, v_ref, kc_ref, vc_ref, cos_ref, slo_ref, shi_ref, o_ref):
    n = pl.program_id(1)
    blk = B_WINDOW
    band = 3 * blk
    q0 = pl.multiple_of(n * blk, blk)
    k0 = pl.multiple_of(jnp.clip(n - 1, 0, SEQ // blk - 3) * blk, blk)
    scale = HEAD_DIM ** -0.5

    cos_q, slo_q, shi_q = cos_ref[pl.ds(q0, blk), :], slo_ref[pl.ds(q0, blk), :], shi_ref[pl.ds(q0, blk), :]
    cos_k, slo_k, shi_k = cos_ref[pl.ds(k0, band), :], slo_ref[pl.ds(k0, band), :], shi_ref[pl.ds(k0, band), :]

    qpos = q0 + lax.broadcasted_iota(jnp.int32, (B_GROUP * blk, band), 0) % blk
    kpos = k0 + lax.broadcasted_iota(jnp.int32, (B_GROUP * blk, band), 1)
    valid = jnp.abs(kpos - qpos) <= B_WINDOW

    for h in range(B_KV_HEADS):
        hs = slice(h * HEAD_DIM, (h + 1) * HEAD_DIM)
        k_loc = _rope(k_ref[pl.ds(k0, band), hs], cos_k, slo_k, shi_k).astype(BF16)
        v_loc = v_ref[pl.ds(k0, band), hs].astype(BF16)
        k_ctx = kc_ref[:, hs].astype(BF16)
        v_ctx = vc_ref[:, hs].astype(BF16)
        qs = []
        for g in range(B_GROUP):
            c0 = (h * B_GROUP + g) * HEAD_DIM
            qs.append(_rope(q_ref[:, c0:c0 + HEAD_DIM], cos_q, slo_q, shi_q).astype(BF16))
        qst = jnp.concatenate(qs, axis=0)
        s_loc = jnp.where(valid, _qk(qst, k_loc) * scale, NEG_INF)
        s_ctx = _qk(qst, k_ctx) * scale
        out = _softmax_av([s_ctx, s_loc], [v_ctx, v_loc], _sink_column(sink_ref, h, blk))
        for g in range(B_GROUP):
            c0 = (h * B_GROUP + g) * HEAD_DIM
            o_ref[:, c0:c0 + HEAD_DIM] = out[g * blk:(g + 1) * blk].astype(o_ref.dtype)


def _win_attn(p, pc, kc_blk, vc_blk, sink, rope_tabs, batch):
    nb = SEQ // B_WINDOW
    kvw = B_KV_W
    return pl.pallas_call(
        _win_attn_kernel,
        out_shape=jax.ShapeDtypeStruct((batch * SEQ, B_Q_W), BF16),
        grid=(batch, nb),
        in_specs=[
            pl.BlockSpec(memory_space=pltpu.SMEM),
            pl.BlockSpec((B_WINDOW, B_Q_W), lambda b, n: (b * nb + n, OFF_BQ // B_Q_W)),
            pl.BlockSpec((SEQ, kvw), lambda b, n: (b, OFF_BK // kvw)),
            pl.BlockSpec((SEQ, kvw), lambda b, n: (b, OFF_BV // kvw)),
            pl.BlockSpec((CTX_LEN, kvw), lambda b, n: (b, kc_blk)),
            pl.BlockSpec((CTX_LEN, kvw), lambda b, n: (b, vc_blk)),
            pl.BlockSpec((SEQ, HEAD_DIM), lambda b, n: (0, 0)),
            pl.BlockSpec((SEQ, HEAD_DIM), lambda b, n: (0, 0)),
            pl.BlockSpec((SEQ, HEAD_DIM), lambda b, n: (0, 0)),
        ],
        out_specs=pl.BlockSpec((B_WINDOW, B_Q_W), lambda b, n: (b * nb + n, 0)),
        compiler_params=_cparams(("arbitrary", "arbitrary"), 40),
        name="mixer_b_window_attn",
    )(sink, p, p, p, pc, pc, *rope_tabs)


def _ctx_gqa_kernel(sink_ref, q_ref, kc_ref, vc_ref, o_ref):
    scale = HEAD_DIM ** -0.5
    lc = q_ref.shape[0]
    for h in range(B_KV_HEADS):
        hs = slice(h * HEAD_DIM, (h + 1) * HEAD_DIM)
        k_ctx = kc_ref[:, hs].astype(BF16)
        v_ctx = vc_ref[:, hs].astype(BF16)
        qst = jnp.concatenate(
            [q_ref[:, (h * B_GROUP + g) * HEAD_DIM:(h * B_GROUP + g + 1) * HEAD_DIM].astype(BF16)
             for g in range(B_GROUP)], axis=0)
        out = _softmax_av([_qk(qst, k_ctx) * scale], [v_ctx], _sink_column(sink_ref, h, lc))
        for g in range(B_GROUP):
            c0 = (h * B_GROUP + g) * HEAD_DIM
            o_ref[:, c0:c0 + HEAD_DIM] = out[g * lc:(g + 1) * lc].astype(o_ref.dtype)


def _ctx_gqa(pc, sink, batch):
    kvw = B_KV_W
    return pl.pallas_call(
        _ctx_gqa_kernel,
        out_shape=jax.ShapeDtypeStruct((batch * CTX_LEN, B_Q_W), BF16),
        grid=(batch,),
        in_specs=[
            pl.BlockSpec(memory_space=pltpu.SMEM),
            pl.BlockSpec((CTX_LEN, B_Q_W), lambda b: (b, OFF_BQ // B_Q_W)),
            pl.BlockSpec((CTX_LEN, kvw), lambda b: (b, OFF_BK // kvw)),
            pl.BlockSpec((CTX_LEN, kvw), lambda b: (b, OFF_BV // kvw)),
        ],
        out_specs=pl.BlockSpec((CTX_LEN, B_Q_W), lambda b: (b, 0)),
        compiler_params=_cparams(("arbitrary",), 32),
        name="mixer_b_ctx_attn",
    )(sink, pc, pc, pc)


NA_ROWS_PER_STEP = 4
GRID_H = SEQ // GRID_W
NA_LOC = NA_KH * GRID_W


def _na_kernel(q_ref, k_ref, v_ref, kc_ref, vc_ref, bias_ref, o_ref):
    step = pl.program_id(1)
    scale = HEAD_DIM ** -0.5
    for rr in range(NA_ROWS_PER_STEP):
        r = step * NA_ROWS_PER_STEP + rr
        rs = jnp.clip(r - NA_KH // 2, 0, GRID_H - NA_KH)
        k0 = pl.multiple_of(rs * GRID_W, GRID_W)
        d = r - rs
        qrows = slice(rr * GRID_W, (rr + 1) * GRID_W)
        for h in range(D_HEADS):
            hs = slice(h * HEAD_DIM, (h + 1) * HEAD_DIM)
            q = q_ref[qrows, hs].astype(BF16)
            k_loc = k_ref[pl.ds(k0, NA_LOC), hs].astype(BF16)
            v_loc = v_ref[pl.ds(k0, NA_LOC), hs].astype(BF16)
            s_loc = _qk(q, k_loc) * scale + bias_ref[h, d]
            s_ctx = _qk(q, kc_ref[:, hs].astype(BF16)) * scale
            out = _softmax_av([s_ctx, s_loc], [vc_ref[:, hs].astype(BF16), v_loc])
            o_ref[qrows, hs] = out.astype(o_ref.dtype)


def _na_bias_table(rpb):
    col = np.arange(GRID_W)
    cstart = np.clip(col - NA_KW // 2, 0, GRID_W - NA_KW)
    kc = np.arange(GRID_W)
    inside = (kc[None, :] >= cstart[:, None]) & (kc[None, :] < cstart[:, None] + NA_KW)
    dc = np.clip(kc[None, :] - col[:, None] + NA_KW - 1, 0, 2 * NA_KW - 2)
    a = np.arange(NA_KH)
    d = np.arange(NA_KH)
    dr = a[None, :] - d[:, None] + NA_KH - 1
    tab = rpb[:, dr[:, :, None, None], dc[None, None, :, :]]
    tab = jnp.where(inside[None, None, None], tab.astype(F32), NEG_INF)
    return jnp.transpose(tab, (0, 1, 3, 2, 4)).reshape(D_HEADS, NA_KH, GRID_W, NA_LOC)


def _na_attn(p, pc, kc_blk, vc_blk, bias_tab, batch):
    rows_q = NA_ROWS_PER_STEP * GRID_W
    steps = SEQ // rows_q
    return pl.pallas_call(
        _na_kernel,
        out_shape=jax.ShapeDtypeStruct((batch * SEQ, D_W), BF16),
        grid=(batch, steps),
        in_specs=[
            pl.BlockSpec((rows_q, D_W), lambda b, s: (b * steps + s, OFF_DQ // D_W)),
            pl.BlockSpec((SEQ, D_W), lambda b, s: (b, OFF_DK // D_W)),
            pl.BlockSpec((SEQ, D_W), lambda b, s: (b, OFF_DV // D_W)),
            pl.BlockSpec((CTX_LEN, D_W), lambda b, s: (b, kc_blk)),
            pl.BlockSpec((CTX_LEN, D_W), lambda b, s: (b, vc_blk)),
            pl.BlockSpec((D_HEADS, NA_KH, GRID_W, NA_LOC), lambda b, s: (0, 0, 0, 0)),
        ],
        out_specs=pl.BlockSpec((rows_q, D_W), lambda b, s: (b * steps + s, 0)),
        compiler_params=_cparams(("arbitrary", "arbitrary"), 48),
        name="mixer_d_neighborhood_attn",
    )(p, p, p, pc, pc, bias_tab)


def _ctx_mha_kernel(q_ref, kc_ref, vc_ref, o_ref):
    scale = HEAD_DIM ** -0.5
    for h in range(D_HEADS):
        hs = slice(h * HEAD_DIM, (h + 1) * HEAD_DIM)
        s = _qk(q_ref[:, hs].astype(BF16), kc_ref[:, hs].astype(BF16)) * scale
        o_ref[:, hs] = _softmax_av([s], [vc_ref[:, hs].astype(BF16)]).astype(o_ref.dtype)


def _ctx_mha(pc, batch):
    return pl.pallas_call(
        _ctx_mha_kernel,
        out_shape=jax.ShapeDtypeStruct((batch * CTX_LEN, D_W), BF16),
        grid=(batch,),
        in_specs=[
            pl.BlockSpec((CTX_LEN, D_W), lambda b: (b, OFF_DQ // D_W)),
            pl.BlockSpec((CTX_LEN, D_W), lambda b: (b, OFF_DK // D_W)),
            pl.BlockSpec((CTX_LEN, D_W), lambda b: (b, OFF_DV // D_W)),
        ],
        out_specs=pl.BlockSpec((CTX_LEN, D_W), lambda b: (b, 0)),
        compiler_params=_cparams(("arbitrary",), 32),
        name="mixer_d_ctx_attn",
    )(pc, pc, pc)


BR_W = (A_CH, B_Q_W, C_W, D_W)
BR_OFF = (0, A_CH, A_CH + B_Q_W, A_CH + B_Q_W + C_W)


def _merge_kernel(ya_ref, yb_ref, yc_ref, yd_ref, w_ref, g0_ref, g1_ref, g2_ref, g3_ref, o_ref):
    z = None
    for y_ref, g_ref, off, width in zip((ya_ref, yb_ref, yc_ref, yd_ref),
                                        (g0_ref, g1_ref, g2_ref, g3_ref), BR_OFF, BR_W):
        t = jnp.dot(y_ref[...], w_ref[off:off + width, :].astype(BF16), preferred_element_type=F32)
        t = jax.nn.sigmoid(g_ref[...]) * t
        z = t if z is None else z + t
    o_ref[...] = z.astype(o_ref.dtype)


def _merge(p, ys, w_branch, layer):
    m = p.shape[0]
    tm, tn = TM_MERGE, TN_PROJ
    gate_specs = [pl.BlockSpec((tm, tn), functools.partial(
        lambda i, j, b: (i, (OFF_G + b * D_MODEL) // tn + j), b=b)) for b in range(4)]
    return pl.pallas_call(
        _merge_kernel,
        out_shape=jax.ShapeDtypeStruct((m, D_MODEL), BF16),
        grid=(m // tm, D_MODEL // tn),
        in_specs=[pl.BlockSpec((tm, w), lambda i, j: (i, 0)) for w in BR_W]
        + [pl.BlockSpec((None, MIX_W, tn), lambda i, j: (layer, 0, j))] + gate_specs,
        out_specs=pl.BlockSpec((tm, tn), lambda i, j: (i, j)),
        compiler_params=_cparams(("arbitrary", "arbitrary"), 48),
        name="branch_merge",
    )(*ys, w_branch, p, p, p, p)


ROUTE_LANES = LANES


def _router_kernel(x_ref, g_ref, sc_ref, sh_ref, wr_ref, xn_ref, route_ref):
    xn = _norm_mod(x_ref[...], g_ref[...], sc_ref[0], sh_ref[0])
    xn_ref[...] = xn.astype(BF16)
    logits = jnp.dot(xn, wr_ref[...], preferred_element_type=F32, precision=lax.Precision.HIGHEST)
    lane = lax.broadcasted_iota(jnp.int32, logits.shape, 1).astype(F32)
    logits = jnp.where(lane < N_EXPERTS, logits, -jnp.inf)
    m1 = logits.max(axis=-1, keepdims=True)
    i1 = jnp.where(logits == m1, lane, float(ROUTE_LANES)).min(axis=-1, keepdims=True)
    rest = jnp.where(lane == i1, -jnp.inf, logits)
    m2 = rest.max(axis=-1, keepdims=True)
    i2 = jnp.where(rest == m2, lane, float(ROUTE_LANES)).min(axis=-1, keepdims=True)
    e2 = jnp.exp(m2 - m1)
    w1 = 1.0 / (1.0 + e2)
    w2 = e2 / (1.0 + e2)
    out = jnp.where(lane == i1, w1, 0.0) + jnp.where(lane == i2, w2, 0.0)
    out = jnp.where(lane == N_EXPERTS, i1, out)
    out = jnp.where(lane == N_EXPERTS + 1, i2, out)
    route_ref[...] = out


def _router(h, g, sc, sh, mod_row, w_router_pad):
    t = h.shape[0]
    tm = 512
    return pl.pallas_call(
        _router_kernel,
        out_shape=(jax.ShapeDtypeStruct((t, D_MODEL), BF16), jax.ShapeDtypeStruct((t, ROUTE_LANES), F32)),
        grid=(t // tm,),
        in_specs=[
            pl.BlockSpec((tm, D_MODEL), lambda i: (i, 0)),
            pl.BlockSpec((1, D_MODEL), lambda i: (0, 0)),
            pl.BlockSpec((1, 1, D_MODEL), lambda i: (mod_row(i, tm), 0, 0)),
            pl.BlockSpec((1, 1, D_MODEL), lambda i: (mod_row(i, tm), 0, 0)),
            pl.BlockSpec((D_MODEL, ROUTE_LANES), lambda i: (0, 0)),
        ],
        out_specs=(pl.BlockSpec((tm, D_MODEL), lambda i: (i, 0)),
                   pl.BlockSpec((tm, ROUTE_LANES), lambda i: (i, 0))),
        compiler_params=_cparams(("arbitrary",), 40),
        name="moe_router",
    )(h, g, sc, sh, w_router_pad)


def _gather_rows_kernel(idx_ref, src_ref, o_ref, sem, *, n_sources):
    rows = o_ref.shape[-3]

    def issue(r, carry):
        for s in range(n_sources):
            pltpu.make_async_copy(src_ref.at[idx_ref[0, s, r]], o_ref.at[s, r], sem).start()
        return carry

    lax.fori_loop(0, rows, issue, 0)

    def drain(r, carry):
        for s in range(n_sources):
            pltpu.make_async_copy(src_ref.at[0], o_ref.at[s, r], sem).wait()
        return carry

    lax.fori_loop(0, rows, drain, 0)


def _gather_rows(src3, idx, n_sources):
    _, s_dim, lanes = src3.shape
    r = idx.shape[1]
    tg = TG_ROWS
    idx_blocks = idx.reshape(n_sources, r // tg, tg).transpose(1, 0, 2)
    return pl.pallas_call(
        functools.partial(_gather_rows_kernel, n_sources=n_sources),
        out_shape=jax.ShapeDtypeStruct((n_sources, r, s_dim, lanes), src3.dtype),
        grid=(r // tg,),
        in_specs=[
            pl.BlockSpec((1, n_sources, tg), lambda i: (i, 0, 0), memory_space=pltpu.SMEM),
            pl.BlockSpec(memory_space=pl.ANY),
        ],
        out_specs=pl.BlockSpec((n_sources, tg, s_dim, lanes), lambda i: (0, i, 0, 0)),
        scratch_shapes=[pltpu.SemaphoreType.DMA],
        compiler_params=_cparams(("arbitrary",), 32),
        name="row_gather",
    )(idx_blocks, src3)


def _moe_up_kernel(te_ref, nu_ref, x_ref, w1_ref, w3_ref, o_ref):
    used = pl.program_id(1) < nu_ref[0]

    @pl.when(used)
    def _():
        x = x_ref[...]
        a = jnp.dot(x, w1_ref[...].astype(BF16), preferred_element_type=F32)
        b = jnp.dot(x, w3_ref[...].astype(BF16), preferred_element_type=F32)
        o_ref[...] = (_silu(a) * b).astype(o_ref.dtype)

    @pl.when(jnp.logical_not(used))
    def _():
        o_ref[...] = jnp.zeros_like(o_ref)


def _moe_up(xs, w1, w3, layer, tile_expert, n_used):
    r = xs.shape[0]
    tm, tn = TM_MOE, TN_PROJ
    w_spec = pl.BlockSpec((None, None, D_MODEL, tn), lambda j, t, te, nu: (layer, te[t], 0, j))
    grid_spec = pltpu.PrefetchScalarGridSpec(
        num_scalar_prefetch=2,
        grid=(FFN_DIM // tn, r // tm),
        in_specs=[pl.BlockSpec((tm, D_MODEL), lambda j, t, te, nu: (jnp.minimum(t, nu[0] - 1), 0)),
                  w_spec, w_spec],
        out_specs=pl.BlockSpec((tm, tn), lambda j, t, te, nu: (t, j)),
    )
    return pl.pallas_call(
        _moe_up_kernel,
        out_shape=jax.ShapeDtypeStruct((r, FFN_DIM), BF16),
        grid_spec=grid_spec,
        compiler_params=_cparams(("arbitrary", "arbitrary"), 48),
        name="moe_up",
    )(tile_expert, n_used, xs, w1, w3)


def _moe_down_kernel(te_ref, nu_ref, x_ref, w_ref, rw_ref, o_ref):
    used = pl.program_id(0) < nu_ref[0]

    @pl.when(used)
    def _():
        acc = jnp.dot(x_ref[...], w_ref[...].astype(BF16), preferred_element_type=F32)
        o_ref[...] = rw_ref[...] * acc

    @pl.when(jnp.logical_not(used))
    def _():
        o_ref[...] = jnp.zeros_like(o_ref)


def _moe_down(hmid, w2, layer, row_w, tile_expert, n_used):
    r = hmid.shape[0]
    tm, tn = TM_MOE, TN_PROJ
    grid_spec = pltpu.PrefetchScalarGridSpec(
        num_scalar_prefetch=2,
        grid=(r // tm, D_MODEL // tn),
        in_specs=[
            pl.BlockSpec((tm, FFN_DIM), lambda t, j, te, nu: (jnp.minimum(t, nu[0] - 1), 0)),
            pl.BlockSpec((None, None, FFN_DIM, tn), lambda t, j, te, nu: (layer, te[t], 0, j)),
            pl.BlockSpec((tm, 1), lambda t, j, te, nu: (t, 0)),
        ],
        out_specs=pl.BlockSpec((tm, tn), lambda t, j, te, nu: (t, j)),
    )
    return pl.pallas_call(
        _moe_down_kernel,
        out_shape=jax.ShapeDtypeStruct((r, D_MODEL), F32),
        grid_spec=grid_spec,
        compiler_params=_cparams(("arbitrary", "arbitrary"), 56),
        name="moe_down",
    )(tile_expert, n_used, hmid, w2, row_w)


def _route_plan(route, tm):
    t = route.shape[0]
    idx = route[:, N_EXPERTS:N_EXPERTS + TOP_K].astype(jnp.int32)
    wts = jnp.take_along_axis(route[:, :N_EXPERTS], idx, axis=1)
    e_flat = idx.T.reshape(-1)
    onehot = (e_flat[:, None] == jnp.arange(N_EXPERTS)[None, :]).astype(jnp.int32)
    counts = onehot.sum(axis=0)
    rank = jnp.take_along_axis(jnp.cumsum(onehot, axis=0) - onehot, e_flat[:, None], axis=1)[:, 0]
    tiles_e = (counts + tm - 1) // tm
    tile_end = jnp.cumsum(tiles_e)
    tile_start = tile_end - tiles_e
    pos = tile_start[e_flat] * tm + rank
    n_tiles = TOP_K * t // tm + N_EXPERTS
    n_used = tile_end[-1]
    tile_ids = jnp.minimum(jnp.arange(n_tiles), n_used - 1)
    tile_expert = jnp.minimum(jnp.searchsorted(tile_end, tile_ids, side="right"), N_EXPERTS - 1)
    r = n_tiles * tm
    row_token = jnp.zeros((r,), jnp.int32).at[pos].set(jnp.tile(jnp.arange(t, dtype=jnp.int32), TOP_K))
    row_w = jnp.zeros((r,), F32).at[pos].set(wts.T.reshape(-1))
    return (pos.reshape(TOP_K, t).astype(jnp.int32), row_token, row_w.reshape(r, 1),
            tile_expert.astype(jnp.int32), n_used.reshape(1).astype(jnp.int32))


def _final_kernel(h_ref, y0_ref, y1_ref, gt_ref, g_ref, o_ref):
    h = h_ref[...] + gt_ref[0] * (y0_ref[0] + y1_ref[0])
    o_ref[...] = h * lax.rsqrt(jnp.mean(h * h, axis=-1, keepdims=True) + NORM_EPS) * g_ref[...]


def _final(h, y01, gt, mod_row, g_final):
    t = h.shape[0]
    tm = 512
    return pl.pallas_call(
        _final_kernel,
        out_shape=jax.ShapeDtypeStruct((t, D_MODEL), F32),
        grid=(t // tm,),
        in_specs=[
            pl.BlockSpec((tm, D_MODEL), lambda i: (i, 0)),
            pl.BlockSpec((1, tm, D_MODEL), lambda i: (0, i, 0)),
            pl.BlockSpec((1, tm, D_MODEL), lambda i: (1, i, 0)),
            pl.BlockSpec((1, 1, D_MODEL), lambda i: (mod_row(i, tm), 0, 0)),
            pl.BlockSpec((1, D_MODEL), lambda i: (0, 0)),
        ],
        out_specs=pl.BlockSpec((tm, D_MODEL), lambda i: (i, 0)),
        compiler_params=_cparams(("arbitrary",), 48),
        name="final_residual_norm",
    )(h, y01, y01, gt, g_final)


def _rope_tables():
    pos = jnp.arange(SEQ)
    f = HEAD_DIM // 4
    inv = ROPE_BASE ** (-jnp.arange(f, dtype=F32) / f)
    ang_row = (pos // GRID_W).astype(F32)[:, None] * inv[None, :]
    ang_col = (pos % GRID_W).astype(F32)[:, None] * inv[None, :]
    ang = jnp.concatenate([ang_row, ang_row, ang_col, ang_col], axis=1)
    cos, sin = jnp.cos(ang), jnp.sin(ang)
    first = jnp.asarray(((np.arange(HEAD_DIM) // f) % 2 == 0)[None, :])
    return cos, jnp.where(first, -sin, 0.0), jnp.where(first, 0.0, sin)


def kernel(x, c, ctx, c_ctx, w_ada, b_ada, g_mix, g_ffn, w_in, a_norm_g, a_ws, a_bs, b_sink, c_conv,
           d_rpb, w_branch, w_out, ffn_w1, ffn_w3, ffn_w2, w_router, moe_w1, moe_w3, moe_w2, g_final):
    batch, seq, _ = x.shape
    depth = w_in.shape[0]
    assert seq == SEQ and ctx.shape[1] == CTX_LEN and batch + 1 <= MOD_ROWS
    t = batch * seq
    tc = batch * CTX_LEN

    cond = jnp.zeros((MOD_ROWS, D_MODEL), F32).at[:batch].set(c).at[batch].set(c_ctx)
    mods = _ada(cond, w_ada, b_ada)
    rope_tabs = _rope_tables()
    conv_w8 = jnp.zeros((depth, 8, C_W), F32).at[:, :3].set(c_conv)

    def lat_row(i, tm=TM_PROJ):
        return (i * tm) // SEQ

    def ctx_row(i, tm=TM_PROJ):
        return batch

    ffn_w2 = ffn_w2.astype(BF16)
    moe_w2 = moe_w2.astype(BF16)

    h = x.reshape(t, D_MODEL)
    hc = ctx.reshape(tc, D_MODEL)
    out = None
    for layer in range(depth):
        last = layer == depth - 1
        sh1, sc1, gt1, sh2, sc2, gt2 = [m.reshape(MOD_ROWS, 1, D_MODEL)
                                        for m in jnp.split(mods[layer], 6, axis=-1)]
        g_mix_l = g_mix[layer].reshape(1, D_MODEL)
        g_ffn_l = g_ffn[layer].reshape(1, D_MODEL)
        bias_tab = _na_bias_table(d_rpb[layer])

        if last:
            pc = _norm_mm(hc, g_mix_l, sc1, sh1, ctx_row, (w_in,), layer,
                          lambda j: jnp.where(j == 0, OFF_BK // TN_PROJ, OFF_DK // TN_PROJ - 1 + j),
                          3 * TN_PROJ, F32, TM_PROJ, "ctx_kv_proj")
            kcb, vcb, kcd, vcd = 0, 1, 1, 2
        else:
            pc = _norm_mm(hc, g_mix_l, sc1, sh1, ctx_row, (w_in,), layer, lambda j: j,
                          IN_W, F32, TM_PROJ, "ctx_in_proj")
            kcb, vcb, kcd, vcd = OFF_BK // B_KV_W, OFF_BV // B_KV_W, OFF_DK // D_W, OFF_DV // D_W

        px = _norm_mm(h, g_mix_l, sc1, sh1, lat_row, (w_in,), layer, lambda j: j,
                      IN_W, F32, TM_PROJ, "in_proj")
        ys = (
            _gmlp(px, a_norm_g, a_ws, a_bs, layer),
            _win_attn(px, pc, kcb, vcb, b_sink[layer], rope_tabs, batch),
            _short_conv(px, conv_w8, layer, SEQ),
            _na_attn(px, pc, kcd, vcd, bias_tab, batch),
        )
        z = _merge(px, ys, w_branch, layer)
        h = _res_mm(z, w_out, layer, h, gt1, lambda i: lat_row(i, TM_DOWN), TM_DOWN, "out_proj")

        if not last:
            ysc = (
                _gmlp(pc, a_norm_g, a_ws, a_bs, layer),
                _ctx_gqa(pc, b_sink[layer], batch),
                _short_conv(pc, conv_w8, layer, CTX_LEN),
                _ctx_mha(pc, batch),
            )
            zc = _merge(pc, ysc, w_branch, layer)
            hc = _res_mm(zc, w_out, layer, hc, gt1, ctx_row, TM_DOWN, "ctx_out_proj")

        j = layer // 2
        if layer % 2 == 0:
            hm = _norm_mm(h, g_ffn_l, sc2, sh2, lat_row, (ffn_w1, ffn_w3), j, lambda n: n,
                          FFN_DIM, BF16, TM_PROJ, "ffn_up")
            h = _res_mm(hm, ffn_w2, j, h, gt2, lambda i: lat_row(i, TM_DOWN), TM_DOWN, "ffn_down")
            if not last:
                hmc = _norm_mm(hc, g_ffn_l, sc2, sh2, ctx_row, (ffn_w1, ffn_w3), j, lambda n: n,
                               FFN_DIM, BF16, TM_PROJ, "ctx_ffn_up")
                hc = _res_mm(hmc, ffn_w2, j, hc, gt2, ctx_row, TM_DOWN, "ctx_ffn_down")
        else:
            assert last, "context MoE path is only needed when an odd layer is not the last"
            wr_pad = jnp.zeros((D_MODEL, ROUTE_LANES), F32).at[:, :N_EXPERTS].set(w_router[j])
            xn, route = _router(h, g_ffn_l, sc2, sh2, lat_row, wr_pad)
            pos, row_token, row_w, tile_expert, n_used = _route_plan(route, TM_MOE)
            xs = _gather_rows(xn.reshape(t, D_MODEL // LANES, LANES), row_token[None, :], 1)
            xs = xs.reshape(-1, D_MODEL)
            hm = _moe_up(xs, moe_w1, moe_w3, j, tile_expert, n_used)
            ye = _moe_down(hm, moe_w2, j, row_w, tile_expert, n_used)
            y01 = _gather_rows(ye.reshape(-1, D_MODEL // LANES, LANES), pos, TOP_K)
            y01 = y01.reshape(TOP_K, t, D_MODEL)
            if last:
                out = _final(h, y01, gt2, lat_row, g_final.reshape(1, D_MODEL))

    if out is None:
        raise NotImplementedError("this kernel expects the last layer to be a MoE layer")
    return out.reshape(batch, seq, D_MODEL)
```

```python
import functools

import jax
import jax.numpy as jnp
import numpy as np
from jax import lax
from jax.experimental import pallas as pl
from jax.experimental.pallas import tpu as pltpu

F32 = jnp.float32
BF16 = jnp.bfloat16

D_MODEL = 2048
SEQ = 2048
CTX_LEN = 256
GRID_W = 64
HEAD_DIM = 128
ROPE_BASE = 10000.0
NORM_EPS = 1e-6
NEG_INF = -1e30

CHUNK = 128
A_GROUPS = 4
A_CH = 512
B_HEADS = 8
B_KV_HEADS = 2
B_GROUP = B_HEADS // B_KV_HEADS
B_WINDOW = 128
B_Q_W = B_HEADS * HEAD_DIM
B_KV_W = B_KV_HEADS * HEAD_DIM
C_W = 512
D_HEADS = 4
D_W = D_HEADS * HEAD_DIM
NA_KH = 8
NA_KW = 16

OFF_AU = 0
OFF_AV = OFF_AU + A_CH
OFF_BQ = OFF_AV + A_CH
OFF_BK = OFF_BQ + B_Q_W
OFF_BV = OFF_BK + B_KV_W
OFF_CB = OFF_BV + B_KV_W
OFF_CC = OFF_CB + C_W
OFF_CH = OFF_CC + C_W
OFF_DQ = OFF_CH + C_W
OFF_DK = OFF_DQ + D_W
OFF_DV = OFF_DK + D_W
OFF_G = OFF_DV + D_W
IN_W = OFF_G + 4 * D_MODEL
MIX_W = A_CH + B_Q_W + C_W + D_W

FFN_DIM = 7168
N_EXPERTS = 8
TOP_K = 2

LANES = 128
BF16_SUBLANES = 16
MIB = 2**20

TM_PROJ = 1024
TN_KV = 512
TN_IN = 1536
TN_FFN = 512
TM_OUT, TN_OUT = 1024, 1024
TM_DOWN, TN_DOWN = 512, 512
TM_MERGE, TN_MERGE = 1024, 512
TM_MOE = 512
TN_MOE = 512
TG_ROWS = 512
GATHER_UNROLL = 8
MOD_ROWS = 16


def _cparams(sem, vmem_mib):
    return pltpu.CompilerParams(dimension_semantics=sem, vmem_limit_bytes=vmem_mib * MIB)


def _silu(a):
    return a * jax.nn.sigmoid(a)


def _ada_kernel(c_ref, w_ref, b_ref, o_ref):
    c = c_ref[...]
    s = _silu(c).astype(BF16)
    o_ref[...] = jnp.dot(s, w_ref[...].astype(BF16), preferred_element_type=F32) + b_ref[...]


def _ada(cond, w_ada, b_ada):
    depth, _, n = w_ada.shape
    tn = 1024
    return pl.pallas_call(
        _ada_kernel,
        out_shape=jax.ShapeDtypeStruct((depth, MOD_ROWS, n), F32),
        grid=(depth, n // tn),
        in_specs=[
            pl.BlockSpec((MOD_ROWS, D_MODEL), lambda l, j: (0, 0)),
            pl.BlockSpec((None, D_MODEL, tn), lambda l, j: (l, 0, j)),
            pl.BlockSpec((None, 1, tn), lambda l, j: (l, 0, j)),
        ],
        out_specs=pl.BlockSpec((None, MOD_ROWS, tn), lambda l, j: (l, 0, j)),
        compiler_params=_cparams(("arbitrary", "arbitrary"), 40),
        name="ada_modulation",
    )(cond, w_ada, b_ada.reshape(depth, 1, n))


def _norm_mod(x, g, sc, sh):
    y = x * lax.rsqrt(jnp.mean(x * x, axis=-1, keepdims=True) + NORM_EPS) * g
    return y * (1.0 + sc) + sh


def _norm_mm_kernel(x_ref, g_ref, sc_ref, sh_ref, *rest, swiglu):
    n_w = 2 if swiglu else 1
    w_refs, o_ref, xn_ref = rest[:n_w], rest[n_w], rest[n_w + 1]

    @pl.when(pl.program_id(1) == 0)
    def _():
        xn_ref[...] = _norm_mod(x_ref[...], g_ref[...], sc_ref[0], sh_ref[0]).astype(BF16)

    xn = xn_ref[...]
    a = jnp.dot(xn, w_refs[0][...].astype(BF16), preferred_element_type=F32)
    if swiglu:
        b = jnp.dot(xn, w_refs[1][...].astype(BF16), preferred_element_type=F32)
        a = _silu(a) * b
    o_ref[...] = a.astype(o_ref.dtype)


def _norm_mm(x, g, sc, sh, mod_row, weights, layer, col_block, n_out, tm, tn, name):
    m = x.shape[0]
    swiglu = len(weights) == 2
    w_spec = pl.BlockSpec((None, D_MODEL, tn), lambda i, j: (layer, 0, col_block(j)))
    return pl.pallas_call(
        functools.partial(_norm_mm_kernel, swiglu=swiglu),
        out_shape=jax.ShapeDtypeStruct((m, n_out), BF16),
        grid=(m // tm, n_out // tn),
        in_specs=[
            pl.BlockSpec((tm, D_MODEL), lambda i, j: (i, 0)),
            pl.BlockSpec((1, D_MODEL), lambda i, j: (0, 0)),
            pl.BlockSpec((1, 1, D_MODEL), lambda i, j: (mod_row(i, tm), 0, 0)),
            pl.BlockSpec((1, 1, D_MODEL), lambda i, j: (mod_row(i, tm), 0, 0)),
        ] + [w_spec] * len(weights),
        out_specs=pl.BlockSpec((tm, tn), lambda i, j: (i, j)),
        scratch_shapes=[pltpu.VMEM((tm, D_MODEL), BF16)],
        compiler_params=_cparams(("arbitrary", "arbitrary"), 56),
        name=name,
    )(x, g, sc, sh, *weights)


def _res_mm_kernel(x_ref, w_ref, res_ref, gt_ref, o_ref):
    acc = jnp.dot(x_ref[...], w_ref[...].astype(BF16), preferred_element_type=F32)
    o_ref[...] = res_ref[...] + gt_ref[0] * acc


def _res_mm(x, w, layer, res, gt, mod_row, tm, tn, name):
    m, k = x.shape
    n = res.shape[1]
    return pl.pallas_call(
        _res_mm_kernel,
        out_shape=jax.ShapeDtypeStruct((m, n), F32),
        grid=(m // tm, n // tn),
        in_specs=[
            pl.BlockSpec((tm, k), lambda i, j: (i, 0)),
            pl.BlockSpec((None, k, tn), lambda i, j: (layer, 0, j)),
            pl.BlockSpec((tm, tn), lambda i, j: (i, j)),
            pl.BlockSpec((1, 1, tn), lambda i, j: (mod_row(i, tm), 0, j)),
        ],
        out_specs=pl.BlockSpec((tm, tn), lambda i, j: (i, j)),
        compiler_params=_cparams(("arbitrary", "arbitrary"), 56),
        name=name,
    )(x, w, res, gt)


def _gmlp_kernel(u_ref, v_ref, g_ref, ws_ref, bs_ref, o_ref):
    v = v_ref[...].astype(F32)
    vn = (v * lax.rsqrt(jnp.mean(v * v, axis=-1, keepdims=True) + NORM_EPS) * g_ref[...]).astype(BF16)
    rows = v.shape[0]
    for gi in range(A_GROUPS):
        w = ws_ref[gi].astype(BF16)
        bias = bs_ref[gi]
        cs = slice(gi * LANES, (gi + 1) * LANES)
        for c in range(rows // CHUNK):
            rs = slice(c * CHUNK, (c + 1) * CHUNK)
            s = jnp.dot(w, vn[rs, cs], preferred_element_type=F32) + bias
            o_ref[rs, cs] = (u_ref[rs, cs].astype(F32) * s).astype(o_ref.dtype)


def _gmlp(p, norm_g, ws, bs, layer):
    m = p.shape[0]
    tm = 512
    return pl.pallas_call(
        _gmlp_kernel,
        out_shape=jax.ShapeDtypeStruct((m, A_CH), BF16),
        grid=(m // tm,),
        in_specs=[
            pl.BlockSpec((tm, A_CH), lambda i: (i, OFF_AU // A_CH)),
            pl.BlockSpec((tm, A_CH), lambda i: (i, OFF_AV // A_CH)),
            pl.BlockSpec((1, A_CH), lambda i: (0, 0)),
            pl.BlockSpec((None, A_GROUPS, CHUNK, CHUNK), lambda i: (layer, 0, 0, 0)),
            pl.BlockSpec((None, A_GROUPS, CHUNK, 1), lambda i: (layer, 0, 0, 0)),
        ],
        out_specs=pl.BlockSpec((tm, A_CH), lambda i: (i, 0)),
        compiler_params=_cparams(("arbitrary",), 32),
        name="mixer_a_gmlp",
    )(p, p, norm_g[layer].reshape(1, A_CH), ws, bs.reshape(bs.shape + (1,)))


CONV_ROWS = 256
HALO = BF16_SUBLANES


def _conv_kernel(b_ref, c_ref, h_ref, cp_ref, hp_ref, cn_ref, hn_ref, w_ref, o_ref, *, tiles_per_seq):
    i = pl.program_id(0)
    z = c_ref[...].astype(F32) * h_ref[...].astype(F32)
    n = z.shape[0]
    first = (i % tiles_per_seq) == 0
    last = (i % tiles_per_seq) == tiles_per_seq - 1
    zp = jnp.where(first, 0.0, cp_ref[HALO - 1:HALO, :].astype(F32) * hp_ref[HALO - 1:HALO, :].astype(F32))
    zn = jnp.where(last, 0.0, cn_ref[0:1, :].astype(F32) * hn_ref[0:1, :].astype(F32))
    pos = lax.broadcasted_iota(jnp.int32, z.shape, 0)
    z_prev = jnp.where(pos == 0, zp, pltpu.roll(z, 1, axis=0))
    z_next = jnp.where(pos == n - 1, zn, pltpu.roll(z, n - 1, axis=0))
    w = w_ref[...]
    conv = z_prev * w[0:1] + z * w[1:2] + z_next * w[2:3]
    o_ref[...] = (b_ref[...].astype(F32) * conv).astype(o_ref.dtype)


def _short_conv(p, conv_w8, layer, seq_len):
    m = p.shape[0]
    tr = CONV_ROWS
    per = tr // HALO
    n_halo = m // HALO

    def prev_blk(col):
        return lambda i: (jnp.maximum(i * per - 1, 0), col)

    def next_blk(col):
        return lambda i: (jnp.minimum((i + 1) * per, n_halo - 1), col)

    cc, ch = OFF_CC // C_W, OFF_CH // C_W
    return pl.pallas_call(
        functools.partial(_conv_kernel, tiles_per_seq=seq_len // tr),
        out_shape=jax.ShapeDtypeStruct((m, C_W), BF16),
        grid=(m // tr,),
        in_specs=[
            pl.BlockSpec((tr, C_W), lambda i: (i, OFF_CB // C_W)),
            pl.BlockSpec((tr, C_W), lambda i: (i, cc)),
            pl.BlockSpec((tr, C_W), lambda i: (i, ch)),
            pl.BlockSpec((HALO, C_W), prev_blk(cc)),
            pl.BlockSpec((HALO, C_W), prev_blk(ch)),
            pl.BlockSpec((HALO, C_W), next_blk(cc)),
            pl.BlockSpec((HALO, C_W), next_blk(ch)),
            pl.BlockSpec((None, 8, C_W), lambda i: (layer, 0, 0)),
        ],
        out_specs=pl.BlockSpec((tr, C_W), lambda i: (i, 0)),
        compiler_params=_cparams(("arbitrary",), 32),
        name="mixer_c_conv",
    )(p, p, p, p, p, p, p, conv_w8)


def _rope(x, cos, sin_lo, sin_hi):
    x = x.astype(F32)
    return x * cos + pltpu.roll(x, LANES - 32, axis=1) * sin_lo + pltpu.roll(x, 32, axis=1) * sin_hi


def _softmax_av(s_parts, v_parts, extra_logit=None):
    m = s_parts[0].max(axis=-1, keepdims=True)
    for s in s_parts[1:]:
        m = jnp.maximum(m, s.max(axis=-1, keepdims=True))
    if extra_logit is not None:
        m = jnp.maximum(m, extra_logit)
    denom = jnp.exp(extra_logit - m) if extra_logit is not None else 0.0
    acc = None
    for s, v in zip(s_parts, v_parts):
        e = jnp.exp(s - m)
        denom = denom + e.sum(axis=-1, keepdims=True)
        pv = jnp.dot(e.astype(BF16), v, preferred_element_type=F32)
        acc = pv if acc is None else acc + pv
    return acc / denom


def _qk(q, k):
    return lax.dot_general(q, k, (((1,), (1,)), ((), ())), preferred_element_type=F32)


def _sink_column(sink_ref, h, rows_per_head):
    rid = lax.broadcasted_iota(jnp.int32, (B_GROUP * rows_per_head, 1), 0) // rows_per_head
    col = jnp.zeros((B_GROUP * rows_per_head, 1), F32)
    for g in range(B_GROUP):
        col = jnp.where(rid == g, sink_ref[h * B_GROUP + g], col)
    return col


def _win_attn_kernel(sink_ref, q_ref, k_ref, v_ref, kc_ref, vc_ref, cos_ref, slo_ref, shi_ref, o_ref):
    n = pl.program_id(1)
    blk = B_WINDOW
    band = 3 * blk
    q0 = pl.multiple_of(n * blk, blk)
    k0 = pl.multiple_of(jnp.clip(n - 1, 0, SEQ // blk - 3) * blk, blk)
    scale = HEAD_DIM ** -0.5

    cos_q, slo_q, shi_q = cos_ref[pl.ds(q0, blk), :], slo_ref[pl.ds(q0, blk), :], shi_ref[pl.ds(q0, blk), :]
    cos_k, slo_k, shi_k = cos_ref[pl.ds(k0, band), :], slo_ref[pl.ds(k0, band), :], shi_ref[pl.ds(k0, band), :]

    qpos = q0 + lax.broadcasted_iota(jnp.int32, (B_GROUP * blk, band), 0) % blk
    kpos = k0 + lax.broadcasted_iota(jnp.int32, (B_GROUP * blk, band), 1)
    valid = jnp.abs(kpos - qpos) <= B_WINDOW

    for h in range(B_KV_HEADS):
        hs = slice(h * HEAD_DIM, (h + 1) * HEAD_DIM)
        k_loc = _rope(k_ref[pl.ds(k0, band), hs], cos_k, slo_k, shi_k).astype(BF16)
        v_loc = v_ref[pl.ds(k0, band), hs]
        k_ctx = kc_ref[:, hs]
        v_ctx = vc_ref[:, hs]
        qs = []
        for g in range(B_GROUP):
            c0 = (h * B_GROUP + g) * HEAD_DIM
            qs.append(_rope(q_ref[:, c0:c0 + HEAD_DIM], cos_q, slo_q, shi_q).astype(BF16))
        qst = jnp.concatenate(qs, axis=0)
        s_loc = jnp.where(valid, _qk(qst, k_loc) * scale, NEG_INF)
        s_ctx = _qk(qst, k_ctx) * scale
        out = _softmax_av([s_ctx, s_loc], [v_ctx, v_loc], _sink_column(sink_ref, h, blk))
        for g in range(B_GROUP):
            c0 = (h * B_GROUP + g) * HEAD_DIM
            o_ref[:, c0:c0 + HEAD_DIM] = out[g * blk:(g + 1) * blk].astype(o_ref.dtype)


def _win_attn(p, pc, kc_blk, vc_blk, sink, rope_tabs, batch):
    nb = SEQ // B_WINDOW
    kvw = B_KV_W
    return pl.pallas_call(
        _win_attn_kernel,
        out_shape=jax.ShapeDtypeStruct((batch * SEQ, B_Q_W), BF16),
        grid=(batch, nb),
        in_specs=[
            pl.BlockSpec(memory_space=pltpu.SMEM),
            pl.BlockSpec((B_WINDOW, B_Q_W), lambda b, n: (b * nb + n, OFF_BQ // B_Q_W)),
            pl.BlockSpec((SEQ, kvw), lambda b, n: (b, OFF_BK // kvw)),
            pl.BlockSpec((SEQ, kvw), lambda b, n: (b, OFF_BV // kvw)),
            pl.BlockSpec((CTX_LEN, kvw), lambda b, n: (b, kc_blk)),
            pl.BlockSpec((CTX_LEN, kvw), lambda b, n: (b, vc_blk)),
            pl.BlockSpec((SEQ, HEAD_DIM), lambda b, n: (0, 0)),
            pl.BlockSpec((SEQ, HEAD_DIM), lambda b, n: (0, 0)),
            pl.BlockSpec((SEQ, HEAD_DIM), lambda b, n: (0, 0)),
        ],
        out_specs=pl.BlockSpec((B_WINDOW, B_Q_W), lambda b, n: (b * nb + n, 0)),
        compiler_params=_cparams(("arbitrary", "arbitrary"), 40),
        name="mixer_b_window_attn",
    )(sink, p, p, p, pc, pc, *rope_tabs)


def _ctx_gqa_kernel(sink_ref, q_ref, kc_ref, vc_ref, o_ref):
    scale = HEAD_DIM ** -0.5
    lc = q_ref.shape[0]
    for h in range(B_KV_HEADS):
        hs = slice(h * HEAD_DIM, (h + 1) * HEAD_DIM)
        qst = jnp.concatenate(
            [q_ref[:, (h * B_GROUP + g) * HEAD_DIM:(h * B_GROUP + g + 1) * HEAD_DIM] for g in range(B_GROUP)],
            axis=0)
        out = _softmax_av([_qk(qst, kc_ref[:, hs]) * scale], [vc_ref[:, hs]], _sink_column(sink_ref, h, lc))
        for g in range(B_GROUP):
            c0 = (h * B_GROUP + g) * HEAD_DIM
            o_ref[:, c0:c0 + HEAD_DIM] = out[g * lc:(g + 1) * lc].astype(o_ref.dtype)


def _ctx_gqa(pc, sink, batch):
    kvw = B_KV_W
    return pl.pallas_call(
        _ctx_gqa_kernel,
        out_shape=jax.ShapeDtypeStruct((batch * CTX_LEN, B_Q_W), BF16),
        grid=(batch,),
        in_specs=[
            pl.BlockSpec(memory_space=pltpu.SMEM),
            pl.BlockSpec((CTX_LEN, B_Q_W), lambda b: (b, OFF_BQ // B_Q_W)),
            pl.BlockSpec((CTX_LEN, kvw), lambda b: (b, OFF_BK // kvw)),
            pl.BlockSpec((CTX_LEN, kvw), lambda b: (b, OFF_BV // kvw)),
        ],
        out_specs=pl.BlockSpec((CTX_LEN, B_Q_W), lambda b: (b, 0)),
        compiler_params=_cparams(("arbitrary",), 32),
        name="mixer_b_ctx_attn",
    )(sink, pc, pc, pc)


NA_ROWS_PER_STEP = 4
GRID_H = SEQ // GRID_W
NA_LOC = NA_KH * GRID_W


def _na_kernel(q_ref, k_ref, v_ref, kc_ref, vc_ref, bias_ref, o_ref):
    step = pl.program_id(1)
    scale = HEAD_DIM ** -0.5
    for rr in range(NA_ROWS_PER_STEP):
        r = step * NA_ROWS_PER_STEP + rr
        rs = jnp.clip(r - NA_KH // 2, 0, GRID_H - NA_KH)
        k0 = pl.multiple_of(rs * GRID_W, GRID_W)
        d = r - rs
        qrows = slice(rr * GRID_W, (rr + 1) * GRID_W)
        for h in range(D_HEADS):
            hs = slice(h * HEAD_DIM, (h + 1) * HEAD_DIM)
            q = q_ref[qrows, hs]
            s_loc = _qk(q, k_ref[pl.ds(k0, NA_LOC), hs]) * scale + bias_ref[h, d]
            s_ctx = _qk(q, kc_ref[:, hs]) * scale
            out = _softmax_av([s_ctx, s_loc], [vc_ref[:, hs], v_ref[pl.ds(k0, NA_LOC), hs]])
            o_ref[qrows, hs] = out.astype(o_ref.dtype)


def _na_bias_table(rpb):
    rpb = rpb.astype(F32)
    rows = []
    for c in range(GRID_W):
        cstart = min(max(c - NA_KW // 2, 0), GRID_W - NA_KW)
        first_dc = cstart - c + NA_KW - 1
        win = rpb[:, :, first_dc:first_dc + NA_KW]
        rows.append(jnp.pad(win, ((0, 0), (0, 0), (cstart, GRID_W - NA_KW - cstart)), constant_values=NEG_INF))
    tab0 = jnp.stack(rows, axis=2)
    slabs = []
    for d in range(NA_KH):
        sl = tab0[:, NA_KH - 1 - d:2 * NA_KH - 1 - d]
        slabs.append(jnp.transpose(sl, (0, 2, 1, 3)).reshape(D_HEADS, GRID_W, NA_LOC))
    return jnp.stack(slabs, axis=1)


def _na_attn(p, pc, kc_blk, vc_blk, bias_tab, batch):
    rows_q = NA_ROWS_PER_STEP * GRID_W
    steps = SEQ // rows_q
    return pl.pallas_call(
        _na_kernel,
        out_shape=jax.ShapeDtypeStruct((batch * SEQ, D_W), BF16),
        grid=(batch, steps),
        in_specs=[
            pl.BlockSpec((rows_q, D_W), lambda b, s: (b * steps + s, OFF_DQ // D_W)),
            pl.BlockSpec((SEQ, D_W), lambda b, s: (b, OFF_DK // D_W)),
            pl.BlockSpec((SEQ, D_W), lambda b, s: (b, OFF_DV // D_W)),
            pl.BlockSpec((CTX_LEN, D_W), lambda b, s: (b, kc_blk)),
            pl.BlockSpec((CTX_LEN, D_W), lambda b, s: (b, vc_blk)),
            pl.BlockSpec((D_HEADS, NA_KH, GRID_W, NA_LOC), lambda b, s: (0, 0, 0, 0)),
        ],
        out_specs=pl.BlockSpec((rows_q, D_W), lambda b, s: (b * steps + s, 0)),
        compiler_params=_cparams(("arbitrary", "arbitrary"), 48),
        name="mixer_d_neighborhood_attn",
    )(p, p, p, pc, pc, bias_tab)


def _ctx_mha_kernel(q_ref, kc_ref, vc_ref, o_ref):
    scale = HEAD_DIM ** -0.5
    for h in range(D_HEADS):
        hs = slice(h * HEAD_DIM, (h + 1) * HEAD_DIM)
        s = _qk(q_ref[:, hs], kc_ref[:, hs]) * scale
        o_ref[:, hs] = _softmax_av([s], [vc_ref[:, hs]]).astype(o_ref.dtype)


def _ctx_mha(pc, batch):
    return pl.pallas_call(
        _ctx_mha_kernel,
        out_shape=jax.ShapeDtypeStruct((batch * CTX_LEN, D_W), BF16),
        grid=(batch,),
        in_specs=[
            pl.BlockSpec((CTX_LEN, D_W), lambda b: (b, OFF_DQ // D_W)),
            pl.BlockSpec((CTX_LEN, D_W), lambda b: (b, OFF_DK // D_W)),
            pl.BlockSpec((CTX_LEN, D_W), lambda b: (b, OFF_DV // D_W)),
        ],
        out_specs=pl.BlockSpec((CTX_LEN, D_W), lambda b: (b, 0)),
        compiler_params=_cparams(("arbitrary",), 32),
        name="mixer_d_ctx_attn",
    )(pc, pc, pc)


BR_W = (A_CH, B_Q_W, C_W, D_W)
BR_OFF = (0, A_CH, A_CH + B_Q_W, A_CH + B_Q_W + C_W)


def _merge_kernel(ya_ref, yb_ref, yc_ref, yd_ref, w_ref, g0_ref, g1_ref, g2_ref, g3_ref, o_ref):
    z = None
    for y_ref, g_ref, off, width in zip((ya_ref, yb_ref, yc_ref, yd_ref),
                                        (g0_ref, g1_ref, g2_ref, g3_ref), BR_OFF, BR_W):
        t = jnp.dot(y_ref[...], w_ref[off:off + width, :].astype(BF16), preferred_element_type=F32)
        t = jax.nn.sigmoid(g_ref[...].astype(F32)) * t
        z = t if z is None else z + t
    o_ref[...] = z.astype(o_ref.dtype)


def _merge(p, ys, w_branch, layer):
    m = p.shape[0]
    tm, tn = TM_MERGE, TN_MERGE
    gate_specs = [pl.BlockSpec((tm, tn), functools.partial(
        lambda i, j, b: (i, (OFF_G + b * D_MODEL) // tn + j), b=b)) for b in range(4)]
    return pl.pallas_call(
        _merge_kernel,
        out_shape=jax.ShapeDtypeStruct((m, D_MODEL), BF16),
        grid=(m // tm, D_MODEL // tn),
        in_specs=[pl.BlockSpec((tm, w), lambda i, j: (i, 0)) for w in BR_W]
        + [pl.BlockSpec((None, MIX_W, tn), lambda i, j: (layer, 0, j))] + gate_specs,
        out_specs=pl.BlockSpec((tm, tn), lambda i, j: (i, j)),
        compiler_params=_cparams(("arbitrary", "arbitrary"), 48),
        name="branch_merge",
    )(*ys, w_branch, p, p, p, p)


ROUTE_LANES = LANES
LANE_IDX = N_EXPERTS
LANE_WT = N_EXPERTS + TOP_K


def _router_kernel(x_ref, g_ref, sc_ref, sh_ref, wr_ref, xn_ref, route_ref):
    xn = _norm_mod(x_ref[...], g_ref[...], sc_ref[0], sh_ref[0])
    xn_ref[...] = xn.astype(BF16)
    logits = jnp.dot(xn, wr_ref[...], preferred_element_type=F32, precision=lax.Precision.HIGHEST)
    lane = lax.broadcasted_iota(jnp.int32, logits.shape, 1).astype(F32)
    logits = jnp.where(lane < N_EXPERTS, logits, -jnp.inf)
    m1 = logits.max(axis=-1, keepdims=True)
    i1 = jnp.where(logits == m1, lane, float(ROUTE_LANES)).min(axis=-1, keepdims=True)
    rest = jnp.where(lane == i1, -jnp.inf, logits)
    m2 = rest.max(axis=-1, keepdims=True)
    i2 = jnp.where(rest == m2, lane, float(ROUTE_LANES)).min(axis=-1, keepdims=True)
    e2 = jnp.exp(m2 - m1)
    w1 = 1.0 / (1.0 + e2)
    w2 = e2 / (1.0 + e2)
    out = jnp.where(lane == LANE_IDX, i1, 0.0)
    out = jnp.where(lane == LANE_IDX + 1, i2, out)
    out = jnp.where(lane == LANE_WT, w1, out)
    out = jnp.where(lane == LANE_WT + 1, w2, out)
    route_ref[...] = out


def _router(h, g, sc, sh, mod_row, w_router_pad):
    t = h.shape[0]
    tm = 512
    return pl.pallas_call(
        _router_kernel,
        out_shape=(jax.ShapeDtypeStruct((t, D_MODEL), BF16), jax.ShapeDtypeStruct((t, ROUTE_LANES), F32)),
        grid=(t // tm,),
        in_specs=[
            pl.BlockSpec((tm, D_MODEL), lambda i: (i, 0)),
            pl.BlockSpec((1, D_MODEL), lambda i: (0, 0)),
            pl.BlockSpec((1, 1, D_MODEL), lambda i: (mod_row(i, tm), 0, 0)),
            pl.BlockSpec((1, 1, D_MODEL), lambda i: (mod_row(i, tm), 0, 0)),
            pl.BlockSpec((D_MODEL, ROUTE_LANES), lambda i: (0, 0)),
        ],
        out_specs=(pl.BlockSpec((tm, D_MODEL), lambda i: (i, 0)),
                   pl.BlockSpec((tm, ROUTE_LANES), lambda i: (i, 0))),
        compiler_params=_cparams(("arbitrary",), 40),
        name="moe_router",
    )(h, g, sc, sh, w_router_pad)


def _gather_rows_kernel(idx_ref, src_ref, o_ref, sem, *, n_sources):
    rows = o_ref.shape[-3]

    def row_copy(s, r, src_row):
        return pltpu.make_async_copy(src_ref.at[src_row], o_ref.at[s, r], sem)

    def issue(i, carry):
        for u in range(GATHER_UNROLL):
            r = i * GATHER_UNROLL + u
            for s in range(n_sources):
                row_copy(s, r, idx_ref[0, s, r]).start(priority=(u * n_sources + s) % 2)
        return carry

    lax.fori_loop(0, rows // GATHER_UNROLL, issue, 0)

    def drain(i, carry):
        for u in range(GATHER_UNROLL):
            for s in range(n_sources):
                row_copy(s, i * GATHER_UNROLL + u, 0).wait()
        return carry

    lax.fori_loop(0, rows // GATHER_UNROLL, drain, 0)


def _gather_rows(src3, idx, n_sources):
    _, s_dim, lanes = src3.shape
    r = idx.shape[1]
    tg = TG_ROWS
    idx_blocks = idx.reshape(n_sources, r // tg, tg).transpose(1, 0, 2)
    return pl.pallas_call(
        functools.partial(_gather_rows_kernel, n_sources=n_sources),
        out_shape=jax.ShapeDtypeStruct((n_sources, r, s_dim, lanes), src3.dtype),
        grid=(r // tg,),
        in_specs=[
            pl.BlockSpec((1, n_sources, tg), lambda i: (i, 0, 0), memory_space=pltpu.SMEM),
            pl.BlockSpec(memory_space=pl.ANY),
        ],
        out_specs=pl.BlockSpec((n_sources, tg, s_dim, lanes), lambda i: (0, i, 0, 0)),
        scratch_shapes=[pltpu.SemaphoreType.DMA],
        compiler_params=_cparams(("arbitrary",), 32),
        name="row_gather",
    )(idx_blocks, src3)


def _moe_up_kernel(te_ref, nu_ref, x_ref, w1_ref, w3_ref, o_ref):
    used = pl.program_id(1) < nu_ref[0]

    @pl.when(used)
    def _():
        x = x_ref[...]
        a = jnp.dot(x, w1_ref[...].astype(BF16), preferred_element_type=F32)
        b = jnp.dot(x, w3_ref[...].astype(BF16), preferred_element_type=F32)
        o_ref[...] = (_silu(a) * b).astype(o_ref.dtype)

    @pl.when(jnp.logical_not(used))
    def _():
        o_ref[...] = jnp.zeros_like(o_ref)


def _moe_up(xs, w1, w3, layer, tile_expert, n_used):
    r = xs.shape[0]
    tm, tn = TM_MOE, TN_MOE
    w_spec = pl.BlockSpec((None, None, D_MODEL, tn), lambda j, t, te, nu: (layer, te[t], 0, j))
    grid_spec = pltpu.PrefetchScalarGridSpec(
        num_scalar_prefetch=2,
        grid=(FFN_DIM // tn, r // tm),
        in_specs=[pl.BlockSpec((tm, D_MODEL), lambda j, t, te, nu: (jnp.minimum(t, nu[0] - 1), 0)),
                  w_spec, w_spec],
        out_specs=pl.BlockSpec((tm, tn), lambda j, t, te, nu: (t, j)),
    )
    return pl.pallas_call(
        _moe_up_kernel,
        out_shape=jax.ShapeDtypeStruct((r, FFN_DIM), BF16),
        grid_spec=grid_spec,
        compiler_params=_cparams(("arbitrary", "arbitrary"), 48),
        name="moe_up",
    )(tile_expert, n_used, xs, w1, w3)


def _moe_down_kernel(te_ref, nu_ref, x_ref, w_ref, o_ref):
    used = pl.program_id(0) < nu_ref[0]

    @pl.when(used)
    def _():
        acc = jnp.dot(x_ref[...], w_ref[...].astype(BF16), preferred_element_type=F32)
        o_ref[...] = acc.astype(o_ref.dtype)

    @pl.when(jnp.logical_not(used))
    def _():
        o_ref[...] = jnp.zeros_like(o_ref)


def _moe_down(hmid, w2, layer, tile_expert, n_used):
    r = hmid.shape[0]
    tm, tn = TM_MOE, TN_MOE
    grid_spec = pltpu.PrefetchScalarGridSpec(
        num_scalar_prefetch=2,
        grid=(r // tm, D_MODEL // tn),
        in_specs=[
            pl.BlockSpec((tm, FFN_DIM), lambda t, j, te, nu: (jnp.minimum(t, nu[0] - 1), 0)),
            pl.BlockSpec((None, None, FFN_DIM, tn), lambda t, j, te, nu: (layer, te[t], 0, j)),
        ],
        out_specs=pl.BlockSpec((tm, tn), lambda t, j, te, nu: (t, j)),
    )
    return pl.pallas_call(
        _moe_down_kernel,
        out_shape=jax.ShapeDtypeStruct((r, D_MODEL), BF16),
        grid_spec=grid_spec,
        compiler_params=_cparams(("arbitrary", "arbitrary"), 56),
        name="moe_down",
    )(tile_expert, n_used, hmid, w2)


def _route_plan(route, tm):
    t = route.shape[0]
    idx = route[:, LANE_IDX:LANE_IDX + TOP_K].astype(jnp.int32)
    e_flat = idx.T.reshape(-1)
    onehot = (e_flat[:, None] == jnp.arange(N_EXPERTS)[None, :]).astype(jnp.int32)
    counts = onehot.sum(axis=0)
    rank = (onehot * (jnp.cumsum(onehot, axis=0) - onehot)).sum(axis=1)
    tiles_e = (counts + tm - 1) // tm
    tile_end = jnp.cumsum(tiles_e)
    tile_start = tile_end - tiles_e
    pos = (onehot * tile_start[None, :]).sum(axis=1) * tm + rank
    n_tiles = TOP_K * t // tm + N_EXPERTS
    n_used = tile_end[-1]
    tile_ids = jnp.minimum(jnp.arange(n_tiles), n_used - 1)
    tile_expert = jnp.minimum((tile_ids[:, None] >= tile_end[None, :]).sum(axis=1), N_EXPERTS - 1)
    row_token = jnp.zeros((n_tiles * tm,), jnp.int32).at[pos].set(
        jnp.tile(jnp.arange(t, dtype=jnp.int32), TOP_K))
    return (pos.reshape(TOP_K, t).astype(jnp.int32), row_token,
            tile_expert.astype(jnp.int32), n_used.reshape(1).astype(jnp.int32))


def _final_kernel(h_ref, y0_ref, y1_ref, route_ref, gt_ref, g_ref, o_ref):
    tm = h_ref.shape[0]
    w1 = route_ref[:, LANE_WT:LANE_WT + 1]
    w2 = route_ref[:, LANE_WT + 1:LANE_WT + 2]
    y0 = y0_ref[0].reshape(tm, D_MODEL).astype(F32)
    y1 = y1_ref[0].reshape(tm, D_MODEL).astype(F32)
    h = h_ref[...] + gt_ref[0] * (w1 * y0 + w2 * y1)
    o_ref[...] = h * lax.rsqrt(jnp.mean(h * h, axis=-1, keepdims=True) + NORM_EPS) * g_ref[...]


def _final(h, y01, route, gt, mod_row, g_final):
    t = h.shape[0]
    tm = 512
    s_dim = D_MODEL // LANES
    return pl.pallas_call(
        _final_kernel,
        out_shape=jax.ShapeDtypeStruct((t, D_MODEL), F32),
        grid=(t // tm,),
        in_specs=[
            pl.BlockSpec((tm, D_MODEL), lambda i: (i, 0)),
            pl.BlockSpec((1, tm, s_dim, LANES), lambda i: (0, i, 0, 0)),
            pl.BlockSpec((1, tm, s_dim, LANES), lambda i: (1, i, 0, 0)),
            pl.BlockSpec((tm, ROUTE_LANES), lambda i: (i, 0)),
            pl.BlockSpec((1, 1, D_MODEL), lambda i: (mod_row(i, tm), 0, 0)),
            pl.BlockSpec((1, D_MODEL), lambda i: (0, 0)),
        ],
        out_specs=pl.BlockSpec((tm, D_MODEL), lambda i: (i, 0)),
        compiler_params=_cparams(("arbitrary",), 48),
        name="final_residual_norm",
    )(h, y01, y01, route, gt, g_final)


def _rope_tables():
    pos = jnp.arange(SEQ)
    f = HEAD_DIM // 4
    inv = ROPE_BASE ** (-jnp.arange(f, dtype=F32) / f)
    ang_row = (pos // GRID_W).astype(F32)[:, None] * inv[None, :]
    ang_col = (pos % GRID_W).astype(F32)[:, None] * inv[None, :]
    ang = jnp.concatenate([ang_row, ang_row, ang_col, ang_col], axis=1)
    cos, sin = jnp.cos(ang), jnp.sin(ang)
    first = jnp.asarray(((np.arange(HEAD_DIM) // f) % 2 == 0)[None, :])
    return cos, jnp.where(first, -sin, 0.0), jnp.where(first, 0.0, sin)


def kernel(x, c, ctx, c_ctx, w_ada, b_ada, g_mix, g_ffn, w_in, a_norm_g, a_ws, a_bs, b_sink, c_conv,
           d_rpb, w_branch, w_out, ffn_w1, ffn_w3, ffn_w2, w_router, moe_w1, moe_w3, moe_w2, g_final):
    batch, seq, _ = x.shape
    depth = w_in.shape[0]
    assert seq == SEQ and ctx.shape[1] == CTX_LEN and batch + 1 <= MOD_ROWS
    assert depth % 2 == 0, "the fused residual + final-norm epilogue lives in the MoE (odd, last) layer"
    t = batch * seq
    tc = batch * CTX_LEN

    cond = jnp.zeros((MOD_ROWS, D_MODEL), F32).at[:batch].set(c).at[batch].set(c_ctx)
    mods = _ada(cond, w_ada, b_ada)
    rope_tabs = _rope_tables()
    conv_w8 = jnp.zeros((depth, 8, C_W), F32).at[:, :3].set(c_conv)

    w_in, w_branch, w_out = w_in.astype(BF16), w_branch.astype(BF16), w_out.astype(BF16)
    ffn_w1, ffn_w3, ffn_w2 = ffn_w1.astype(BF16), ffn_w3.astype(BF16), ffn_w2.astype(BF16)
    moe_w2 = moe_w2.astype(BF16)

    def lat_row(i, tm):
        return (i * tm) // SEQ

    def ctx_row(i, tm):
        return batch

    h = x.reshape(t, D_MODEL)
    hc = ctx.reshape(tc, D_MODEL)
    out = None
    for layer in range(depth):
        last = layer == depth - 1
        sh1, sc1, gt1, sh2, sc2, gt2 = [m.reshape(MOD_ROWS, 1, D_MODEL)
                                        for m in jnp.split(mods[layer], 6, axis=-1)]
        g_mix_l = g_mix[layer].reshape(1, D_MODEL)
        g_ffn_l = g_ffn[layer].reshape(1, D_MODEL)
        bias_tab = _na_bias_table(d_rpb[layer])

        if last:
            pc = _norm_mm(hc, g_mix_l, sc1, sh1, ctx_row, (w_in,), layer,
                          lambda j: jnp.where(j == 0, OFF_BK // TN_KV, OFF_DK // TN_KV - 1 + j),
                          3 * TN_KV, TM_PROJ, TN_KV, "ctx_kv_proj")
            kcb, vcb, kcd, vcd = 0, 1, 1, 2
        else:
            pc = _norm_mm(hc, g_mix_l, sc1, sh1, ctx_row, (w_in,), layer, lambda j: j,
                          IN_W, TM_PROJ, TN_IN, "ctx_in_proj")
            kcb, vcb, kcd, vcd = OFF_BK // B_KV_W, OFF_BV // B_KV_W, OFF_DK // D_W, OFF_DV // D_W

        px = _norm_mm(h, g_mix_l, sc1, sh1, lat_row, (w_in,), layer, lambda j: j,
                      IN_W, TM_PROJ, TN_IN, "in_proj")
        ys = (
            _gmlp(px, a_norm_g, a_ws, a_bs, layer),
            _win_attn(px, pc, kcb, vcb, b_sink[layer], rope_tabs, batch),
            _short_conv(px, conv_w8, layer, SEQ),
            _na_attn(px, pc, kcd, vcd, bias_tab, batch),
        )
        z = _merge(px, ys, w_branch, layer)
        h = _res_mm(z, w_out, layer, h, gt1, lat_row, TM_OUT, TN_OUT, "out_proj")

        if not last:
            ysc = (
                _gmlp(pc, a_norm_g, a_ws, a_bs, layer),
                _ctx_gqa(pc, b_sink[layer], batch),
                _short_conv(pc, conv_w8, layer, CTX_LEN),
                _ctx_mha(pc, batch),
            )
            zc = _merge(pc, ysc, w_branch, layer)
            hc = _res_mm(zc, w_out, layer, hc, gt1, ctx_row, TM_OUT, TN_OUT, "ctx_out_proj")

        j = layer // 2
        if layer % 2 == 0:
            hm = _norm_mm(h, g_ffn_l, sc2, sh2, lat_row, (ffn_w1, ffn_w3), j, lambda n: n,
                          FFN_DIM, TM_PROJ, TN_FFN, "ffn_up")
            h = _res_mm(hm, ffn_w2, j, h, gt2, lat_row, TM_DOWN, TN_DOWN, "ffn_down")
            if not last:
                hmc = _norm_mm(hc, g_ffn_l, sc2, sh2, ctx_row, (ffn_w1, ffn_w3), j, lambda n: n,
                               FFN_DIM, TM_PROJ, TN_FFN, "ctx_ffn_up")
                hc = _res_mm(hmc, ffn_w2, j, hc, gt2, ctx_row, TM_DOWN, TN_DOWN, "ctx_ffn_down")
        else:
            assert last, "the context MoE path would only be needed for an odd layer that is not the last"
            wr_pad = jnp.zeros((D_MODEL, ROUTE_LANES), F32).at[:, :N_EXPERTS].set(w_router[j])
            xn, route = _router(h, g_ffn_l, sc2, sh2, lat_row, wr_pad)
            pos, row_token, tile_expert, n_used = _route_plan(route, TM_MOE)
            xs = _gather_rows(xn.reshape(t, D_MODEL // LANES, LANES), row_token[None, :], 1)
            xs = xs.reshape(-1, D_MODEL)
            hm = _moe_up(xs, moe_w1, moe_w3, j, tile_expert, n_used)
            ye = _moe_down(hm, moe_w2, j, tile_expert, n_used)
            y01 = _gather_rows(ye.reshape(-1, D_MODEL // LANES, LANES), pos, TOP_K)
            out = _final(h, y01, route, gt2, lat_row, g_final.reshape(1, D_MODEL))

    return out.reshape(batch, seq, D_MODEL)
```

```python
import functools

import jax
import jax.numpy as jnp
import numpy as np
from jax import lax
from jax.experimental import pallas as pl
from jax.experimental.pallas import tpu as pltpu

F32 = jnp.float32
BF16 = jnp.bfloat16

D_MODEL = 2048
SEQ = 2048
CTX_LEN = 256
GRID_W = 64
HEAD_DIM = 128
ROPE_BASE = 10000.0
NORM_EPS = 1e-6
NEG_INF = -1e30

CHUNK = 128
A_GROUPS = 4
A_CH = 512
B_HEADS = 8
B_KV_HEADS = 2
B_GROUP = B_HEADS // B_KV_HEADS
B_WINDOW = 128
B_Q_W = B_HEADS * HEAD_DIM
B_KV_W = B_KV_HEADS * HEAD_DIM
C_W = 512
D_HEADS = 4
D_W = D_HEADS * HEAD_DIM
NA_KH = 8
NA_KW = 16

OFF_AU = 0
OFF_AV = OFF_AU + A_CH
OFF_BQ = OFF_AV + A_CH
OFF_BK = OFF_BQ + B_Q_W
OFF_BV = OFF_BK + B_KV_W
OFF_CB = OFF_BV + B_KV_W
OFF_CC = OFF_CB + C_W
OFF_CH = OFF_CC + C_W
OFF_DQ = OFF_CH + C_W
OFF_DK = OFF_DQ + D_W
OFF_DV = OFF_DK + D_W
OFF_G = OFF_DV + D_W
IN_W = OFF_G + 4 * D_MODEL
MIX_W = A_CH + B_Q_W + C_W + D_W

FFN_DIM = 7168
N_EXPERTS = 8
TOP_K = 2

LANES = 128
BF16_SUBLANES = 16
MIB = 2**20

TM_PROJ = 1024
TN_KV = 512
TN_IN = 1536
TN_FFN = 512
TM_OUT, TN_OUT = 1024, 1024
TM_DOWN, TN_DOWN = 512, 512
TM_MERGE, TN_MERGE = 1024, 512
TM_MOE = 512
TN_MOE = 512
TG_ROWS = 512
GATHER_UNROLL = 8
MOD_ROWS = 16


def _cparams(sem, vmem_mib):
    return pltpu.CompilerParams(dimension_semantics=sem, vmem_limit_bytes=vmem_mib * MIB)


def _silu(a):
    return a * jax.nn.sigmoid(a)


def _ada_kernel(c_ref, w_ref, b_ref, o_ref):
    c = c_ref[...]
    s = _silu(c).astype(BF16)
    o_ref[...] = jnp.dot(s, w_ref[...].astype(BF16), preferred_element_type=F32) + b_ref[...]


def _ada(cond, w_ada, b_ada):
    depth, _, n = w_ada.shape
    tn = 1024
    return pl.pallas_call(
        _ada_kernel,
        out_shape=jax.ShapeDtypeStruct((depth, MOD_ROWS, n), F32),
        grid=(depth, n // tn),
        in_specs=[
            pl.BlockSpec((MOD_ROWS, D_MODEL), lambda l, j: (0, 0)),
            pl.BlockSpec((None, D_MODEL, tn), lambda l, j: (l, 0, j)),
            pl.BlockSpec((None, 1, tn), lambda l, j: (l, 0, j)),
        ],
        out_specs=pl.BlockSpec((None, MOD_ROWS, tn), lambda l, j: (l, 0, j)),
        compiler_params=_cparams(("arbitrary", "arbitrary"), 40),
        name="ada_modulation",
    )(cond, w_ada, b_ada.reshape(depth, 1, n))


def _norm_mod(x, g, sc, sh):
    y = x * lax.rsqrt(jnp.mean(x * x, axis=-1, keepdims=True) + NORM_EPS) * g
    return y * (1.0 + sc) + sh


def _norm_mm_kernel(x_ref, g_ref, sc_ref, sh_ref, *rest, swiglu):
    n_w = 2 if swiglu else 1
    w_refs, o_ref, xn_ref = rest[:n_w], rest[n_w], rest[n_w + 1]

    @pl.when(pl.program_id(1) == 0)
    def _():
        xn_ref[...] = _norm_mod(x_ref[...], g_ref[...], sc_ref[0], sh_ref[0]).astype(BF16)

    xn = xn_ref[...]
    a = jnp.dot(xn, w_refs[0][...].astype(BF16), preferred_element_type=F32)
    if swiglu:
        b = jnp.dot(xn, w_refs[1][...].astype(BF16), preferred_element_type=F32)
        a = _silu(a) * b
    o_ref[...] = a.astype(o_ref.dtype)


def _norm_mm(x, g, sc, sh, mod_row, weights, layer, col_block, n_out, tm, tn, name):
    m = x.shape[0]
    swiglu = len(weights) == 2
    w_spec = pl.BlockSpec((None, D_MODEL, tn), lambda i, j: (layer, 0, col_block(j)))
    return pl.pallas_call(
        functools.partial(_norm_mm_kernel, swiglu=swiglu),
        out_shape=jax.ShapeDtypeStruct((m, n_out), BF16),
        grid=(m // tm, n_out // tn),
        in_specs=[
            pl.BlockSpec((tm, D_MODEL), lambda i, j: (i, 0)),
            pl.BlockSpec((1, D_MODEL), lambda i, j: (0, 0)),
            pl.BlockSpec((1, 1, D_MODEL), lambda i, j: (mod_row(i, tm), 0, 0)),
            pl.BlockSpec((1, 1, D_MODEL), lambda i, j: (mod_row(i, tm), 0, 0)),
        ] + [w_spec] * len(weights),
        out_specs=pl.BlockSpec((tm, tn), lambda i, j: (i, j)),
        scratch_shapes=[pltpu.VMEM((tm, D_MODEL), BF16)],
        compiler_params=_cparams(("arbitrary", "arbitrary"), 56),
        name=name,
    )(x, g, sc, sh, *weights)


def _res_mm_kernel(x_ref, w_ref, res_ref, gt_ref, o_ref):
    acc = jnp.dot(x_ref[...], w_ref[...].astype(BF16), preferred_element_type=F32)
    o_ref[...] = res_ref[...] + gt_ref[0] * acc


def _res_mm(x, w, layer, res, gt, mod_row, tm, tn, name):
    m, k = x.shape
    n = res.shape[1]
    return pl.pallas_call(
        _res_mm_kernel,
        out_shape=jax.ShapeDtypeStruct((m, n), F32),
        grid=(m // tm, n // tn),
        in_specs=[
            pl.BlockSpec((tm, k), lambda i, j: (i, 0)),
            pl.BlockSpec((None, k, tn), lambda i, j: (layer, 0, j)),
            pl.BlockSpec((tm, tn), lambda i, j: (i, j)),
            pl.BlockSpec((1, 1, tn), lambda i, j: (mod_row(i, tm), 0, j)),
        ],
        out_specs=pl.BlockSpec((tm, tn), lambda i, j: (i, j)),
        compiler_params=_cparams(("arbitrary", "arbitrary"), 56),
        name=name,
    )(x, w, res, gt)


def _gmlp_kernel(u_ref, v_ref, g_ref, ws_ref, bs_ref, o_ref):
    v = v_ref[...].astype(F32)
    vn = (v * lax.rsqrt(jnp.mean(v * v, axis=-1, keepdims=True) + NORM_EPS) * g_ref[...]).astype(BF16)
    rows = v.shape[0]
    for gi in range(A_GROUPS):
        w = ws_ref[gi].astype(BF16)
        bias = bs_ref[gi]
        cs = slice(gi * LANES, (gi + 1) * LANES)
        for c in range(rows // CHUNK):
            rs = slice(c * CHUNK, (c + 1) * CHUNK)
            s = jnp.dot(w, vn[rs, cs], preferred_element_type=F32) + bias
            o_ref[rs, cs] = (u_ref[rs, cs].astype(F32) * s).astype(o_ref.dtype)


def _gmlp(p, norm_g, ws, bs, layer):
    m = p.shape[0]
    tm = 512
    return pl.pallas_call(
        _gmlp_kernel,
        out_shape=jax.ShapeDtypeStruct((m, A_CH), BF16),
        grid=(m // tm,),
        in_specs=[
            pl.BlockSpec((tm, A_CH), lambda i: (i, OFF_AU // A_CH)),
            pl.BlockSpec((tm, A_CH), lambda i: (i, OFF_AV // A_CH)),
            pl.BlockSpec((1, A_CH), lambda i: (0, 0)),
            pl.BlockSpec((None, A_GROUPS, CHUNK, CHUNK), lambda i: (layer, 0, 0, 0)),
            pl.BlockSpec((None, A_GROUPS, CHUNK, 1), lambda i: (layer, 0, 0, 0)),
        ],
        out_specs=pl.BlockSpec((tm, A_CH), lambda i: (i, 0)),
        compiler_params=_cparams(("arbitrary",), 32),
        name="mixer_a_gmlp",
    )(p, p, norm_g[layer].reshape(1, A_CH), ws, bs.reshape(bs.shape + (1,)))


CONV_ROWS = 256
HALO = BF16_SUBLANES


def _conv_kernel(b_ref, c_ref, h_ref, cp_ref, hp_ref, cn_ref, hn_ref, w_ref, o_ref, *, tiles_per_seq):
    i = pl.program_id(0)
    z = c_ref[...].astype(F32) * h_ref[...].astype(F32)
    n = z.shape[0]
    first = (i % tiles_per_seq) == 0
    last = (i % tiles_per_seq) == tiles_per_seq - 1
    zp = jnp.where(first, 0.0, cp_ref[HALO - 1:HALO, :].astype(F32) * hp_ref[HALO - 1:HALO, :].astype(F32))
    zn = jnp.where(last, 0.0, cn_ref[0:1, :].astype(F32) * hn_ref[0:1, :].astype(F32))
    pos = lax.broadcasted_iota(jnp.int32, z.shape, 0)
    z_prev = jnp.where(pos == 0, zp, pltpu.roll(z, 1, axis=0))
    z_next = jnp.where(pos == n - 1, zn, pltpu.roll(z, n - 1, axis=0))
    w = w_ref[...]
    conv = z_prev * w[0:1] + z * w[1:2] + z_next * w[2:3]
    o_ref[...] = (b_ref[...].astype(F32) * conv).astype(o_ref.dtype)


def _short_conv(p, conv_w8, layer, seq_len):
    m = p.shape[0]
    tr = CONV_ROWS
    per = tr // HALO
    n_halo = m // HALO

    def prev_blk(col):
        return lambda i: (jnp.maximum(i * per - 1, 0), col)

    def next_blk(col):
        return lambda i: (jnp.minimum((i + 1) * per, n_halo - 1), col)

    cc, ch = OFF_CC // C_W, OFF_CH // C_W
    return pl.pallas_call(
        functools.partial(_conv_kernel, tiles_per_seq=seq_len // tr),
        out_shape=jax.ShapeDtypeStruct((m, C_W), BF16),
        grid=(m // tr,),
        in_specs=[
            pl.BlockSpec((tr, C_W), lambda i: (i, OFF_CB // C_W)),
            pl.BlockSpec((tr, C_W), lambda i: (i, cc)),
            pl.BlockSpec((tr, C_W), lambda i: (i, ch)),
            pl.BlockSpec((HALO, C_W), prev_blk(cc)),
            pl.BlockSpec((HALO, C_W), prev_blk(ch)),
            pl.BlockSpec((HALO, C_W), next_blk(cc)),
            pl.BlockSpec((HALO, C_W), next_blk(ch)),
            pl.BlockSpec((None, 8, C_W), lambda i: (layer, 0, 0)),
        ],
        out_specs=pl.BlockSpec((tr, C_W), lambda i: (i, 0)),
        compiler_params=_cparams(("arbitrary",), 32),
        name="mixer_c_conv",
    )(p, p, p, p, p, p, p, conv_w8)


def _rope(x, cos, sin_lo, sin_hi):
    x = x.astype(F32)
    return x * cos + pltpu.roll(x, LANES - 32, axis=1) * sin_lo + pltpu.roll(x, 32, axis=1) * sin_hi


def _softmax_av(s_parts, v_parts, extra_logit=None):
    m = s_parts[0].max(axis=-1, keepdims=True)
    for s in s_parts[1:]:
        m = jnp.maximum(m, s.max(axis=-1, keepdims=True))
    if extra_logit is not None:
        m = jnp.maximum(m, extra_logit)
    denom = jnp.exp(extra_logit - m) if extra_logit is not None else 0.0
    acc = None
    for s, v in zip(s_parts, v_parts):
        e = jnp.exp(s - m)
        denom = denom + e.sum(axis=-1, keepdims=True)
        pv = jnp.dot(e.astype(BF16), v, preferred_element_type=F32)
        acc = pv if acc is None else acc + pv
    return acc / denom


def _qk(q, k):
    return lax.dot_general(q, k, (((1,), (1,)), ((), ())), preferred_element_type=F32)


def _sink_column(sink_ref, h, rows_per_head):
    rid = lax.broadcasted_iota(jnp.int32, (B_GROUP * rows_per_head, 1), 0) // rows_per_head
    col = jnp.zeros((B_GROUP * rows_per_head, 1), F32)
    for g in range(B_GROUP):
        col = jnp.where(rid == g, sink_ref[h * B_GROUP + g], col)
    return col


def _win_attn_kernel(sink_ref, q_ref, k_ref, v_ref, kc_ref, vc_ref, cos_ref, slo_ref, shi_ref, mask_ref, o_ref):
    n = pl.program_id(1)
    blk = B_WINDOW
    band = 3 * blk
    kblk = jnp.clip(n - 1, 0, SEQ // blk - 3)
    q0 = pl.multiple_of(n * blk, blk)
    k0 = pl.multiple_of(kblk * blk, blk)
    scale = HEAD_DIM ** -0.5

    cos_q, slo_q, shi_q = cos_ref[pl.ds(q0, blk), :], slo_ref[pl.ds(q0, blk), :], shi_ref[pl.ds(q0, blk), :]
    cos_k, slo_k, shi_k = cos_ref[pl.ds(k0, band), :], slo_ref[pl.ds(k0, band), :], shi_ref[pl.ds(k0, band), :]
    window_bias = mask_ref[n - kblk]

    for h in range(B_KV_HEADS):
        hs = slice(h * HEAD_DIM, (h + 1) * HEAD_DIM)
        k_loc = _rope(k_ref[pl.ds(k0, band), hs], cos_k, slo_k, shi_k).astype(BF16)
        v_loc = v_ref[pl.ds(k0, band), hs]
        k_ctx = kc_ref[:, hs]
        v_ctx = vc_ref[:, hs]
        qs = []
        for g in range(B_GROUP):
            c0 = (h * B_GROUP + g) * HEAD_DIM
            qs.append(_rope(q_ref[:, c0:c0 + HEAD_DIM], cos_q, slo_q, shi_q).astype(BF16))
        qst = jnp.concatenate(qs, axis=0)
        s_loc = _qk(qst, k_loc) * scale + window_bias
        s_ctx = _qk(qst, k_ctx) * scale
        out = _softmax_av([s_ctx, s_loc], [v_ctx, v_loc], _sink_column(sink_ref, h, blk))
        for g in range(B_GROUP):
            c0 = (h * B_GROUP + g) * HEAD_DIM
            o_ref[:, c0:c0 + HEAD_DIM] = out[g * blk:(g + 1) * blk].astype(o_ref.dtype)


def _window_bias_table():
    i = np.arange(B_WINDOW)[:, None]
    j = np.arange(3 * B_WINDOW)[None, :]
    tabs = []
    for c in range(3):
        rel = j - c * B_WINDOW - i
        one = np.where(np.abs(rel) <= B_WINDOW, 0.0, NEG_INF).astype(np.float32)
        tabs.append(np.tile(one, (B_GROUP, 1)))
    return jnp.asarray(np.stack(tabs))


def _win_attn(p, pc, kc_blk, vc_blk, sink, rope_tabs, batch):
    nb = SEQ // B_WINDOW
    kvw = B_KV_W
    mask_tab = _window_bias_table()
    return pl.pallas_call(
        _win_attn_kernel,
        out_shape=jax.ShapeDtypeStruct((batch * SEQ, B_Q_W), BF16),
        grid=(batch, nb),
        in_specs=[
            pl.BlockSpec(memory_space=pltpu.SMEM),
            pl.BlockSpec((B_WINDOW, B_Q_W), lambda b, n: (b * nb + n, OFF_BQ // B_Q_W)),
            pl.BlockSpec((SEQ, kvw), lambda b, n: (b, OFF_BK // kvw)),
            pl.BlockSpec((SEQ, kvw), lambda b, n: (b, OFF_BV // kvw)),
            pl.BlockSpec((CTX_LEN, kvw), lambda b, n: (b, kc_blk)),
            pl.BlockSpec((CTX_LEN, kvw), lambda b, n: (b, vc_blk)),
            pl.BlockSpec((SEQ, HEAD_DIM), lambda b, n: (0, 0)),
            pl.BlockSpec((SEQ, HEAD_DIM), lambda b, n: (0, 0)),
            pl.BlockSpec((SEQ, HEAD_DIM), lambda b, n: (0, 0)),
            pl.BlockSpec(mask_tab.shape, lambda b, n: (0, 0, 0)),
        ],
        out_specs=pl.BlockSpec((B_WINDOW, B_Q_W), lambda b, n: (b * nb + n, 0)),
        compiler_params=_cparams(("arbitrary", "arbitrary"), 40),
        name="mixer_b_window_attn",
    )(sink, p, p, p, pc, pc, *rope_tabs, mask_tab)


def _ctx_gqa_kernel(sink_ref, q_ref, kc_ref, vc_ref, o_ref):
    scale = HEAD_DIM ** -0.5
    lc = q_ref.shape[0]
    for h in range(B_KV_HEADS):
        hs = slice(h * HEAD_DIM, (h + 1) * HEAD_DIM)
        qst = jnp.concatenate(
            [q_ref[:, (h * B_GROUP + g) * HEAD_DIM:(h * B_GROUP + g + 1) * HEAD_DIM] for g in range(B_GROUP)],
            axis=0)
        out = _softmax_av([_qk(qst, kc_ref[:, hs]) * scale], [vc_ref[:, hs]], _sink_column(sink_ref, h, lc))
        for g in range(B_GROUP):
            c0 = (h * B_GROUP + g) * HEAD_DIM
            o_ref[:, c0:c0 + HEAD_DIM] = out[g * lc:(g + 1) * lc].astype(o_ref.dtype)


def _ctx_gqa(pc, sink, batch):
    kvw = B_KV_W
    return pl.pallas_call(
        _ctx_gqa_kernel,
        out_shape=jax.ShapeDtypeStruct((batch * CTX_LEN, B_Q_W), BF16),
        grid=(batch,),
        in_specs=[
            pl.BlockSpec(memory_space=pltpu.SMEM),
            pl.BlockSpec((CTX_LEN, B_Q_W), lambda b: (b, OFF_BQ // B_Q_W)),
            pl.BlockSpec((CTX_LEN, kvw), lambda b: (b, OFF_BK // kvw)),
            pl.BlockSpec((CTX_LEN, kvw), lambda b: (b, OFF_BV // kvw)),
        ],
        out_specs=pl.BlockSpec((CTX_LEN, B_Q_W), lambda b: (b, 0)),
        compiler_params=_cparams(("arbitrary",), 32),
        name="mixer_b_ctx_attn",
    )(sink, pc, pc, pc)


NA_ROWS_PER_STEP = 4
GRID_H = SEQ // GRID_W
NA_LOC = NA_KH * GRID_W


def _na_kernel(q_ref, k_ref, v_ref, kc_ref, vc_ref, bias_ref, o_ref):
    step = pl.program_id(1)
    scale = HEAD_DIM ** -0.5
    blocks, s_loc, s_ctx = [], [], []
    for rr in range(NA_ROWS_PER_STEP):
        r = step * NA_ROWS_PER_STEP + rr
        rs = jnp.clip(r - NA_KH // 2, 0, GRID_H - NA_KH)
        k0 = pl.multiple_of(rs * GRID_W, GRID_W)
        d = r - rs
        qrows = slice(rr * GRID_W, (rr + 1) * GRID_W)
        for h in range(D_HEADS):
            hs = slice(h * HEAD_DIM, (h + 1) * HEAD_DIM)
            q = q_ref[qrows, hs]
            s_loc.append(_qk(q, k_ref[pl.ds(k0, NA_LOC), hs]) * scale + bias_ref[h, d])
            s_ctx.append(_qk(q, kc_ref[:, hs]) * scale)
            blocks.append((qrows, hs, k0))
    s_loc = jnp.concatenate(s_loc, axis=0)
    s_ctx = jnp.concatenate(s_ctx, axis=0)
    m = jnp.maximum(s_loc.max(axis=-1, keepdims=True), s_ctx.max(axis=-1, keepdims=True))
    e_loc = jnp.exp(s_loc - m)
    e_ctx = jnp.exp(s_ctx - m)
    denom = e_loc.sum(axis=-1, keepdims=True) + e_ctx.sum(axis=-1, keepdims=True)
    e_loc = e_loc.astype(BF16)
    e_ctx = e_ctx.astype(BF16)
    for i, (qrows, hs, k0) in enumerate(blocks):
        rows = slice(i * GRID_W, (i + 1) * GRID_W)
        acc = jnp.dot(e_ctx[rows], vc_ref[:, hs], preferred_element_type=F32)
        acc = acc + jnp.dot(e_loc[rows], v_ref[pl.ds(k0, NA_LOC), hs], preferred_element_type=F32)
        o_ref[qrows, hs] = (acc / denom[rows]).astype(o_ref.dtype)


def _na_bias_table(rpb):
    rpb = rpb.astype(F32)
    rows = []
    for c in range(GRID_W):
        cstart = min(max(c - NA_KW // 2, 0), GRID_W - NA_KW)
        first_dc = cstart - c + NA_KW - 1
        win = rpb[:, :, first_dc:first_dc + NA_KW]
        rows.append(jnp.pad(win, ((0, 0), (0, 0), (cstart, GRID_W - NA_KW - cstart)), constant_values=NEG_INF))
    tab0 = jnp.stack(rows, axis=2)
    slabs = []
    for d in range(NA_KH):
        sl = tab0[:, NA_KH - 1 - d:2 * NA_KH - 1 - d]
        slabs.append(jnp.transpose(sl, (0, 2, 1, 3)).reshape(D_HEADS, GRID_W, NA_LOC))
    return jnp.stack(slabs, axis=1)


def _na_attn(p, pc, kc_blk, vc_blk, bias_tab, batch):
    rows_q = NA_ROWS_PER_STEP * GRID_W
    steps = SEQ // rows_q
    return pl.pallas_call(
        _na_kernel,
        out_shape=jax.ShapeDtypeStruct((batch * SEQ, D_W), BF16),
        grid=(batch, steps),
        in_specs=[
            pl.BlockSpec((rows_q, D_W), lambda b, s: (b * steps + s, OFF_DQ // D_W)),
            pl.BlockSpec((SEQ, D_W), lambda b, s: (b, OFF_DK // D_W)),
            pl.BlockSpec((SEQ, D_W), lambda b, s: (b, OFF_DV // D_W)),
            pl.BlockSpec((CTX_LEN, D_W), lambda b, s: (b, kc_blk)),
            pl.BlockSpec((CTX_LEN, D_W), lambda b, s: (b, vc_blk)),
            pl.BlockSpec((D_HEADS, NA_KH, GRID_W, NA_LOC), lambda b, s: (0, 0, 0, 0)),
        ],
        out_specs=pl.BlockSpec((rows_q, D_W), lambda b, s: (b * steps + s, 0)),
        compiler_params=_cparams(("arbitrary", "arbitrary"), 48),
        name="mixer_d_neighborhood_attn",
    )(p, p, p, pc, pc, bias_tab)


def _ctx_mha_kernel(q_ref, kc_ref, vc_ref, o_ref):
    scale = HEAD_DIM ** -0.5
    for h in range(D_HEADS):
        hs = slice(h * HEAD_DIM, (h + 1) * HEAD_DIM)
        s = _qk(q_ref[:, hs], kc_ref[:, hs]) * scale
        o_ref[:, hs] = _softmax_av([s], [vc_ref[:, hs]]).astype(o_ref.dtype)


def _ctx_mha(pc, batch):
    return pl.pallas_call(
        _ctx_mha_kernel,
        out_shape=jax.ShapeDtypeStruct((batch * CTX_LEN, D_W), BF16),
        grid=(batch,),
        in_specs=[
            pl.BlockSpec((CTX_LEN, D_W), lambda b: (b, OFF_DQ // D_W)),
            pl.BlockSpec((CTX_LEN, D_W), lambda b: (b, OFF_DK // D_W)),
            pl.BlockSpec((CTX_LEN, D_W), lambda b: (b, OFF_DV // D_W)),
        ],
        out_specs=pl.BlockSpec((CTX_LEN, D_W), lambda b: (b, 0)),
        compiler_params=_cparams(("arbitrary",), 32),
        name="mixer_d_ctx_attn",
    )(pc, pc, pc)


BR_W = (A_CH, B_Q_W, C_W, D_W)
BR_OFF = (0, A_CH, A_CH + B_Q_W, A_CH + B_Q_W + C_W)


def _merge_kernel(ya_ref, yb_ref, yc_ref, yd_ref, w_ref, g0_ref, g1_ref, g2_ref, g3_ref, o_ref):
    z = None
    for y_ref, g_ref, off, width in zip((ya_ref, yb_ref, yc_ref, yd_ref),
                                        (g0_ref, g1_ref, g2_ref, g3_ref), BR_OFF, BR_W):
        t = jnp.dot(y_ref[...], w_ref[off:off + width, :].astype(BF16), preferred_element_type=F32)
        t = jax.nn.sigmoid(g_ref[...].astype(F32)) * t
        z = t if z is None else z + t
    o_ref[...] = z.astype(o_ref.dtype)


def _merge(p, ys, w_branch, layer):
    m = p.shape[0]
    tm, tn = TM_MERGE, TN_MERGE
    gate_specs = [pl.BlockSpec((tm, tn), functools.partial(
        lambda i, j, b: (i, (OFF_G + b * D_MODEL) // tn + j), b=b)) for b in range(4)]
    return pl.pallas_call(
        _merge_kernel,
        out_shape=jax.ShapeDtypeStruct((m, D_MODEL), BF16),
        grid=(m // tm, D_MODEL // tn),
        in_specs=[pl.BlockSpec((tm, w), lambda i, j: (i, 0)) for w in BR_W]
        + [pl.BlockSpec((None, MIX_W, tn), lambda i, j: (layer, 0, j))] + gate_specs,
        out_specs=pl.BlockSpec((tm, tn), lambda i, j: (i, j)),
        compiler_params=_cparams(("arbitrary", "arbitrary"), 48),
        name="branch_merge",
    )(*ys, w_branch, p, p, p, p)


ROUTE_LANES = LANES
LANE_IDX = N_EXPERTS
LANE_WT = N_EXPERTS + TOP_K


def _router_kernel(x_ref, g_ref, sc_ref, sh_ref, wr_ref, xn_ref, route_ref):
    xn = _norm_mod(x_ref[...], g_ref[...], sc_ref[0], sh_ref[0])
    xn_ref[...] = xn.astype(BF16).reshape(xn_ref.shape)
    logits = jnp.dot(xn, wr_ref[...], preferred_element_type=F32, precision=lax.Precision.HIGHEST)
    lane = lax.broadcasted_iota(jnp.int32, logits.shape, 1).astype(F32)
    logits = jnp.where(lane < N_EXPERTS, logits, -jnp.inf)
    m1 = logits.max(axis=-1, keepdims=True)
    i1 = jnp.where(logits == m1, lane, float(ROUTE_LANES)).min(axis=-1, keepdims=True)
    rest = jnp.where(lane == i1, -jnp.inf, logits)
    m2 = rest.max(axis=-1, keepdims=True)
    i2 = jnp.where(rest == m2, lane, float(ROUTE_LANES)).min(axis=-1, keepdims=True)
    e2 = jnp.exp(m2 - m1)
    w1 = 1.0 / (1.0 + e2)
    w2 = e2 / (1.0 + e2)
    out = jnp.where(lane == LANE_IDX, i1, 0.0)
    out = jnp.where(lane == LANE_IDX + 1, i2, out)
    out = jnp.where(lane == LANE_WT, w1, out)
    out = jnp.where(lane == LANE_WT + 1, w2, out)
    route_ref[...] = out


def _router(h, g, sc, sh, mod_row, w_router_pad):
    t = h.shape[0]
    tm = 512
    return pl.pallas_call(
        _router_kernel,
        out_shape=(jax.ShapeDtypeStruct((t, D_MODEL // LANES, LANES), BF16),
                   jax.ShapeDtypeStruct((t, ROUTE_LANES), F32)),
        grid=(t // tm,),
        in_specs=[
            pl.BlockSpec((tm, D_MODEL), lambda i: (i, 0)),
            pl.BlockSpec((1, D_MODEL), lambda i: (0, 0)),
            pl.BlockSpec((1, 1, D_MODEL), lambda i: (mod_row(i, tm), 0, 0)),
            pl.BlockSpec((1, 1, D_MODEL), lambda i: (mod_row(i, tm), 0, 0)),
            pl.BlockSpec((D_MODEL, ROUTE_LANES), lambda i: (0, 0)),
        ],
        out_specs=(pl.BlockSpec((tm, D_MODEL // LANES, LANES), lambda i: (i, 0, 0)),
                   pl.BlockSpec((tm, ROUTE_LANES), lambda i: (i, 0))),
        compiler_params=_cparams(("arbitrary",), 40),
        name="moe_router",
    )(h, g, sc, sh, w_router_pad)


def _gather_rows_kernel(idx_ref, src_ref, o_ref, *scratch, n_sources, flat):
    sem = scratch[-1]
    dst_ref = scratch[0] if flat else o_ref
    rows = dst_ref.shape[-3]

    def row_copy(s, r, src_row):
        return pltpu.make_async_copy(src_ref.at[src_row], dst_ref.at[s, r], sem)

    def issue(i, carry):
        for u in range(GATHER_UNROLL):
            r = i * GATHER_UNROLL + u
            for s in range(n_sources):
                row_copy(s, r, idx_ref[0, s, r]).start(priority=(u * n_sources + s) % 2)
        return carry

    lax.fori_loop(0, rows // GATHER_UNROLL, issue, 0)

    def drain(i, carry):
        for u in range(GATHER_UNROLL):
            for s in range(n_sources):
                row_copy(s, i * GATHER_UNROLL + u, 0).wait()
        return carry

    lax.fori_loop(0, rows // GATHER_UNROLL, drain, 0)
    if flat:
        o_ref[...] = dst_ref[0].reshape(o_ref.shape)


def _gather_rows(src3, idx, n_sources, flat=False):
    _, s_dim, lanes = src3.shape
    r = idx.shape[1]
    tg = TG_ROWS
    idx_blocks = idx.reshape(n_sources, r // tg, tg).transpose(1, 0, 2)
    if flat:
        assert n_sources == 1
        out_shape = jax.ShapeDtypeStruct((r, s_dim * lanes), src3.dtype)
        out_spec = pl.BlockSpec((tg, s_dim * lanes), lambda i: (i, 0))
        scratch = [pltpu.VMEM((1, tg, s_dim, lanes), src3.dtype), pltpu.SemaphoreType.DMA]
    else:
        out_shape = jax.ShapeDtypeStruct((n_sources, r, s_dim, lanes), src3.dtype)
        out_spec = pl.BlockSpec((n_sources, tg, s_dim, lanes), lambda i: (0, i, 0, 0))
        scratch = [pltpu.SemaphoreType.DMA]
    return pl.pallas_call(
        functools.partial(_gather_rows_kernel, n_sources=n_sources, flat=flat),
        out_shape=out_shape,
        grid=(r // tg,),
        in_specs=[
            pl.BlockSpec((1, n_sources, tg), lambda i: (i, 0, 0), memory_space=pltpu.SMEM),
            pl.BlockSpec(memory_space=pl.ANY),
        ],
        out_specs=out_spec,
        scratch_shapes=scratch,
        compiler_params=_cparams(("arbitrary",), 32),
        name="row_gather",
    )(idx_blocks, src3)


def _moe_up_kernel(te_ref, nu_ref, x_ref, w1_ref, w3_ref, o_ref):
    used = pl.program_id(1) < nu_ref[0]

    @pl.when(used)
    def _():
        x = x_ref[...]
        a = jnp.dot(x, w1_ref[...].astype(BF16), preferred_element_type=F32)
        b = jnp.dot(x, w3_ref[...].astype(BF16), preferred_element_type=F32)
        o_ref[...] = (_silu(a) * b).astype(o_ref.dtype)

    @pl.when(jnp.logical_not(used))
    def _():
        o_ref[...] = jnp.zeros_like(o_ref)


def _moe_up(xs, w1, w3, layer, tile_expert, n_used):
    r = xs.shape[0]
    tm, tn = TM_MOE, TN_MOE
    w_spec = pl.BlockSpec((None, None, D_MODEL, tn), lambda j, t, te, nu: (layer, te[t], 0, j))
    grid_spec = pltpu.PrefetchScalarGridSpec(
        num_scalar_prefetch=2,
        grid=(FFN_DIM // tn, r // tm),
        in_specs=[pl.BlockSpec((tm, D_MODEL), lambda j, t, te, nu: (jnp.minimum(t, nu[0] - 1), 0)),
                  w_spec, w_spec],
        out_specs=pl.BlockSpec((tm, tn), lambda j, t, te, nu: (t, j)),
    )
    return pl.pallas_call(
        _moe_up_kernel,
        out_shape=jax.ShapeDtypeStruct((r, FFN_DIM), BF16),
        grid_spec=grid_spec,
        compiler_params=_cparams(("arbitrary", "arbitrary"), 48),
        name="moe_up",
    )(tile_expert, n_used, xs, w1, w3)


def _moe_down_kernel(te_ref, nu_ref, x_ref, w_ref, o_ref):
    used = pl.program_id(0) < nu_ref[0]

    @pl.when(used)
    def _():
        acc = jnp.dot(x_ref[...], w_ref[...].astype(BF16), preferred_element_type=F32)
        o_ref[...] = acc.astype(o_ref.dtype)

    @pl.when(jnp.logical_not(used))
    def _():
        o_ref[...] = jnp.zeros_like(o_ref)


def _moe_down(hmid, w2, layer, tile_expert, n_used):
    r = hmid.shape[0]
    tm, tn = TM_MOE, TN_MOE
    grid_spec = pltpu.PrefetchScalarGridSpec(
        num_scalar_prefetch=2,
        grid=(r // tm, D_MODEL // tn),
        in_specs=[
            pl.BlockSpec((tm, FFN_DIM), lambda t, j, te, nu: (jnp.minimum(t, nu[0] - 1), 0)),
            pl.BlockSpec((None, None, FFN_DIM, tn), lambda t, j, te, nu: (layer, te[t], 0, j)),
        ],
        out_specs=pl.BlockSpec((tm, tn), lambda t, j, te, nu: (t, j)),
    )
    return pl.pallas_call(
        _moe_down_kernel,
        out_shape=jax.ShapeDtypeStruct((r, D_MODEL), BF16),
        grid_spec=grid_spec,
        compiler_params=_cparams(("arbitrary", "arbitrary"), 56),
        name="moe_down",
    )(tile_expert, n_used, hmid, w2)


def _route_plan(route, tm):
    t = route.shape[0]
    idx = route[:, LANE_IDX:LANE_IDX + TOP_K].astype(jnp.int32)
    e_flat = idx.T.reshape(-1)
    onehot = (e_flat[:, None] == jnp.arange(N_EXPERTS)[None, :]).astype(jnp.int32)
    counts = onehot.sum(axis=0)
    rank = (onehot * (jnp.cumsum(onehot, axis=0) - onehot)).sum(axis=1)
    tiles_e = (counts + tm - 1) // tm
    tile_end = jnp.cumsum(tiles_e)
    tile_start = tile_end - tiles_e
    pos = (onehot * tile_start[None, :]).sum(axis=1) * tm + rank
    n_tiles = TOP_K * t // tm + N_EXPERTS
    n_used = tile_end[-1]
    tile_ids = jnp.minimum(jnp.arange(n_tiles), n_used - 1)
    tile_expert = jnp.minimum((tile_ids[:, None] >= tile_end[None, :]).sum(axis=1), N_EXPERTS - 1)
    row_token = jnp.zeros((n_tiles * tm,), jnp.int32).at[pos].set(
        jnp.tile(jnp.arange(t, dtype=jnp.int32), TOP_K))
    return (pos.reshape(TOP_K, t).astype(jnp.int32), row_token,
            tile_expert.astype(jnp.int32), n_used.reshape(1).astype(jnp.int32))


def _final_kernel(h_ref, y0_ref, y1_ref, route_ref, gt_ref, g_ref, o_ref):
    tm = h_ref.shape[0]
    w1 = route_ref[:, LANE_WT:LANE_WT + 1]
    w2 = route_ref[:, LANE_WT + 1:LANE_WT + 2]
    y0 = y0_ref[0].reshape(tm, D_MODEL).astype(F32)
    y1 = y1_ref[0].reshape(tm, D_MODEL).astype(F32)
    h = h_ref[...] + gt_ref[0] * (w1 * y0 + w2 * y1)
    o_ref[...] = h * lax.rsqrt(jnp.mean(h * h, axis=-1, keepdims=True) + NORM_EPS) * g_ref[...]


def _final(h, y01, route, gt, mod_row, g_final):
    t = h.shape[0]
    tm = 512
    s_dim = D_MODEL // LANES
    return pl.pallas_call(
        _final_kernel,
        out_shape=jax.ShapeDtypeStruct((t, D_MODEL), F32),
        grid=(t // tm,),
        in_specs=[
            pl.BlockSpec((tm, D_MODEL), lambda i: (i, 0)),
            pl.BlockSpec((1, tm, s_dim, LANES), lambda i: (0, i, 0, 0)),
            pl.BlockSpec((1, tm, s_dim, LANES), lambda i: (1, i, 0, 0)),
            pl.BlockSpec((tm, ROUTE_LANES), lambda i: (i, 0)),
            pl.BlockSpec((1, 1, D_MODEL), lambda i: (mod_row(i, tm), 0, 0)),
            pl.BlockSpec((1, D_MODEL), lambda i: (0, 0)),
        ],
        out_specs=pl.BlockSpec((tm, D_MODEL), lambda i: (i, 0)),
        compiler_params=_cparams(("arbitrary",), 48),
        name="final_residual_norm",
    )(h, y01, y01, route, gt, g_final)


def _rope_tables():
    pos = jnp.arange(SEQ)
    f = HEAD_DIM // 4
    inv = ROPE_BASE ** (-jnp.arange(f, dtype=F32) / f)
    ang_row = (pos // GRID_W).astype(F32)[:, None] * inv[None, :]
    ang_col = (pos % GRID_W).astype(F32)[:, None] * inv[None, :]
    ang = jnp.concatenate([ang_row, ang_row, ang_col, ang_col], axis=1)
    cos, sin = jnp.cos(ang), jnp.sin(ang)
    first = jnp.asarray(((np.arange(HEAD_DIM) // f) % 2 == 0)[None, :])
    return cos, jnp.where(first, -sin, 0.0), jnp.where(first, 0.0, sin)


def kernel(x, c, ctx, c_ctx, w_ada, b_ada, g_mix, g_ffn, w_in, a_norm_g, a_ws, a_bs, b_sink, c_conv,
           d_rpb, w_branch, w_out, ffn_w1, ffn_w3, ffn_w2, w_router, moe_w1, moe_w3, moe_w2, g_final):
    batch, seq, _ = x.shape
    depth = w_in.shape[0]
    assert seq == SEQ and ctx.shape[1] == CTX_LEN and batch + 1 <= MOD_ROWS
    assert depth % 2 == 0, "the fused residual + final-norm epilogue lives in the MoE (odd, last) layer"
    t = batch * seq
    tc = batch * CTX_LEN

    cond = jnp.zeros((MOD_ROWS, D_MODEL), F32).at[:batch].set(c).at[batch].set(c_ctx)
    mods = _ada(cond, w_ada, b_ada)
    rope_tabs = _rope_tables()
    conv_w8 = jnp.zeros((depth, 8, C_W), F32).at[:, :3].set(c_conv)

    w_in, w_branch, w_out = w_in.astype(BF16), w_branch.astype(BF16), w_out.astype(BF16)
    ffn_w1, ffn_w3, ffn_w2 = ffn_w1.astype(BF16), ffn_w3.astype(BF16), ffn_w2.astype(BF16)
    moe_w2 = moe_w2.astype(BF16)

    def lat_row(i, tm):
        return (i * tm) // SEQ

    def ctx_row(i, tm):
        return batch

    h = x.reshape(t, D_MODEL)
    hc = ctx.reshape(tc, D_MODEL)
    out = None
    for layer in range(depth):
        last = layer == depth - 1
        sh1, sc1, gt1, sh2, sc2, gt2 = [m.reshape(MOD_ROWS, 1, D_MODEL)
                                        for m in jnp.split(mods[layer], 6, axis=-1)]
        g_mix_l = g_mix[layer].reshape(1, D_MODEL)
        g_ffn_l = g_ffn[layer].reshape(1, D_MODEL)
        bias_tab = _na_bias_table(d_rpb[layer])

        if last:
            pc = _norm_mm(hc, g_mix_l, sc1, sh1, ctx_row, (w_in,), layer,
                          lambda j: jnp.where(j == 0, OFF_BK // TN_KV, OFF_DK // TN_KV - 1 + j),
                          3 * TN_KV, TM_PROJ, TN_KV, "ctx_kv_proj")
            kcb, vcb, kcd, vcd = 0, 1, 1, 2
        else:
            pc = _norm_mm(hc, g_mix_l, sc1, sh1, ctx_row, (w_in,), layer, lambda j: j,
                          IN_W, TM_PROJ, TN_IN, "ctx_in_proj")
            kcb, vcb, kcd, vcd = OFF_BK // B_KV_W, OFF_BV // B_KV_W, OFF_DK // D_W, OFF_DV // D_W

        px = _norm_mm(h, g_mix_l, sc1, sh1, lat_row, (w_in,), layer, lambda j: j,
                      IN_W, TM_PROJ, TN_IN, "in_proj")
        ys = (
            _gmlp(px, a_norm_g, a_ws, a_bs, layer),
            _win_attn(px, pc, kcb, vcb, b_sink[layer], rope_tabs, batch),
            _short_conv(px, conv_w8, layer, SEQ),
            _na_attn(px, pc, kcd, vcd, bias_tab, batch),
        )
        z = _merge(px, ys, w_branch, layer)
        h = _res_mm(z, w_out, layer, h, gt1, lat_row, TM_OUT, TN_OUT, "out_proj")

        if not last:
            ysc = (
                _gmlp(pc, a_norm_g, a_ws, a_bs, layer),
                _ctx_gqa(pc, b_sink[layer], batch),
                _short_conv(pc, conv_w8, layer, CTX_LEN),
                _ctx_mha(pc, batch),
            )
            zc = _merge(pc, ysc, w_branch, layer)
            hc = _res_mm(zc, w_out, layer, hc, gt1, ctx_row, TM_OUT, TN_OUT, "ctx_out_proj")

        j = layer // 2
        if layer % 2 == 0:
            hm = _norm_mm(h, g_ffn_l, sc2, sh2, lat_row, (ffn_w1, ffn_w3), j, lambda n: n,
                          FFN_DIM, TM_PROJ, TN_FFN, "ffn_up")
            h = _res_mm(hm, ffn_w2, j, h, gt2, lat_row, TM_DOWN, TN_DOWN, "ffn_down")
            if not last:
                hmc = _norm_mm(hc, g_ffn_l, sc2, sh2, ctx_row, (ffn_w1, ffn_w3), j, lambda n: n,
                               FFN_DIM, TM_PROJ, TN_FFN, "ctx_ffn_up")
                hc = _res_mm(hmc, ffn_w2, j, hc, gt2, ctx_row, TM_DOWN, TN_DOWN, "ctx_ffn_down")
        else:
            assert last, "the context MoE path would only be needed for an odd layer that is not the last"
            wr_pad = jnp.zeros((D_MODEL, ROUTE_LANES), F32).at[:, :N_EXPERTS].set(w_router[j])
            xn, route = _router(h, g_ffn_l, sc2, sh2, lat_row, wr_pad)
            pos, row_token, tile_expert, n_used = _route_plan(route, TM_MOE)
            xs = _gather_rows(xn, row_token[None, :], 1, flat=True)
            hm = _moe_up(xs, moe_w1, moe_w3, j, tile_expert, n_used)
            ye = _moe_down(hm, moe_w2, j, tile_expert, n_used)
            y01 = _gather_rows(ye.reshape(-1, D_MODEL // LANES, LANES), pos, TOP_K)
            out = _final(h, y01, route, gt2, lat_row, g_final.reshape(1, D_MODEL))

    return out.reshape(batch, seq, D_MODEL)
```

```python
import functools

import jax
import jax.numpy as jnp
import numpy as np
from jax import lax
from jax.experimental import pallas as pl
from jax.experimental.pallas import tpu as pltpu

F32 = jnp.float32
BF16 = jnp.bfloat16

D_MODEL = 2048
SEQ = 2048
CTX_LEN = 256
GRID_W = 64
HEAD_DIM = 128
ROPE_BASE = 10000.0
NORM_EPS = 1e-6
NEG_INF = -1e30

CHUNK = 128
A_GROUPS = 4
A_CH = 512
B_HEADS = 8
B_KV_HEADS = 2
B_GROUP = B_HEADS // B_KV_HEADS
B_WINDOW = 128
B_Q_W = B_HEADS * HEAD_DIM
B_KV_W = B_KV_HEADS * HEAD_DIM
C_W = 512
D_HEADS = 4
D_W = D_HEADS * HEAD_DIM
NA_KH = 8
NA_KW = 16

OFF_AU = 0
OFF_AV = OFF_AU + A_CH
OFF_BQ = OFF_AV + A_CH
OFF_BK = OFF_BQ + B_Q_W
OFF_BV = OFF_BK + B_KV_W
OFF_CB = OFF_BV + B_KV_W
OFF_CC = OFF_CB + C_W
OFF_CH = OFF_CC + C_W
OFF_DQ = OFF_CH + C_W
OFF_DK = OFF_DQ + D_W
OFF_DV = OFF_DK + D_W
OFF_G = OFF_DV + D_W
IN_W = OFF_G + 4 * D_MODEL
MIX_W = A_CH + B_Q_W + C_W + D_W

FFN_DIM = 7168
N_EXPERTS = 8
TOP_K = 2

LANES = 128
BF16_SUBLANES = 16
MIB = 2**20

TM_PROJ = 1024
TN_KV = 512
TN_IN = 1536
TN_FFN = 1024
TM_OUT, TN_OUT = 1024, 1024
TM_DOWN, TN_DOWN = 1024, 256
TM_MERGE, TN_MERGE = 1024, 512
TM_MOE = 512
TN_MOE_UP = 1024
TN_MOE_DOWN = 512
TG_ROWS = 512
GATHER_UNROLL = 8
MOD_ROWS = 16


def _cparams(sem, vmem_mib):
    return pltpu.CompilerParams(dimension_semantics=sem, vmem_limit_bytes=vmem_mib * MIB)


def _silu(a):
    return a * jax.nn.sigmoid(a)


def _ada_kernel(c_ref, w_ref, b_ref, o_ref):
    c = c_ref[...]
    s = _silu(c).astype(BF16)
    o_ref[...] = jnp.dot(s, w_ref[...].astype(BF16), preferred_element_type=F32) + b_ref[...]


def _ada(cond, w_ada, b_ada):
    depth, _, n = w_ada.shape
    tn = 1024
    return pl.pallas_call(
        _ada_kernel,
        out_shape=jax.ShapeDtypeStruct((depth, MOD_ROWS, n), F32),
        grid=(depth, n // tn),
        in_specs=[
            pl.BlockSpec((MOD_ROWS, D_MODEL), lambda l, j: (0, 0)),
            pl.BlockSpec((None, D_MODEL, tn), lambda l, j: (l, 0, j)),
            pl.BlockSpec((None, 1, tn), lambda l, j: (l, 0, j)),
        ],
        out_specs=pl.BlockSpec((None, MOD_ROWS, tn), lambda l, j: (l, 0, j)),
        compiler_params=_cparams(("arbitrary", "arbitrary"), 40),
        name="ada_modulation",
    )(cond, w_ada, b_ada.reshape(depth, 1, n))


def _norm_mod(x, g, sc, sh):
    y = x * lax.rsqrt(jnp.mean(x * x, axis=-1, keepdims=True) + NORM_EPS) * g
    return y * (1.0 + sc) + sh


MXU_COLS = 256


def _swiglu_store(x, w1_ref, w3_ref, o_ref):
    for c0 in range(0, o_ref.shape[1], MXU_COLS):
        cs = slice(c0, c0 + MXU_COLS)
        a = jnp.dot(x, w1_ref[:, cs].astype(BF16), preferred_element_type=F32)
        b = jnp.dot(x, w3_ref[:, cs].astype(BF16), preferred_element_type=F32)
        o_ref[:, cs] = (_silu(a) * b).astype(o_ref.dtype)


def _norm_mm_kernel(x_ref, g_ref, sc_ref, sh_ref, *rest, swiglu):
    n_w = 2 if swiglu else 1
    w_refs, o_ref, xn_ref = rest[:n_w], rest[n_w], rest[n_w + 1]

    @pl.when(pl.program_id(1) == 0)
    def _():
        xn_ref[...] = _norm_mod(x_ref[...], g_ref[...], sc_ref[0], sh_ref[0]).astype(BF16)

    xn = xn_ref[...]
    if swiglu:
        _swiglu_store(xn, w_refs[0], w_refs[1], o_ref)
    else:
        o_ref[...] = jnp.dot(xn, w_refs[0][...].astype(BF16), preferred_element_type=F32).astype(o_ref.dtype)


def _norm_mm(x, g, sc, sh, mod_row, weights, layer, col_block, n_out, tm, tn, name):
    m = x.shape[0]
    swiglu = len(weights) == 2
    w_spec = pl.BlockSpec((None, D_MODEL, tn), lambda i, j: (layer, 0, col_block(j)))
    return pl.pallas_call(
        functools.partial(_norm_mm_kernel, swiglu=swiglu),
        out_shape=jax.ShapeDtypeStruct((m, n_out), BF16),
        grid=(m // tm, n_out // tn),
        in_specs=[
            pl.BlockSpec((tm, D_MODEL), lambda i, j: (i, 0)),
            pl.BlockSpec((1, D_MODEL), lambda i, j: (0, 0)),
            pl.BlockSpec((1, 1, D_MODEL), lambda i, j: (mod_row(i, tm), 0, 0)),
            pl.BlockSpec((1, 1, D_MODEL), lambda i, j: (mod_row(i, tm), 0, 0)),
        ] + [w_spec] * len(weights),
        out_specs=pl.BlockSpec((tm, tn), lambda i, j: (i, j)),
        scratch_shapes=[pltpu.VMEM((tm, D_MODEL), BF16)],
        compiler_params=_cparams(("arbitrary", "arbitrary"), 56),
        name=name,
    )(x, g, sc, sh, *weights)


def _res_mm_kernel(x_ref, w_ref, res_ref, gt_ref, o_ref):
    acc = jnp.dot(x_ref[...], w_ref[...].astype(BF16), preferred_element_type=F32)
    o_ref[...] = res_ref[...] + gt_ref[0] * acc


def _res_mm(x, w, layer, res, gt, mod_row, tm, tn, name):
    m, k = x.shape
    n = res.shape[1]
    return pl.pallas_call(
        _res_mm_kernel,
        out_shape=jax.ShapeDtypeStruct((m, n), F32),
        grid=(m // tm, n // tn),
        in_specs=[
            pl.BlockSpec((tm, k), lambda i, j: (i, 0)),
            pl.BlockSpec((None, k, tn), lambda i, j: (layer, 0, j)),
            pl.BlockSpec((tm, tn), lambda i, j: (i, j)),
            pl.BlockSpec((1, 1, tn), lambda i, j: (mod_row(i, tm), 0, j)),
        ],
        out_specs=pl.BlockSpec((tm, tn), lambda i, j: (i, j)),
        compiler_params=_cparams(("arbitrary", "arbitrary"), 56),
        name=name,
    )(x, w, res, gt)


def _gmlp_kernel(u_ref, v_ref, g_ref, ws_ref, bs_ref, o_ref):
    v = v_ref[...].astype(F32)
    vn = (v * lax.rsqrt(jnp.mean(v * v, axis=-1, keepdims=True) + NORM_EPS) * g_ref[...]).astype(BF16)
    rows = v.shape[0]
    for gi in range(A_GROUPS):
        w = ws_ref[gi].astype(BF16)
        bias = bs_ref[gi]
        cs = slice(gi * LANES, (gi + 1) * LANES)
        for c in range(rows // CHUNK):
            rs = slice(c * CHUNK, (c + 1) * CHUNK)
            s = jnp.dot(w, vn[rs, cs], preferred_element_type=F32) + bias
            o_ref[rs, cs] = (u_ref[rs, cs].astype(F32) * s).astype(o_ref.dtype)


def _gmlp(p, norm_g, ws, bs, layer):
    m = p.shape[0]
    tm = 512
    return pl.pallas_call(
        _gmlp_kernel,
        out_shape=jax.ShapeDtypeStruct((m, A_CH), BF16),
        grid=(m // tm,),
        in_specs=[
            pl.BlockSpec((tm, A_CH), lambda i: (i, OFF_AU // A_CH)),
            pl.BlockSpec((tm, A_CH), lambda i: (i, OFF_AV // A_CH)),
            pl.BlockSpec((1, A_CH), lambda i: (0, 0)),
            pl.BlockSpec((None, A_GROUPS, CHUNK, CHUNK), lambda i: (layer, 0, 0, 0)),
            pl.BlockSpec((None, A_GROUPS, CHUNK, 1), lambda i: (layer, 0, 0, 0)),
        ],
        out_specs=pl.BlockSpec((tm, A_CH), lambda i: (i, 0)),
        compiler_params=_cparams(("arbitrary",), 32),
        name="mixer_a_gmlp",
    )(p, p, norm_g[layer].reshape(1, A_CH), ws, bs.reshape(bs.shape + (1,)))


CONV_ROWS = 256
HALO = BF16_SUBLANES


def _conv_kernel(b_ref, c_ref, h_ref, cp_ref, hp_ref, cn_ref, hn_ref, w_ref, o_ref, *, tiles_per_seq):
    i = pl.program_id(0)
    z = c_ref[...].astype(F32) * h_ref[...].astype(F32)
    n = z.shape[0]
    first = (i % tiles_per_seq) == 0
    last = (i % tiles_per_seq) == tiles_per_seq - 1
    zp = jnp.where(first, 0.0, cp_ref[HALO - 1:HALO, :].astype(F32) * hp_ref[HALO - 1:HALO, :].astype(F32))
    zn = jnp.where(last, 0.0, cn_ref[0:1, :].astype(F32) * hn_ref[0:1, :].astype(F32))
    pos = lax.broadcasted_iota(jnp.int32, z.shape, 0)
    z_prev = jnp.where(pos == 0, zp, pltpu.roll(z, 1, axis=0))
    z_next = jnp.where(pos == n - 1, zn, pltpu.roll(z, n - 1, axis=0))
    w = w_ref[...]
    conv = z_prev * w[0:1] + z * w[1:2] + z_next * w[2:3]
    o_ref[...] = (b_ref[...].astype(F32) * conv).astype(o_ref.dtype)


def _short_conv(p, conv_w8, layer, seq_len):
    m = p.shape[0]
    tr = CONV_ROWS
    per = tr // HALO
    n_halo = m // HALO

    def prev_blk(col):
        return lambda i: (jnp.maximum(i * per - 1, 0), col)

    def next_blk(col):
        return lambda i: (jnp.minimum((i + 1) * per, n_halo - 1), col)

    cc, ch = OFF_CC // C_W, OFF_CH // C_W
    return pl.pallas_call(
        functools.partial(_conv_kernel, tiles_per_seq=seq_len // tr),
        out_shape=jax.ShapeDtypeStruct((m, C_W), BF16),
        grid=(m // tr,),
        in_specs=[
            pl.BlockSpec((tr, C_W), lambda i: (i, OFF_CB // C_W)),
            pl.BlockSpec((tr, C_W), lambda i: (i, cc)),
            pl.BlockSpec((tr, C_W), lambda i: (i, ch)),
            pl.BlockSpec((HALO, C_W), prev_blk(cc)),
            pl.BlockSpec((HALO, C_W), prev_blk(ch)),
            pl.BlockSpec((HALO, C_W), next_blk(cc)),
            pl.BlockSpec((HALO, C_W), next_blk(ch)),
            pl.BlockSpec((None, 8, C_W), lambda i: (layer, 0, 0)),
        ],
        out_specs=pl.BlockSpec((tr, C_W), lambda i: (i, 0)),
        compiler_params=_cparams(("arbitrary",), 32),
        name="mixer_c_conv",
    )(p, p, p, p, p, p, p, conv_w8)


def _rope(x, cos, sin_lo, sin_hi):
    x = x.astype(F32)
    return x * cos + pltpu.roll(x, LANES - 32, axis=1) * sin_lo + pltpu.roll(x, 32, axis=1) * sin_hi


def _softmax_av(s_parts, v_parts, extra_logit=None):
    m = s_parts[0].max(axis=-1, keepdims=True)
    for s in s_parts[1:]:
        m = jnp.maximum(m, s.max(axis=-1, keepdims=True))
    if extra_logit is not None:
        m = jnp.maximum(m, extra_logit)
    denom = jnp.exp(extra_logit - m) if extra_logit is not None else 0.0
    acc = None
    for s, v in zip(s_parts, v_parts):
        e = jnp.exp(s - m)
        denom = denom + e.sum(axis=-1, keepdims=True)
        pv = jnp.dot(e.astype(BF16), v, preferred_element_type=F32)
        acc = pv if acc is None else acc + pv
    return acc / denom


def _qk(q, k):
    return lax.dot_general(q, k, (((1,), (1,)), ((), ())), preferred_element_type=F32)


def _sink_column(sink_ref, h, rows_per_head):
    rid = lax.broadcasted_iota(jnp.int32, (B_GROUP * rows_per_head, 1), 0) // rows_per_head
    col = jnp.zeros((B_GROUP * rows_per_head, 1), F32)
    for g in range(B_GROUP):
        col = jnp.where(rid == g, sink_ref[h * B_GROUP + g], col)
    return col


def _win_attn_kernel(sink_ref, q_ref, k_ref, v_ref, kc_ref, vc_ref, cos_ref, slo_ref, shi_ref, mask_ref, o_ref):
    n = pl.program_id(1)
    blk = B_WINDOW
    band = 3 * blk
    kblk = jnp.clip(n - 1, 0, SEQ // blk - 3)
    q0 = pl.multiple_of(n * blk, blk)
    k0 = pl.multiple_of(kblk * blk, blk)
    scale = HEAD_DIM ** -0.5

    cos_q, slo_q, shi_q = cos_ref[pl.ds(q0, blk), :], slo_ref[pl.ds(q0, blk), :], shi_ref[pl.ds(q0, blk), :]
    cos_k, slo_k, shi_k = cos_ref[pl.ds(k0, band), :], slo_ref[pl.ds(k0, band), :], shi_ref[pl.ds(k0, band), :]
    window_bias = mask_ref[n - kblk]

    for h in range(B_KV_HEADS):
        hs = slice(h * HEAD_DIM, (h + 1) * HEAD_DIM)
        k_loc = _rope(k_ref[pl.ds(k0, band), hs], cos_k, slo_k, shi_k).astype(BF16)
        v_loc = v_ref[pl.ds(k0, band), hs]
        k_ctx = kc_ref[:, hs]
        v_ctx = vc_ref[:, hs]
        qs = []
        for g in range(B_GROUP):
            c0 = (h * B_GROUP + g) * HEAD_DIM
            qs.append(_rope(q_ref[:, c0:c0 + HEAD_DIM], cos_q, slo_q, shi_q).astype(BF16))
        qst = jnp.concatenate(qs, axis=0)
        s_loc = _qk(qst, k_loc) * scale + window_bias
        s_ctx = _qk(qst, k_ctx) * scale
        out = _softmax_av([s_ctx, s_loc], [v_ctx, v_loc], _sink_column(sink_ref, h, blk))
        for g in range(B_GROUP):
            c0 = (h * B_GROUP + g) * HEAD_DIM
            o_ref[:, c0:c0 + HEAD_DIM] = out[g * blk:(g + 1) * blk].astype(o_ref.dtype)


def _window_bias_table():
    i = np.arange(B_WINDOW)[:, None]
    j = np.arange(3 * B_WINDOW)[None, :]
    tabs = []
    for c in range(3):
        rel = j - c * B_WINDOW - i
        one = np.where(np.abs(rel) <= B_WINDOW, 0.0, NEG_INF).astype(np.float32)
        tabs.append(np.tile(one, (B_GROUP, 1)))
    return jnp.asarray(np.stack(tabs))


def _win_attn(p, pc, kc_blk, vc_blk, sink, rope_tabs, batch):
    nb = SEQ // B_WINDOW
    kvw = B_KV_W
    mask_tab = _window_bias_table()
    return pl.pallas_call(
        _win_attn_kernel,
        out_shape=jax.ShapeDtypeStruct((batch * SEQ, B_Q_W), BF16),
        grid=(batch, nb),
        in_specs=[
            pl.BlockSpec(memory_space=pltpu.SMEM),
            pl.BlockSpec((B_WINDOW, B_Q_W), lambda b, n: (b * nb + n, OFF_BQ // B_Q_W)),
            pl.BlockSpec((SEQ, kvw), lambda b, n: (b, OFF_BK // kvw)),
            pl.BlockSpec((SEQ, kvw), lambda b, n: (b, OFF_BV // kvw)),
            pl.BlockSpec((CTX_LEN, kvw), lambda b, n: (b, kc_blk)),
            pl.BlockSpec((CTX_LEN, kvw), lambda b, n: (b, vc_blk)),
            pl.BlockSpec((SEQ, HEAD_DIM), lambda b, n: (0, 0)),
            pl.BlockSpec((SEQ, HEAD_DIM), lambda b, n: (0, 0)),
            pl.BlockSpec((SEQ, HEAD_DIM), lambda b, n: (0, 0)),
            pl.BlockSpec(mask_tab.shape, lambda b, n: (0, 0, 0)),
        ],
        out_specs=pl.BlockSpec((B_WINDOW, B_Q_W), lambda b, n: (b * nb + n, 0)),
        compiler_params=_cparams(("arbitrary", "arbitrary"), 40),
        name="mixer_b_window_attn",
    )(sink, p, p, p, pc, pc, *rope_tabs, mask_tab)


def _ctx_gqa_kernel(sink_ref, q_ref, kc_ref, vc_ref, o_ref):
    scale = HEAD_DIM ** -0.5
    lc = q_ref.shape[0]
    for h in range(B_KV_HEADS):
        hs = slice(h * HEAD_DIM, (h + 1) * HEAD_DIM)
        qst = jnp.concatenate(
            [q_ref[:, (h * B_GROUP + g) * HEAD_DIM:(h * B_GROUP + g + 1) * HEAD_DIM] for g in range(B_GROUP)],
            axis=0)
        out = _softmax_av([_qk(qst, kc_ref[:, hs]) * scale], [vc_ref[:, hs]], _sink_column(sink_ref, h, lc))
        for g in range(B_GROUP):
            c0 = (h * B_GROUP + g) * HEAD_DIM
            o_ref[:, c0:c0 + HEAD_DIM] = out[g * lc:(g + 1) * lc].astype(o_ref.dtype)


def _ctx_gqa(pc, sink, batch):
    kvw = B_KV_W
    return pl.pallas_call(
        _ctx_gqa_kernel,
        out_shape=jax.ShapeDtypeStruct((batch * CTX_LEN, B_Q_W), BF16),
        grid=(batch,),
        in_specs=[
            pl.BlockSpec(memory_space=pltpu.SMEM),
            pl.BlockSpec((CTX_LEN, B_Q_W), lambda b: (b, OFF_BQ // B_Q_W)),
            pl.BlockSpec((CTX_LEN, kvw), lambda b: (b, OFF_BK // kvw)),
            pl.BlockSpec((CTX_LEN, kvw), lambda b: (b, OFF_BV // kvw)),
        ],
        out_specs=pl.BlockSpec((CTX_LEN, B_Q_W), lambda b: (b, 0)),
        compiler_params=_cparams(("arbitrary",), 32),
        name="mixer_b_ctx_attn",
    )(sink, pc, pc, pc)


NA_ROWS_PER_STEP = 4
GRID_H = SEQ // GRID_W
NA_LOC = NA_KH * GRID_W


def _na_kernel(q_ref, k_ref, v_ref, kc_ref, vc_ref, bias_ref, o_ref):
    step = pl.program_id(1)
    scale = HEAD_DIM ** -0.5
    blocks, s_loc, s_ctx = [], [], []
    for rr in range(NA_ROWS_PER_STEP):
        r = step * NA_ROWS_PER_STEP + rr
        rs = jnp.clip(r - NA_KH // 2, 0, GRID_H - NA_KH)
        k0 = pl.multiple_of(rs * GRID_W, GRID_W)
        d = r - rs
        qrows = slice(rr * GRID_W, (rr + 1) * GRID_W)
        for h in range(D_HEADS):
            hs = slice(h * HEAD_DIM, (h + 1) * HEAD_DIM)
            q = q_ref[qrows, hs]
            s_loc.append(_qk(q, k_ref[pl.ds(k0, NA_LOC), hs]) * scale + bias_ref[h, d])
            s_ctx.append(_qk(q, kc_ref[:, hs]) * scale)
            blocks.append((qrows, hs, k0))
    s_loc = jnp.concatenate(s_loc, axis=0)
    s_ctx = jnp.concatenate(s_ctx, axis=0)
    m = jnp.maximum(s_loc.max(axis=-1, keepdims=True), s_ctx.max(axis=-1, keepdims=True))
    e_loc = jnp.exp(s_loc - m)
    e_ctx = jnp.exp(s_ctx - m)
    denom = e_loc.sum(axis=-1, keepdims=True) + e_ctx.sum(axis=-1, keepdims=True)
    e_loc = e_loc.astype(BF16)
    e_ctx = e_ctx.astype(BF16)
    for i, (qrows, hs, k0) in enumerate(blocks):
        rows = slice(i * GRID_W, (i + 1) * GRID_W)
        acc = jnp.dot(e_ctx[rows], vc_ref[:, hs], preferred_element_type=F32)
        acc = acc + jnp.dot(e_loc[rows], v_ref[pl.ds(k0, NA_LOC), hs], preferred_element_type=F32)
        o_ref[qrows, hs] = (acc / denom[rows]).astype(o_ref.dtype)


def _na_bias_table(rpb):
    rpb = rpb.astype(F32)
    rows = []
    for c in range(GRID_W):
        cstart = min(max(c - NA_KW // 2, 0), GRID_W - NA_KW)
        first_dc = cstart - c + NA_KW - 1
        win = rpb[:, :, first_dc:first_dc + NA_KW]
        rows.append(jnp.pad(win, ((0, 0), (0, 0), (cstart, GRID_W - NA_KW - cstart)), constant_values=NEG_INF))
    tab0 = jnp.stack(rows, axis=2)
    slabs = []
    for d in range(NA_KH):
        sl = tab0[:, NA_KH - 1 - d:2 * NA_KH - 1 - d]
        slabs.append(jnp.transpose(sl, (0, 2, 1, 3)).reshape(D_HEADS, GRID_W, NA_LOC))
    return jnp.stack(slabs, axis=1)


def _na_attn(p, pc, kc_blk, vc_blk, bias_tab, batch):
    rows_q = NA_ROWS_PER_STEP * GRID_W
    steps = SEQ // rows_q
    return pl.pallas_call(
        _na_kernel,
        out_shape=jax.ShapeDtypeStruct((batch * SEQ, D_W), BF16),
        grid=(batch, steps),
        in_specs=[
            pl.BlockSpec((rows_q, D_W), lambda b, s: (b * steps + s, OFF_DQ // D_W)),
            pl.BlockSpec((SEQ, D_W), lambda b, s: (b, OFF_DK // D_W)),
            pl.BlockSpec((SEQ, D_W), lambda b, s: (b, OFF_DV // D_W)),
            pl.BlockSpec((CTX_LEN, D_W), lambda b, s: (b, kc_blk)),
            pl.BlockSpec((CTX_LEN, D_W), lambda b, s: (b, vc_blk)),
            pl.BlockSpec((D_HEADS, NA_KH, GRID_W, NA_LOC), lambda b, s: (0, 0, 0, 0)),
        ],
        out_specs=pl.BlockSpec((rows_q, D_W), lambda b, s: (b * steps + s, 0)),
        compiler_params=_cparams(("arbitrary", "arbitrary"), 48),
        name="mixer_d_neighborhood_attn",
    )(p, p, p, pc, pc, bias_tab)


def _ctx_mha_kernel(q_ref, kc_ref, vc_ref, o_ref):
    scale = HEAD_DIM ** -0.5
    for h in range(D_HEADS):
        hs = slice(h * HEAD_DIM, (h + 1) * HEAD_DIM)
        s = _qk(q_ref[:, hs], kc_ref[:, hs]) * scale
        o_ref[:, hs] = _softmax_av([s], [vc_ref[:, hs]]).astype(o_ref.dtype)


def _ctx_mha(pc, batch):
    return pl.pallas_call(
        _ctx_mha_kernel,
        out_shape=jax.ShapeDtypeStruct((batch * CTX_LEN, D_W), BF16),
        grid=(batch,),
        in_specs=[
            pl.BlockSpec((CTX_LEN, D_W), lambda b: (b, OFF_DQ // D_W)),
            pl.BlockSpec((CTX_LEN, D_W), lambda b: (b, OFF_DK // D_W)),
            pl.BlockSpec((CTX_LEN, D_W), lambda b: (b, OFF_DV // D_W)),
        ],
        out_specs=pl.BlockSpec((CTX_LEN, D_W), lambda b: (b, 0)),
        compiler_params=_cparams(("arbitrary",), 32),
        name="mixer_d_ctx_attn",
    )(pc, pc, pc)


BR_W = (A_CH, B_Q_W, C_W, D_W)
BR_OFF = (0, A_CH, A_CH + B_Q_W, A_CH + B_Q_W + C_W)


def _merge_kernel(ya_ref, yb_ref, yc_ref, yd_ref, w_ref, g0_ref, g1_ref, g2_ref, g3_ref, o_ref):
    z = None
    for y_ref, g_ref, off, width in zip((ya_ref, yb_ref, yc_ref, yd_ref),
                                        (g0_ref, g1_ref, g2_ref, g3_ref), BR_OFF, BR_W):
        t = jnp.dot(y_ref[...], w_ref[off:off + width, :].astype(BF16), preferred_element_type=F32)
        t = jax.nn.sigmoid(g_ref[...].astype(F32)) * t
        z = t if z is None else z + t
    o_ref[...] = z.astype(o_ref.dtype)


def _merge(p, ys, w_branch, layer):
    m = p.shape[0]
    tm, tn = TM_MERGE, TN_MERGE
    gate_specs = [pl.BlockSpec((tm, tn), functools.partial(
        lambda i, j, b: (i, (OFF_G + b * D_MODEL) // tn + j), b=b)) for b in range(4)]
    return pl.pallas_call(
        _merge_kernel,
        out_shape=jax.ShapeDtypeStruct((m, D_MODEL), BF16),
        grid=(m // tm, D_MODEL // tn),
        in_specs=[pl.BlockSpec((tm, w), lambda i, j: (i, 0)) for w in BR_W]
        + [pl.BlockSpec((None, MIX_W, tn), lambda i, j: (layer, 0, j))] + gate_specs,
        out_specs=pl.BlockSpec((tm, tn), lambda i, j: (i, j)),
        compiler_params=_cparams(("arbitrary", "arbitrary"), 48),
        name="branch_merge",
    )(*ys, w_branch, p, p, p, p)


ROUTE_LANES = LANES
LANE_IDX = N_EXPERTS
LANE_WT = N_EXPERTS + TOP_K


def _router_kernel(x_ref, g_ref, sc_ref, sh_ref, wr_ref, xn_ref, route_ref):
    xn = _norm_mod(x_ref[...], g_ref[...], sc_ref[0], sh_ref[0])
    xn_ref[...] = xn.astype(BF16).reshape(xn_ref.shape)
    logits = jnp.dot(xn, wr_ref[...], preferred_element_type=F32, precision=lax.Precision.HIGHEST)
    lane = lax.broadcasted_iota(jnp.int32, logits.shape, 1).astype(F32)
    logits = jnp.where(lane < N_EXPERTS, logits, -jnp.inf)
    m1 = logits.max(axis=-1, keepdims=True)
    i1 = jnp.where(logits == m1, lane, float(ROUTE_LANES)).min(axis=-1, keepdims=True)
    rest = jnp.where(lane == i1, -jnp.inf, logits)
    m2 = rest.max(axis=-1, keepdims=True)
    i2 = jnp.where(rest == m2, lane, float(ROUTE_LANES)).min(axis=-1, keepdims=True)
    e2 = jnp.exp(m2 - m1)
    w1 = 1.0 / (1.0 + e2)
    w2 = e2 / (1.0 + e2)
    out = jnp.where(lane == LANE_IDX, i1, 0.0)
    out = jnp.where(lane == LANE_IDX + 1, i2, out)
    out = jnp.where(lane == LANE_WT, w1, out)
    out = jnp.where(lane == LANE_WT + 1, w2, out)
    route_ref[...] = out


def _router(h, g, sc, sh, mod_row, w_router_pad):
    t = h.shape[0]
    tm = 512
    return pl.pallas_call(
        _router_kernel,
        out_shape=(jax.ShapeDtypeStruct((t, D_MODEL // LANES, LANES), BF16),
                   jax.ShapeDtypeStruct((t, ROUTE_LANES), F32)),
        grid=(t // tm,),
        in_specs=[
            pl.BlockSpec((tm, D_MODEL), lambda i: (i, 0)),
            pl.BlockSpec((1, D_MODEL), lambda i: (0, 0)),
            pl.BlockSpec((1, 1, D_MODEL), lambda i: (mod_row(i, tm), 0, 0)),
            pl.BlockSpec((1, 1, D_MODEL), lambda i: (mod_row(i, tm), 0, 0)),
            pl.BlockSpec((D_MODEL, ROUTE_LANES), lambda i: (0, 0)),
        ],
        out_specs=(pl.BlockSpec((tm, D_MODEL // LANES, LANES), lambda i: (i, 0, 0)),
                   pl.BlockSpec((tm, ROUTE_LANES), lambda i: (i, 0))),
        compiler_params=_cparams(("arbitrary",), 40),
        name="moe_router",
    )(h, g, sc, sh, w_router_pad)


def _gather_rows_kernel(idx_ref, src_ref, o_ref, *scratch, n_sources, flat):
    sem = scratch[-1]
    dst_ref = scratch[0] if flat else o_ref
    rows = dst_ref.shape[-3]

    def row_copy(s, r, src_row):
        return pltpu.make_async_copy(src_ref.at[src_row], dst_ref.at[s, r], sem)

    def issue(i, carry):
        for u in range(GATHER_UNROLL):
            r = i * GATHER_UNROLL + u
            for s in range(n_sources):
                row_copy(s, r, idx_ref[0, s, r]).start(priority=(u * n_sources + s) % 2)
        return carry

    lax.fori_loop(0, rows // GATHER_UNROLL, issue, 0)

    def drain(i, carry):
        for u in range(GATHER_UNROLL):
            for s in range(n_sources):
                row_copy(s, i * GATHER_UNROLL + u, 0).wait()
        return carry

    lax.fori_loop(0, rows // GATHER_UNROLL, drain, 0)
    if flat:
        o_ref[...] = dst_ref[0].reshape(o_ref.shape)


def _gather_rows(src3, idx, n_sources, flat=False):
    _, s_dim, lanes = src3.shape
    r = idx.shape[1]
    tg = TG_ROWS
    idx_blocks = idx.reshape(n_sources, r // tg, tg).transpose(1, 0, 2)
    if flat:
        assert n_sources == 1
        out_shape = jax.ShapeDtypeStruct((r, s_dim * lanes), src3.dtype)
        out_spec = pl.BlockSpec((tg, s_dim * lanes), lambda i: (i, 0))
        scratch = [pltpu.VMEM((1, tg, s_dim, lanes), src3.dtype), pltpu.SemaphoreType.DMA]
    else:
        out_shape = jax.ShapeDtypeStruct((n_sources, r, s_dim, lanes), src3.dtype)
        out_spec = pl.BlockSpec((n_sources, tg, s_dim, lanes), lambda i: (0, i, 0, 0))
        scratch = [pltpu.SemaphoreType.DMA]
    return pl.pallas_call(
        functools.partial(_gather_rows_kernel, n_sources=n_sources, flat=flat),
        out_shape=out_shape,
        grid=(r // tg,),
        in_specs=[
            pl.BlockSpec((1, n_sources, tg), lambda i: (i, 0, 0), memory_space=pltpu.SMEM),
            pl.BlockSpec(memory_space=pl.ANY),
        ],
        out_specs=out_spec,
        scratch_shapes=scratch,
        compiler_params=_cparams(("arbitrary",), 32),
        name="row_gather",
    )(idx_blocks, src3)


def _moe_up_kernel(te_ref, nu_ref, x_ref, w1_ref, w3_ref, w2_ref, o_ref, w2b_ref):
    used = pl.program_id(1) < nu_ref[0]
    w2b_ref[...] = w2_ref[...].astype(BF16)

    @pl.when(used)
    def _():
        _swiglu_store(x_ref[...], w1_ref, w3_ref, o_ref)

    @pl.when(jnp.logical_not(used))
    def _():
        o_ref[...] = jnp.zeros_like(o_ref)


def _moe_up(xs, w1, w3, w2, layer, tile_expert, n_used):
    r = xs.shape[0]
    tm, tn = TM_MOE, TN_MOE_UP
    n_t = r // tm
    w2_rows = N_EXPERTS * FFN_DIM
    steps = (FFN_DIM // tn) * n_t
    cast_rows = next(d for d in range(BF16_SUBLANES, w2_rows + 1, BF16_SUBLANES)
                     if w2_rows % d == 0 and w2_rows // d <= steps)
    cast_blocks = w2_rows // cast_rows
    w2_flat = w2.reshape(-1, D_MODEL)

    def cast_blk(j, t, te, nu):
        return (layer * cast_blocks + jnp.minimum(j * n_t + t, cast_blocks - 1), 0)

    def cast_out_blk(j, t, te, nu):
        return (jnp.minimum(j * n_t + t, cast_blocks - 1), 0)

    w_spec = pl.BlockSpec((None, None, D_MODEL, tn), lambda j, t, te, nu: (layer, te[t], 0, j))
    grid_spec = pltpu.PrefetchScalarGridSpec(
        num_scalar_prefetch=2,
        grid=(FFN_DIM // tn, n_t),
        in_specs=[pl.BlockSpec((tm, D_MODEL), lambda j, t, te, nu: (jnp.minimum(t, nu[0] - 1), 0)),
                  w_spec, w_spec,
                  pl.BlockSpec((cast_rows, D_MODEL), cast_blk)],
        out_specs=(pl.BlockSpec((tm, tn), lambda j, t, te, nu: (t, j)),
                   pl.BlockSpec((cast_rows, D_MODEL), cast_out_blk)),
    )
    hm, w2b = pl.pallas_call(
        _moe_up_kernel,
        out_shape=(jax.ShapeDtypeStruct((r, FFN_DIM), BF16),
                   jax.ShapeDtypeStruct((w2_rows, D_MODEL), BF16)),
        grid_spec=grid_spec,
        compiler_params=_cparams(("arbitrary", "arbitrary"), 56),
        name="moe_up",
    )(tile_expert, n_used, xs, w1, w3, w2_flat)
    return hm, w2b.reshape(1, N_EXPERTS, FFN_DIM, D_MODEL)


def _moe_down_kernel(te_ref, nu_ref, x_ref, w_ref, o_ref):
    used = pl.program_id(0) < nu_ref[0]

    @pl.when(used)
    def _():
        acc = jnp.dot(x_ref[...], w_ref[...].astype(BF16), preferred_element_type=F32)
        o_ref[...] = acc.astype(o_ref.dtype)

    @pl.when(jnp.logical_not(used))
    def _():
        o_ref[...] = jnp.zeros_like(o_ref)


def _moe_down(hmid, w2, layer, tile_expert, n_used):
    r = hmid.shape[0]
    tm, tn = TM_MOE, TN_MOE_DOWN
    grid_spec = pltpu.PrefetchScalarGridSpec(
        num_scalar_prefetch=2,
        grid=(r // tm, D_MODEL // tn),
        in_specs=[
            pl.BlockSpec((tm, FFN_DIM), lambda t, j, te, nu: (jnp.minimum(t, nu[0] - 1), 0)),
            pl.BlockSpec((None, None, FFN_DIM, tn), lambda t, j, te, nu: (layer, te[t], 0, j)),
        ],
        out_specs=pl.BlockSpec((tm, tn), lambda t, j, te, nu: (t, j)),
    )
    return pl.pallas_call(
        _moe_down_kernel,
        out_shape=jax.ShapeDtypeStruct((r, D_MODEL), BF16),
        grid_spec=grid_spec,
        compiler_params=_cparams(("arbitrary", "arbitrary"), 56),
        name="moe_down",
    )(tile_expert, n_used, hmid, w2)


def _route_plan(route, tm):
    t = route.shape[0]
    idx = route[:, LANE_IDX:LANE_IDX + TOP_K].astype(jnp.int32)
    e_flat = idx.T.reshape(-1)
    onehot = (e_flat[:, None] == jnp.arange(N_EXPERTS)[None, :]).astype(jnp.int32)
    counts = onehot.sum(axis=0)
    rank = (onehot * (jnp.cumsum(onehot, axis=0) - onehot)).sum(axis=1)
    tiles_e = (counts + tm - 1) // tm
    tile_end = jnp.cumsum(tiles_e)
    tile_start = tile_end - tiles_e
    pos = (onehot * tile_start[None, :]).sum(axis=1) * tm + rank
    n_tiles = TOP_K * t // tm + N_EXPERTS
    n_used = tile_end[-1]
    tile_ids = jnp.minimum(jnp.arange(n_tiles), n_used - 1)
    tile_expert = jnp.minimum((tile_ids[:, None] >= tile_end[None, :]).sum(axis=1), N_EXPERTS - 1)
    row_token = (jnp.arange(n_tiles * tm, dtype=jnp.int32) % t).at[pos].set(
        jnp.tile(jnp.arange(t, dtype=jnp.int32), TOP_K))
    return (pos.reshape(TOP_K, t).astype(jnp.int32), row_token,
            tile_expert.astype(jnp.int32), n_used.reshape(1).astype(jnp.int32))


def _final_kernel(h_ref, y0_ref, y1_ref, route_ref, gt_ref, g_ref, o_ref):
    tm = h_ref.shape[0]
    w1 = route_ref[:, LANE_WT:LANE_WT + 1]
    w2 = route_ref[:, LANE_WT + 1:LANE_WT + 2]
    y0 = y0_ref[0].reshape(tm, D_MODEL).astype(F32)
    y1 = y1_ref[0].reshape(tm, D_MODEL).astype(F32)
    h = h_ref[...] + gt_ref[0] * (w1 * y0 + w2 * y1)
    o_ref[...] = h * lax.rsqrt(jnp.mean(h * h, axis=-1, keepdims=True) + NORM_EPS) * g_ref[...]


def _final(h, y01, route, gt, mod_row, g_final):
    t = h.shape[0]
    tm = 512
    s_dim = D_MODEL // LANES
    return pl.pallas_call(
        _final_kernel,
        out_shape=jax.ShapeDtypeStruct((t, D_MODEL), F32),
        grid=(t // tm,),
        in_specs=[
            pl.BlockSpec((tm, D_MODEL), lambda i: (i, 0)),
            pl.BlockSpec((1, tm, s_dim, LANES), lambda i: (0, i, 0, 0)),
            pl.BlockSpec((1, tm, s_dim, LANES), lambda i: (1, i, 0, 0)),
            pl.BlockSpec((tm, ROUTE_LANES), lambda i: (i, 0)),
            pl.BlockSpec((1, 1, D_MODEL), lambda i: (mod_row(i, tm), 0, 0)),
            pl.BlockSpec((1, D_MODEL), lambda i: (0, 0)),
        ],
        out_specs=pl.BlockSpec((tm, D_MODEL), lambda i: (i, 0)),
        compiler_params=_cparams(("arbitrary",), 48),
        name="final_residual_norm",
    )(h, y01, y01, route, gt, g_final)


def _rope_tables():
    pos = jnp.arange(SEQ)
    f = HEAD_DIM // 4
    inv = ROPE_BASE ** (-jnp.arange(f, dtype=F32) / f)
    ang_row = (pos // GRID_W).astype(F32)[:, None] * inv[None, :]
    ang_col = (pos % GRID_W).astype(F32)[:, None] * inv[None, :]
    ang = jnp.concatenate([ang_row, ang_row, ang_col, ang_col], axis=1)
    cos, sin = jnp.cos(ang), jnp.sin(ang)
    first = jnp.asarray(((np.arange(HEAD_DIM) // f) % 2 == 0)[None, :])
    return cos, jnp.where(first, -sin, 0.0), jnp.where(first, 0.0, sin)


def kernel(x, c, ctx, c_ctx, w_ada, b_ada, g_mix, g_ffn, w_in, a_norm_g, a_ws, a_bs, b_sink, c_conv,
           d_rpb, w_branch, w_out, ffn_w1, ffn_w3, ffn_w2, w_router, moe_w1, moe_w3, moe_w2, g_final):
    batch, seq, _ = x.shape
    depth = w_in.shape[0]
    assert seq == SEQ and ctx.shape[1] == CTX_LEN and batch + 1 <= MOD_ROWS
    assert depth % 2 == 0, "the fused residual + final-norm epilogue lives in the MoE (odd, last) layer"
    t = batch * seq
    tc = batch * CTX_LEN

    cond = jnp.zeros((MOD_ROWS, D_MODEL), F32).at[:batch].set(c).at[batch].set(c_ctx)
    mods = _ada(cond, w_ada, b_ada)
    rope_tabs = _rope_tables()
    conv_w8 = jnp.zeros((depth, 8, C_W), F32).at[:, :3].set(c_conv)

    w_in, w_branch, w_out = w_in.astype(BF16), w_branch.astype(BF16), w_out.astype(BF16)
    ffn_w1, ffn_w3, ffn_w2 = ffn_w1.astype(BF16), ffn_w3.astype(BF16), ffn_w2.astype(BF16)

    def lat_row(i, tm):
        return (i * tm) // SEQ

    def ctx_row(i, tm):
        return batch

    h = x.reshape(t, D_MODEL)
    hc = ctx.reshape(tc, D_MODEL)
    out = None
    for layer in range(depth):
        last = layer == depth - 1
        sh1, sc1, gt1, sh2, sc2, gt2 = [m.reshape(MOD_ROWS, 1, D_MODEL)
                                        for m in jnp.split(mods[layer], 6, axis=-1)]
        g_mix_l = g_mix[layer].reshape(1, D_MODEL)
        g_ffn_l = g_ffn[layer].reshape(1, D_MODEL)
        bias_tab = _na_bias_table(d_rpb[layer])

        if last:
            pc = _norm_mm(hc, g_mix_l, sc1, sh1, ctx_row, (w_in,), layer,
                          lambda j: jnp.where(j == 0, OFF_BK // TN_KV, OFF_DK // TN_KV - 1 + j),
                          3 * TN_KV, TM_PROJ, TN_KV, "ctx_kv_proj")
            kcb, vcb, kcd, vcd = 0, 1, 1, 2
        else:
            pc = _norm_mm(hc, g_mix_l, sc1, sh1, ctx_row, (w_in,), layer, lambda j: j,
                          IN_W, TM_PROJ, TN_IN, "ctx_in_proj")
            kcb, vcb, kcd, vcd = OFF_BK // B_KV_W, OFF_BV // B_KV_W, OFF_DK // D_W, OFF_DV // D_W

        px = _norm_mm(h, g_mix_l, sc1, sh1, lat_row, (w_in,), layer, lambda j: j,
                      IN_W, TM_PROJ, TN_IN, "in_proj")
        ys = (
            _gmlp(px, a_norm_g, a_ws, a_bs, layer),
            _win_attn(px, pc, kcb, vcb, b_sink[layer], rope_tabs, batch),
            _short_conv(px, conv_w8, layer, SEQ),
            _na_attn(px, pc, kcd, vcd, bias_tab, batch),
        )
        z = _merge(px, ys, w_branch, layer)
        h = _res_mm(z, w_out, layer, h, gt1, lat_row, TM_OUT, TN_OUT, "out_proj")

        if not last:
            ysc = (
                _gmlp(pc, a_norm_g, a_ws, a_bs, layer),
                _ctx_gqa(pc, b_sink[layer], batch),
                _short_conv(pc, conv_w8, layer, CTX_LEN),
                _ctx_mha(pc, batch),
            )
            zc = _merge(pc, ysc, w_branch, layer)
            hc = _res_mm(zc, w_out, layer, hc, gt1, ctx_row, TM_OUT, TN_OUT, "ctx_out_proj")

        j = layer // 2
        if layer % 2 == 0:
            hm = _norm_mm(h, g_ffn_l, sc2, sh2, lat_row, (ffn_w1, ffn_w3), j, lambda n: n,
                          FFN_DIM, TM_PROJ, TN_FFN, "ffn_up")
            h = _res_mm(hm, ffn_w2, j, h, gt2, lat_row, TM_DOWN, TN_DOWN, "ffn_down")
            if not last:
                hmc = _norm_mm(hc, g_ffn_l, sc2, sh2, ctx_row, (ffn_w1, ffn_w3), j, lambda n: n,
                               FFN_DIM, TM_PROJ, TN_FFN, "ctx_ffn_up")
                hc = _res_mm(hmc, ffn_w2, j, hc, gt2, ctx_row, TM_DOWN, TN_DOWN, "ctx_ffn_down")
        else:
            assert last, "the context MoE path would only be needed for an odd layer that is not the last"
            wr_pad = jnp.zeros((D_MODEL, ROUTE_LANES), F32).at[:, :N_EXPERTS].set(w_router[j])
            xn, route = _router(h, g_ffn_l, sc2, sh2, lat_row, wr_pad)
            pos, row_token, tile_expert, n_used = _route_plan(route, TM_MOE)
            xs = _gather_rows(xn, row_token[None, :], 1, flat=True)
            hm, w2b = _moe_up(xs, moe_w1, moe_w3, moe_w2, j, tile_expert, n_used)
            ye = _moe_down(hm, w2b, 0, tile_expert, n_used)
            y01 = _gather_rows(ye.reshape(-1, D_MODEL // LANES, LANES), pos, TOP_K)
            out = _final(h, y01, route, gt2, lat_row, g_final.reshape(1, D_MODEL))

    return out.reshape(batch, seq, D_MODEL)
```

```python
import functools

import jax
import jax.numpy as jnp
import numpy as np
from jax import lax
from jax.experimental import pallas as pl
from jax.experimental.pallas import tpu as pltpu

F32 = jnp.float32
BF16 = jnp.bfloat16

D_MODEL = 2048
SEQ = 2048
CTX_LEN = 256
GRID_W = 64
HEAD_DIM = 128
ROPE_BASE = 10000.0
NORM_EPS = 1e-6
NEG_INF = -1e30

CHUNK = 128
A_GROUPS = 4
A_CH = 512
B_HEADS = 8
B_KV_HEADS = 2
B_GROUP = B_HEADS // B_KV_HEADS
B_WINDOW = 128
B_Q_W = B_HEADS * HEAD_DIM
B_KV_W = B_KV_HEADS * HEAD_DIM
C_W = 512
D_HEADS = 4
D_W = D_HEADS * HEAD_DIM
NA_KH = 8
NA_KW = 16

OFF_AU = 0
OFF_AV = OFF_AU + A_CH
OFF_BQ = OFF_AV + A_CH
OFF_BK = OFF_BQ + B_Q_W
OFF_BV = OFF_BK + B_KV_W
OFF_CB = OFF_BV + B_KV_W
OFF_CC = OFF_CB + C_W
OFF_CH = OFF_CC + C_W
OFF_DQ = OFF_CH + C_W
OFF_DK = OFF_DQ + D_W
OFF_DV = OFF_DK + D_W
OFF_G = OFF_DV + D_W
IN_W = OFF_G + 4 * D_MODEL
MIX_W = A_CH + B_Q_W + C_W + D_W

FFN_DIM = 7168
N_EXPERTS = 8
TOP_K = 2

LANES = 128
BF16_SUBLANES = 16
MIB = 2**20

TM_PROJ = 1024
TN_KV = 512
TN_IN = 1536
TN_FFN = 1024
TM_OUT, TN_OUT = 1024, 1024
TM_DOWN, TN_DOWN = 1024, 256
TM_MERGE, TN_MERGE = 1024, 512
TM_MOE = 512
TN_MOE_UP = 1024
TN_MOE_DOWN = 1024
TG_ROWS = 512
GATHER_UNROLL = 8
MOD_ROWS = 16


def _cparams(sem, vmem_mib):
    return pltpu.CompilerParams(dimension_semantics=sem, vmem_limit_bytes=vmem_mib * MIB)


def _silu(a):
    return a * jax.nn.sigmoid(a)


def _ada_kernel(c_ref, w_ref, b_ref, o_ref):
    c = c_ref[...]
    s = _silu(c).astype(BF16)
    o_ref[...] = jnp.dot(s, w_ref[...].astype(BF16), preferred_element_type=F32) + b_ref[...]


def _ada(cond, w_ada, b_ada):
    depth, _, n = w_ada.shape
    tn = 1024
    return pl.pallas_call(
        _ada_kernel,
        out_shape=jax.ShapeDtypeStruct((depth, MOD_ROWS, n), F32),
        grid=(depth, n // tn),
        in_specs=[
            pl.BlockSpec((MOD_ROWS, D_MODEL), lambda l, j: (0, 0)),
            pl.BlockSpec((None, D_MODEL, tn), lambda l, j: (l, 0, j)),
            pl.BlockSpec((None, 1, tn), lambda l, j: (l, 0, j)),
        ],
        out_specs=pl.BlockSpec((None, MOD_ROWS, tn), lambda l, j: (l, 0, j)),
        compiler_params=_cparams(("arbitrary", "arbitrary"), 40),
        name="ada_modulation",
    )(cond, w_ada, b_ada.reshape(depth, 1, n))


def _norm_mod(x, g, sc, sh):
    y = x * lax.rsqrt(jnp.mean(x * x, axis=-1, keepdims=True) + NORM_EPS) * g
    return y * (1.0 + sc) + sh


MXU_COLS = 256


def _swiglu_store(x, w1_ref, w3_ref, o_ref):
    for c0 in range(0, o_ref.shape[1], MXU_COLS):
        cs = slice(c0, c0 + MXU_COLS)
        a = jnp.dot(x, w1_ref[:, cs].astype(BF16), preferred_element_type=F32)
        b = jnp.dot(x, w3_ref[:, cs].astype(BF16), preferred_element_type=F32)
        o_ref[:, cs] = (_silu(a) * b).astype(o_ref.dtype)


def _norm_mm_kernel(x_ref, g_ref, sc_ref, sh_ref, *rest, swiglu):
    n_w = 2 if swiglu else 1
    w_refs, o_ref, xn_ref = rest[:n_w], rest[n_w], rest[n_w + 1]

    @pl.when(pl.program_id(1) == 0)
    def _():
        xn_ref[...] = _norm_mod(x_ref[...], g_ref[...], sc_ref[0], sh_ref[0]).astype(BF16)

    xn = xn_ref[...]
    if swiglu:
        _swiglu_store(xn, w_refs[0], w_refs[1], o_ref)
    else:
        o_ref[...] = jnp.dot(xn, w_refs[0][...].astype(BF16), preferred_element_type=F32).astype(o_ref.dtype)


def _norm_mm(x, g, sc, sh, mod_row, weights, layer, col_block, n_out, tm, tn, name):
    m = x.shape[0]
    swiglu = len(weights) == 2
    w_spec = pl.BlockSpec((None, D_MODEL, tn), lambda i, j: (layer, 0, col_block(j)))
    return pl.pallas_call(
        functools.partial(_norm_mm_kernel, swiglu=swiglu),
        out_shape=jax.ShapeDtypeStruct((m, n_out), BF16),
        grid=(m // tm, n_out // tn),
        in_specs=[
            pl.BlockSpec((tm, D_MODEL), lambda i, j: (i, 0)),
            pl.BlockSpec((1, D_MODEL), lambda i, j: (0, 0)),
            pl.BlockSpec((1, 1, D_MODEL), lambda i, j: (mod_row(i, tm), 0, 0)),
            pl.BlockSpec((1, 1, D_MODEL), lambda i, j: (mod_row(i, tm), 0, 0)),
        ] + [w_spec] * len(weights),
        out_specs=pl.BlockSpec((tm, tn), lambda i, j: (i, j)),
        scratch_shapes=[pltpu.VMEM((tm, D_MODEL), BF16)],
        compiler_params=_cparams(("arbitrary", "arbitrary"), 56),
        name=name,
    )(x, g, sc, sh, *weights)


def _res_mm_kernel(x_ref, w_ref, res_ref, gt_ref, o_ref):
    acc = jnp.dot(x_ref[...], w_ref[...].astype(BF16), preferred_element_type=F32)
    o_ref[...] = res_ref[...] + gt_ref[0] * acc


def _res_mm(x, w, layer, res, gt, mod_row, tm, tn, name):
    m, k = x.shape
    n = res.shape[1]
    return pl.pallas_call(
        _res_mm_kernel,
        out_shape=jax.ShapeDtypeStruct((m, n), F32),
        grid=(m // tm, n // tn),
        in_specs=[
            pl.BlockSpec((tm, k), lambda i, j: (i, 0)),
            pl.BlockSpec((None, k, tn), lambda i, j: (layer, 0, j)),
            pl.BlockSpec((tm, tn), lambda i, j: (i, j)),
            pl.BlockSpec((1, 1, tn), lambda i, j: (mod_row(i, tm), 0, j)),
        ],
        out_specs=pl.BlockSpec((tm, tn), lambda i, j: (i, j)),
        compiler_params=_cparams(("arbitrary", "arbitrary"), 56),
        name=name,
    )(x, w, res, gt)


def _gmlp_kernel(u_ref, v_ref, g_ref, ws_ref, bs_ref, o_ref):
    v = v_ref[...].astype(F32)
    vn = (v * lax.rsqrt(jnp.mean(v * v, axis=-1, keepdims=True) + NORM_EPS) * g_ref[...]).astype(BF16)
    rows = v.shape[0]
    for gi in range(A_GROUPS):
        w = ws_ref[gi].astype(BF16)
        bias = bs_ref[gi]
        cs = slice(gi * LANES, (gi + 1) * LANES)
        for c in range(rows // CHUNK):
            rs = slice(c * CHUNK, (c + 1) * CHUNK)
            s = jnp.dot(w, vn[rs, cs], preferred_element_type=F32) + bias
            o_ref[rs, cs] = (u_ref[rs, cs].astype(F32) * s).astype(o_ref.dtype)


def _gmlp(p, norm_g, ws, bs, layer):
    m = p.shape[0]
    tm = 512
    return pl.pallas_call(
        _gmlp_kernel,
        out_shape=jax.ShapeDtypeStruct((m, A_CH), BF16),
        grid=(m // tm,),
        in_specs=[
            pl.BlockSpec((tm, A_CH), lambda i: (i, OFF_AU // A_CH)),
            pl.BlockSpec((tm, A_CH), lambda i: (i, OFF_AV // A_CH)),
            pl.BlockSpec((1, A_CH), lambda i: (0, 0)),
            pl.BlockSpec((None, A_GROUPS, CHUNK, CHUNK), lambda i: (layer, 0, 0, 0)),
            pl.BlockSpec((None, A_GROUPS, CHUNK, 1), lambda i: (layer, 0, 0, 0)),
        ],
        out_specs=pl.BlockSpec((tm, A_CH), lambda i: (i, 0)),
        compiler_params=_cparams(("arbitrary",), 32),
        name="mixer_a_gmlp",
    )(p, p, norm_g[layer].reshape(1, A_CH), ws, bs.reshape(bs.shape + (1,)))


CONV_ROWS = 256
HALO = BF16_SUBLANES


def _conv_kernel(b_ref, c_ref, h_ref, cp_ref, hp_ref, cn_ref, hn_ref, w_ref, o_ref, *, tiles_per_seq):
    i = pl.program_id(0)
    z = c_ref[...].astype(F32) * h_ref[...].astype(F32)
    n = z.shape[0]
    first = (i % tiles_per_seq) == 0
    last = (i % tiles_per_seq) == tiles_per_seq - 1
    zp = jnp.where(first, 0.0, cp_ref[HALO - 1:HALO, :].astype(F32) * hp_ref[HALO - 1:HALO, :].astype(F32))
    zn = jnp.where(last, 0.0, cn_ref[0:1, :].astype(F32) * hn_ref[0:1, :].astype(F32))
    pos = lax.broadcasted_iota(jnp.int32, z.shape, 0)
    z_prev = jnp.where(pos == 0, zp, pltpu.roll(z, 1, axis=0))
    z_next = jnp.where(pos == n - 1, zn, pltpu.roll(z, n - 1, axis=0))
    w = w_ref[...]
    conv = z_prev * w[0:1] + z * w[1:2] + z_next * w[2:3]
    o_ref[...] = (b_ref[...].astype(F32) * conv).astype(o_ref.dtype)


def _short_conv(p, conv_w8, layer, seq_len):
    m = p.shape[0]
    tr = CONV_ROWS
    per = tr // HALO
    n_halo = m // HALO

    def prev_blk(col):
        return lambda i: (jnp.maximum(i * per - 1, 0), col)

    def next_blk(col):
        return lambda i: (jnp.minimum((i + 1) * per, n_halo - 1), col)

    cc, ch = OFF_CC // C_W, OFF_CH // C_W
    return pl.pallas_call(
        functools.partial(_conv_kernel, tiles_per_seq=seq_len // tr),
        out_shape=jax.ShapeDtypeStruct((m, C_W), BF16),
        grid=(m // tr,),
        in_specs=[
            pl.BlockSpec((tr, C_W), lambda i: (i, OFF_CB // C_W)),
            pl.BlockSpec((tr, C_W), lambda i: (i, cc)),
            pl.BlockSpec((tr, C_W), lambda i: (i, ch)),
            pl.BlockSpec((HALO, C_W), prev_blk(cc)),
            pl.BlockSpec((HALO, C_W), prev_blk(ch)),
            pl.BlockSpec((HALO, C_W), next_blk(cc)),
            pl.BlockSpec((HALO, C_W), next_blk(ch)),
            pl.BlockSpec((None, 8, C_W), lambda i: (layer, 0, 0)),
        ],
        out_specs=pl.BlockSpec((tr, C_W), lambda i: (i, 0)),
        compiler_params=_cparams(("arbitrary",), 32),
        name="mixer_c_conv",
    )(p, p, p, p, p, p, p, conv_w8)


def _rope(x, cos, sin_lo, sin_hi):
    x = x.astype(F32)
    return x * cos + pltpu.roll(x, LANES - 32, axis=1) * sin_lo + pltpu.roll(x, 32, axis=1) * sin_hi


def _softmax_av(s_parts, v_parts, extra_logit=None):
    m = s_parts[0].max(axis=-1, keepdims=True)
    for s in s_parts[1:]:
        m = jnp.maximum(m, s.max(axis=-1, keepdims=True))
    if extra_logit is not None:
        m = jnp.maximum(m, extra_logit)
    denom = jnp.exp(extra_logit - m) if extra_logit is not None else 0.0
    acc = None
    for s, v in zip(s_parts, v_parts):
        e = jnp.exp(s - m)
        denom = denom + e.sum(axis=-1, keepdims=True)
        pv = jnp.dot(e.astype(BF16), v, preferred_element_type=F32)
        acc = pv if acc is None else acc + pv
    return acc / denom


def _qk(q, k):
    return lax.dot_general(q, k, (((1,), (1,)), ((), ())), preferred_element_type=F32)


def _sink_column(sink_ref, h, rows_per_head):
    rid = lax.broadcasted_iota(jnp.int32, (B_GROUP * rows_per_head, 1), 0) // rows_per_head
    col = jnp.zeros((B_GROUP * rows_per_head, 1), F32)
    for g in range(B_GROUP):
        col = jnp.where(rid == g, sink_ref[h * B_GROUP + g], col)
    return col


def _win_attn_kernel(sink_ref, q_ref, k_ref, v_ref, kc_ref, vc_ref, cos_ref, slo_ref, shi_ref,
                     qcos_ref, qslo_ref, qshi_ref, mask_ref, o_ref):
    n = pl.program_id(1)
    blk = B_WINDOW
    band = 3 * blk
    kblk = jnp.clip(n - 1, 0, SEQ // blk - 3)
    q0 = pl.multiple_of(n * blk, blk)
    k0 = pl.multiple_of(kblk * blk, blk)

    cos_q, slo_q, shi_q = qcos_ref[pl.ds(q0, blk), :], qslo_ref[pl.ds(q0, blk), :], qshi_ref[pl.ds(q0, blk), :]
    cos_k, slo_k, shi_k = cos_ref[pl.ds(k0, band), :], slo_ref[pl.ds(k0, band), :], shi_ref[pl.ds(k0, band), :]
    window_bias = mask_ref[n - kblk]

    for h in range(B_KV_HEADS):
        hs = slice(h * HEAD_DIM, (h + 1) * HEAD_DIM)
        k_loc = _rope(k_ref[pl.ds(k0, band), hs], cos_k, slo_k, shi_k).astype(BF16)
        v_loc = v_ref[pl.ds(k0, band), hs]
        k_ctx = kc_ref[:, hs]
        v_ctx = vc_ref[:, hs]
        qs = []
        for g in range(B_GROUP):
            c0 = (h * B_GROUP + g) * HEAD_DIM
            qs.append(_rope(q_ref[:, c0:c0 + HEAD_DIM], cos_q, slo_q, shi_q).astype(BF16))
        qst = jnp.concatenate(qs, axis=0)
        s_loc = _qk(qst, k_loc) + window_bias
        s_ctx = _qk(qst, k_ctx)
        out = _softmax_av([s_ctx, s_loc], [v_ctx, v_loc], _sink_column(sink_ref, h, blk))
        for g in range(B_GROUP):
            c0 = (h * B_GROUP + g) * HEAD_DIM
            o_ref[:, c0:c0 + HEAD_DIM] = out[g * blk:(g + 1) * blk].astype(o_ref.dtype)


def _window_bias_table():
    i = np.arange(B_WINDOW)[:, None]
    j = np.arange(3 * B_WINDOW)[None, :]
    tabs = []
    for c in range(3):
        rel = j - c * B_WINDOW - i
        one = np.where(np.abs(rel) <= B_WINDOW, 0.0, NEG_INF).astype(np.float32)
        tabs.append(np.tile(one, (B_GROUP, 1)))
    return jnp.asarray(np.stack(tabs))


def _win_attn(p, pc, kc_blk, vc_blk, sink, rope_tabs, batch):
    nb = SEQ // B_WINDOW
    kvw = B_KV_W
    mask_tab = _window_bias_table()
    return pl.pallas_call(
        _win_attn_kernel,
        out_shape=jax.ShapeDtypeStruct((batch * SEQ, B_Q_W), BF16),
        grid=(batch, nb),
        in_specs=[
            pl.BlockSpec(memory_space=pltpu.SMEM),
            pl.BlockSpec((B_WINDOW, B_Q_W), lambda b, n: (b * nb + n, OFF_BQ // B_Q_W)),
            pl.BlockSpec((SEQ, kvw), lambda b, n: (b, OFF_BK // kvw)),
            pl.BlockSpec((SEQ, kvw), lambda b, n: (b, OFF_BV // kvw)),
            pl.BlockSpec((CTX_LEN, kvw), lambda b, n: (b, kc_blk)),
            pl.BlockSpec((CTX_LEN, kvw), lambda b, n: (b, vc_blk)),
            pl.BlockSpec((SEQ, HEAD_DIM), lambda b, n: (0, 0)),
            pl.BlockSpec((SEQ, HEAD_DIM), lambda b, n: (0, 0)),
            pl.BlockSpec((SEQ, HEAD_DIM), lambda b, n: (0, 0)),
            pl.BlockSpec((SEQ, HEAD_DIM), lambda b, n: (0, 0)),
            pl.BlockSpec((SEQ, HEAD_DIM), lambda b, n: (0, 0)),
            pl.BlockSpec((SEQ, HEAD_DIM), lambda b, n: (0, 0)),
            pl.BlockSpec(mask_tab.shape, lambda b, n: (0, 0, 0)),
        ],
        out_specs=pl.BlockSpec((B_WINDOW, B_Q_W), lambda b, n: (b * nb + n, 0)),
        compiler_params=_cparams(("arbitrary", "arbitrary"), 40),
        name="mixer_b_window_attn",
    )(sink, p, p, p, pc, pc, *rope_tabs, *[tab * HEAD_DIM ** -0.5 for tab in rope_tabs], mask_tab)


def _ctx_gqa_kernel(sink_ref, q_ref, kc_ref, vc_ref, o_ref):
    scale = HEAD_DIM ** -0.5
    lc = q_ref.shape[0]
    for h in range(B_KV_HEADS):
        hs = slice(h * HEAD_DIM, (h + 1) * HEAD_DIM)
        qst = jnp.concatenate(
            [q_ref[:, (h * B_GROUP + g) * HEAD_DIM:(h * B_GROUP + g + 1) * HEAD_DIM] for g in range(B_GROUP)],
            axis=0)
        out = _softmax_av([_qk(qst, kc_ref[:, hs]) * scale], [vc_ref[:, hs]], _sink_column(sink_ref, h, lc))
        for g in range(B_GROUP):
            c0 = (h * B_GROUP + g) * HEAD_DIM
            o_ref[:, c0:c0 + HEAD_DIM] = out[g * lc:(g + 1) * lc].astype(o_ref.dtype)


def _ctx_gqa(pc, sink, batch):
    kvw = B_KV_W
    return pl.pallas_call(
        _ctx_gqa_kernel,
        out_shape=jax.ShapeDtypeStruct((batch * CTX_LEN, B_Q_W), BF16),
        grid=(batch,),
        in_specs=[
            pl.BlockSpec(memory_space=pltpu.SMEM),
            pl.BlockSpec((CTX_LEN, B_Q_W), lambda b: (b, OFF_BQ // B_Q_W)),
            pl.BlockSpec((CTX_LEN, kvw), lambda b: (b, OFF_BK // kvw)),
            pl.BlockSpec((CTX_LEN, kvw), lambda b: (b, OFF_BV // kvw)),
        ],
        out_specs=pl.BlockSpec((CTX_LEN, B_Q_W), lambda b: (b, 0)),
        compiler_params=_cparams(("arbitrary",), 32),
        name="mixer_b_ctx_attn",
    )(sink, pc, pc, pc)


NA_ROWS_PER_STEP = 4
GRID_H = SEQ // GRID_W
NA_LOC = NA_KH * GRID_W


def _na_kernel(q_ref, k_ref, v_ref, kc_ref, vc_ref, bias_ref, o_ref):
    step = pl.program_id(1)
    scale = HEAD_DIM ** -0.5
    blocks, s_loc, s_ctx = [], [], []
    for rr in range(NA_ROWS_PER_STEP):
        r = step * NA_ROWS_PER_STEP + rr
        rs = jnp.clip(r - NA_KH // 2, 0, GRID_H - NA_KH)
        k0 = pl.multiple_of(rs * GRID_W, GRID_W)
        d = r - rs
        qrows = slice(rr * GRID_W, (rr + 1) * GRID_W)
        for h in range(D_HEADS):
            hs = slice(h * HEAD_DIM, (h + 1) * HEAD_DIM)
            q = (q_ref[qrows, hs].astype(F32) * scale).astype(BF16)
            s_loc.append(_qk(q, k_ref[pl.ds(k0, NA_LOC), hs]) + bias_ref[h, d])
            s_ctx.append(_qk(q, kc_ref[:, hs]))
            blocks.append((qrows, hs, k0))
    s_loc = jnp.concatenate(s_loc, axis=0)
    s_ctx = jnp.concatenate(s_ctx, axis=0)
    m = jnp.maximum(s_loc.max(axis=-1, keepdims=True), s_ctx.max(axis=-1, keepdims=True))
    e_loc = jnp.exp(s_loc - m)
    e_ctx = jnp.exp(s_ctx - m)
    denom = e_loc.sum(axis=-1, keepdims=True) + e_ctx.sum(axis=-1, keepdims=True)
    e_loc = e_loc.astype(BF16)
    e_ctx = e_ctx.astype(BF16)
    for i, (qrows, hs, k0) in enumerate(blocks):
        rows = slice(i * GRID_W, (i + 1) * GRID_W)
        acc = jnp.dot(e_ctx[rows], vc_ref[:, hs], preferred_element_type=F32)
        acc = acc + jnp.dot(e_loc[rows], v_ref[pl.ds(k0, NA_LOC), hs], preferred_element_type=F32)
        o_ref[qrows, hs] = (acc / denom[rows]).astype(o_ref.dtype)


def _na_bias_table(rpb):
    rpb = rpb.astype(F32)
    rows = []
    for c in range(GRID_W):
        cstart = min(max(c - NA_KW // 2, 0), GRID_W - NA_KW)
        first_dc = cstart - c + NA_KW - 1
        win = rpb[:, :, first_dc:first_dc + NA_KW]
        rows.append(jnp.pad(win, ((0, 0), (0, 0), (cstart, GRID_W - NA_KW - cstart)), constant_values=NEG_INF))
    tab0 = jnp.stack(rows, axis=2)
    slabs = []
    for d in range(NA_KH):
        sl = tab0[:, NA_KH - 1 - d:2 * NA_KH - 1 - d]
        slabs.append(jnp.transpose(sl, (0, 2, 1, 3)).reshape(D_HEADS, GRID_W, NA_LOC))
    return jnp.stack(slabs, axis=1)


def _na_attn(p, pc, kc_blk, vc_blk, bias_tab, batch):
    rows_q = NA_ROWS_PER_STEP * GRID_W
    steps = SEQ // rows_q
    return pl.pallas_call(
        _na_kernel,
        out_shape=jax.ShapeDtypeStruct((batch * SEQ, D_W), BF16),
        grid=(batch, steps),
        in_specs=[
            pl.BlockSpec((rows_q, D_W), lambda b, s: (b * steps + s, OFF_DQ // D_W)),
            pl.BlockSpec((SEQ, D_W), lambda b, s: (b, OFF_DK // D_W)),
            pl.BlockSpec((SEQ, D_W), lambda b, s: (b, OFF_DV // D_W)),
            pl.BlockSpec((CTX_LEN, D_W), lambda b, s: (b, kc_blk)),
            pl.BlockSpec((CTX_LEN, D_W), lambda b, s: (b, vc_blk)),
            pl.BlockSpec((D_HEADS, NA_KH, GRID_W, NA_LOC), lambda b, s: (0, 0, 0, 0)),
        ],
        out_specs=pl.BlockSpec((rows_q, D_W), lambda b, s: (b * steps + s, 0)),
        compiler_params=_cparams(("arbitrary", "arbitrary"), 48),
        name="mixer_d_neighborhood_attn",
    )(p, p, p, pc, pc, bias_tab)


def _ctx_mha_kernel(q_ref, kc_ref, vc_ref, o_ref):
    scale = HEAD_DIM ** -0.5
    for h in range(D_HEADS):
        hs = slice(h * HEAD_DIM, (h + 1) * HEAD_DIM)
        s = _qk(q_ref[:, hs], kc_ref[:, hs]) * scale
        o_ref[:, hs] = _softmax_av([s], [vc_ref[:, hs]]).astype(o_ref.dtype)


def _ctx_mha(pc, batch):
    return pl.pallas_call(
        _ctx_mha_kernel,
        out_shape=jax.ShapeDtypeStruct((batch * CTX_LEN, D_W), BF16),
        grid=(batch,),
        in_specs=[
            pl.BlockSpec((CTX_LEN, D_W), lambda b: (b, OFF_DQ // D_W)),
            pl.BlockSpec((CTX_LEN, D_W), lambda b: (b, OFF_DK // D_W)),
            pl.BlockSpec((CTX_LEN, D_W), lambda b: (b, OFF_DV // D_W)),
        ],
        out_specs=pl.BlockSpec((CTX_LEN, D_W), lambda b: (b, 0)),
        compiler_params=_cparams(("arbitrary",), 32),
        name="mixer_d_ctx_attn",
    )(pc, pc, pc)


BR_W = (A_CH, B_Q_W, C_W, D_W)
BR_OFF = (0, A_CH, A_CH + B_Q_W, A_CH + B_Q_W + C_W)


def _merge_kernel(ya_ref, yb_ref, yc_ref, yd_ref, w_ref, g0_ref, g1_ref, g2_ref, g3_ref, o_ref):
    z = None
    for y_ref, g_ref, off, width in zip((ya_ref, yb_ref, yc_ref, yd_ref),
                                        (g0_ref, g1_ref, g2_ref, g3_ref), BR_OFF, BR_W):
        t = jnp.dot(y_ref[...], w_ref[off:off + width, :].astype(BF16), preferred_element_type=F32)
        t = jax.nn.sigmoid(g_ref[...].astype(F32)) * t
        z = t if z is None else z + t
    o_ref[...] = z.astype(o_ref.dtype)


def _merge(p, ys, w_branch, layer):
    m = p.shape[0]
    tm, tn = TM_MERGE, TN_MERGE
    gate_specs = [pl.BlockSpec((tm, tn), functools.partial(
        lambda i, j, b: (i, (OFF_G + b * D_MODEL) // tn + j), b=b)) for b in range(4)]
    return pl.pallas_call(
        _merge_kernel,
        out_shape=jax.ShapeDtypeStruct((m, D_MODEL), BF16),
        grid=(m // tm, D_MODEL // tn),
        in_specs=[pl.BlockSpec((tm, w), lambda i, j: (i, 0)) for w in BR_W]
        + [pl.BlockSpec((None, MIX_W, tn), lambda i, j: (layer, 0, j))] + gate_specs,
        out_specs=pl.BlockSpec((tm, tn), lambda i, j: (i, j)),
        compiler_params=_cparams(("arbitrary", "arbitrary"), 48),
        name="branch_merge",
    )(*ys, w_branch, p, p, p, p)


ROUTE_LANES = LANES
LANE_IDX = N_EXPERTS
LANE_WT = N_EXPERTS + TOP_K


def _router_kernel(x_ref, g_ref, sc_ref, sh_ref, wr_ref, xn_ref, route_ref):
    xn = _norm_mod(x_ref[...], g_ref[...], sc_ref[0], sh_ref[0])
    xn_ref[...] = xn.astype(BF16).reshape(xn_ref.shape)
    x_hi = xn.astype(BF16)
    x_lo = (xn - x_hi.astype(F32)).astype(BF16)
    wr = wr_ref[...]
    w_hi = wr.astype(BF16)
    w_lo = (wr - w_hi.astype(F32)).astype(BF16)
    logits = (jnp.dot(x_hi, w_hi, preferred_element_type=F32)
              + (jnp.dot(x_hi, w_lo, preferred_element_type=F32) + jnp.dot(x_lo, w_hi, preferred_element_type=F32)))
    lane = lax.broadcasted_iota(jnp.int32, logits.shape, 1).astype(F32)
    logits = jnp.where(lane < N_EXPERTS, logits, -jnp.inf)
    m1 = logits.max(axis=-1, keepdims=True)
    i1 = jnp.where(logits == m1, lane, float(ROUTE_LANES)).min(axis=-1, keepdims=True)
    rest = jnp.where(lane == i1, -jnp.inf, logits)
    m2 = rest.max(axis=-1, keepdims=True)
    i2 = jnp.where(rest == m2, lane, float(ROUTE_LANES)).min(axis=-1, keepdims=True)
    e2 = jnp.exp(m2 - m1)
    w1 = 1.0 / (1.0 + e2)
    w2 = e2 / (1.0 + e2)
    out = jnp.where(lane == LANE_IDX, i1, 0.0)
    out = jnp.where(lane == LANE_IDX + 1, i2, out)
    out = jnp.where(lane == LANE_WT, w1, out)
    out = jnp.where(lane == LANE_WT + 1, w2, out)
    route_ref[...] = out


def _router(h, g, sc, sh, mod_row, w_router_pad):
    t = h.shape[0]
    tm = 512
    return pl.pallas_call(
        _router_kernel,
        out_shape=(jax.ShapeDtypeStruct((t, D_MODEL // LANES, LANES), BF16),
                   jax.ShapeDtypeStruct((t, ROUTE_LANES), F32)),
        grid=(t // tm,),
        in_specs=[
            pl.BlockSpec((tm, D_MODEL), lambda i: (i, 0)),
            pl.BlockSpec((1, D_MODEL), lambda i: (0, 0)),
            pl.BlockSpec((1, 1, D_MODEL), lambda i: (mod_row(i, tm), 0, 0)),
            pl.BlockSpec((1, 1, D_MODEL), lambda i: (mod_row(i, tm), 0, 0)),
            pl.BlockSpec((D_MODEL, ROUTE_LANES), lambda i: (0, 0)),
        ],
        out_specs=(pl.BlockSpec((tm, D_MODEL // LANES, LANES), lambda i: (i, 0, 0)),
                   pl.BlockSpec((tm, ROUTE_LANES), lambda i: (i, 0))),
        compiler_params=_cparams(("arbitrary",), 40),
        name="moe_router",
    )(h, g, sc, sh, w_router_pad)


def _gather_rows_kernel(idx_ref, src_ref, o_ref, *scratch, n_sources, flat):
    sem = scratch[-1]
    dst_ref = scratch[0] if flat else o_ref
    rows = dst_ref.shape[-3]

    def row_copy(s, r, src_row):
        return pltpu.make_async_copy(src_ref.at[src_row], dst_ref.at[s, r], sem)

    def issue(i, carry):
        for u in range(GATHER_UNROLL):
            r = i * GATHER_UNROLL + u
            for s in range(n_sources):
                row_copy(s, r, idx_ref[0, s, r]).start(priority=(u * n_sources + s) % 2)
        return carry

    lax.fori_loop(0, rows // GATHER_UNROLL, issue, 0)

    def drain(i, carry):
        for u in range(GATHER_UNROLL):
            for s in range(n_sources):
                row_copy(s, i * GATHER_UNROLL + u, 0).wait()
        return carry

    lax.fori_loop(0, rows // GATHER_UNROLL, drain, 0)
    if flat:
        o_ref[...] = dst_ref[0].reshape(o_ref.shape)


def _gather_rows(src3, idx, n_sources, flat=False):
    _, s_dim, lanes = src3.shape
    r = idx.shape[1]
    tg = TG_ROWS
    idx_blocks = idx.reshape(n_sources, r // tg, tg).transpose(1, 0, 2)
    if flat:
        assert n_sources == 1
        out_shape = jax.ShapeDtypeStruct((r, s_dim * lanes), src3.dtype)
        out_spec = pl.BlockSpec((tg, s_dim * lanes), lambda i: (i, 0))
        scratch = [pltpu.VMEM((1, tg, s_dim, lanes), src3.dtype), pltpu.SemaphoreType.DMA]
    else:
        out_shape = jax.ShapeDtypeStruct((n_sources, r, s_dim, lanes), src3.dtype)
        out_spec = pl.BlockSpec((n_sources, tg, s_dim, lanes), lambda i: (0, i, 0, 0))
        scratch = [pltpu.SemaphoreType.DMA]
    return pl.pallas_call(
        functools.partial(_gather_rows_kernel, n_sources=n_sources, flat=flat),
        out_shape=out_shape,
        grid=(r // tg,),
        in_specs=[
            pl.BlockSpec((1, n_sources, tg), lambda i: (i, 0, 0), memory_space=pltpu.SMEM),
            pl.BlockSpec(memory_space=pl.ANY),
        ],
        out_specs=out_spec,
        scratch_shapes=scratch,
        compiler_params=_cparams(("arbitrary",), 32),
        name="row_gather",
    )(idx_blocks, src3)


def _moe_up_kernel(te_ref, nu_ref, x_ref, w1_ref, w3_ref, w2_ref, o_ref, w2b_ref):
    used = pl.program_id(1) < nu_ref[0]
    w2b_ref[...] = w2_ref[...].astype(BF16)

    @pl.when(used)
    def _():
        _swiglu_store(x_ref[...], w1_ref, w3_ref, o_ref)

    @pl.when(jnp.logical_not(used))
    def _():
        o_ref[...] = jnp.zeros_like(o_ref)


def _moe_up(xs, w1, w3, w2, layer, tile_expert, n_used):
    r = xs.shape[0]
    tm, tn = TM_MOE, TN_MOE_UP
    n_t = r // tm
    w2_rows = N_EXPERTS * FFN_DIM
    steps = (FFN_DIM // tn) * n_t
    cast_rows = next(d for d in range(BF16_SUBLANES, w2_rows + 1, BF16_SUBLANES)
                     if w2_rows % d == 0 and w2_rows // d <= steps)
    cast_blocks = w2_rows // cast_rows
    w2_flat = w2.reshape(-1, D_MODEL)

    def cast_blk(j, t, te, nu):
        return (layer * cast_blocks + jnp.minimum(j * n_t + t, cast_blocks - 1), 0)

    def cast_out_blk(j, t, te, nu):
        return (jnp.minimum(j * n_t + t, cast_blocks - 1), 0)

    w_spec = pl.BlockSpec((None, None, D_MODEL, tn), lambda j, t, te, nu: (layer, te[t], 0, j))
    grid_spec = pltpu.PrefetchScalarGridSpec(
        num_scalar_prefetch=2,
        grid=(FFN_DIM // tn, n_t),
        in_specs=[pl.BlockSpec((tm, D_MODEL), lambda j, t, te, nu: (jnp.minimum(t, nu[0] - 1), 0)),
                  w_spec, w_spec,
                  pl.BlockSpec((cast_rows, D_MODEL), cast_blk)],
        out_specs=(pl.BlockSpec((tm, tn), lambda j, t, te, nu: (t, j)),
                   pl.BlockSpec((cast_rows, D_MODEL), cast_out_blk)),
    )
    hm, w2b = pl.pallas_call(
        _moe_up_kernel,
        out_shape=(jax.ShapeDtypeStruct((r, FFN_DIM), BF16),
                   jax.ShapeDtypeStruct((w2_rows, D_MODEL), BF16)),
        grid_spec=grid_spec,
        compiler_params=_cparams(("arbitrary", "arbitrary"), 56),
        name="moe_up",
    )(tile_expert, n_used, xs, w1, w3, w2_flat)
    return hm, w2b.reshape(1, N_EXPERTS, FFN_DIM, D_MODEL)


def _moe_down_kernel(te_ref, nu_ref, x_ref, w_ref, o_ref):
    used = pl.program_id(1) < nu_ref[0]

    @pl.when(used)
    def _():
        x = x_ref[...]
        for c0 in range(0, o_ref.shape[1], MXU_COLS):
            cs = slice(c0, c0 + MXU_COLS)
            o_ref[:, cs] = jnp.dot(x, w_ref[:, cs], preferred_element_type=F32).astype(o_ref.dtype)

    @pl.when(jnp.logical_not(used))
    def _():
        o_ref[...] = jnp.zeros_like(o_ref)


def _moe_down(hmid, w2, layer, tile_expert, n_used):
    r = hmid.shape[0]
    tm, tn = TM_MOE, TN_MOE_DOWN
    grid_spec = pltpu.PrefetchScalarGridSpec(
        num_scalar_prefetch=2,
        grid=(D_MODEL // tn, r // tm),
        in_specs=[
            pl.BlockSpec((tm, FFN_DIM), lambda j, t, te, nu: (jnp.minimum(t, nu[0] - 1), 0)),
            pl.BlockSpec((None, None, FFN_DIM, tn), lambda j, t, te, nu: (layer, te[t], 0, j)),
        ],
        out_specs=pl.BlockSpec((tm, tn), lambda j, t, te, nu: (t, j)),
    )
    return pl.pallas_call(
        _moe_down_kernel,
        out_shape=jax.ShapeDtypeStruct((r, D_MODEL), BF16),
        grid_spec=grid_spec,
        compiler_params=_cparams(("arbitrary", "arbitrary"), 60),
        name="moe_down",
    )(tile_expert, n_used, hmid, w2)


def _route_plan(route, tm):
    t = route.shape[0]
    idx = route[:, LANE_IDX:LANE_IDX + TOP_K].astype(jnp.int32)
    e_flat = idx.T.reshape(-1)
    onehot = (e_flat[:, None] == jnp.arange(N_EXPERTS)[None, :]).astype(jnp.int32)
    counts = onehot.sum(axis=0)
    rank = (onehot * (jnp.cumsum(onehot, axis=0) - onehot)).sum(axis=1)
    tiles_e = (counts + tm - 1) // tm
    tile_end = jnp.cumsum(tiles_e)
    tile_start = tile_end - tiles_e
    pos = (onehot * tile_start[None, :]).sum(axis=1) * tm + rank
    n_tiles = TOP_K * t // tm + N_EXPERTS
    n_used = tile_end[-1]
    tile_ids = jnp.minimum(jnp.arange(n_tiles), n_used - 1)
    tile_expert = jnp.minimum((tile_ids[:, None] >= tile_end[None, :]).sum(axis=1), N_EXPERTS - 1)
    row_token = (jnp.arange(n_tiles * tm, dtype=jnp.int32) % t).at[pos].set(
        jnp.tile(jnp.arange(t, dtype=jnp.int32), TOP_K))
    return (pos.reshape(TOP_K, t).astype(jnp.int32), row_token,
            tile_expert.astype(jnp.int32), n_used.reshape(1).astype(jnp.int32))


FINAL_ROWS = 512


def _final_kernel(idx_ref, idx_next_ref, y_ref, h_ref, route_ref, gt_ref, g_ref, o_ref, buf, sems):
    i = pl.program_id(0)
    half = FINAL_ROWS // 2

    def row_copy(slot, k, r, src_row):
        return pltpu.make_async_copy(y_ref.at[src_row], buf.at[slot, k, r], sems.at[slot])

    def start_half(ids_ref, slot, first_row):
        def body(g, carry):
            for u in range(GATHER_UNROLL):
                r = g * GATHER_UNROLL + u
                for k in range(TOP_K):
                    row_copy(slot, k, r, ids_ref[0, k, first_row + r]).start(priority=(u * TOP_K + k) % 2)
            return carry
        lax.fori_loop(0, half // GATHER_UNROLL, body, 0)

    def wait_half(slot):
        def body(g, carry):
            for u in range(GATHER_UNROLL):
                for k in range(TOP_K):
                    row_copy(slot, k, g * GATHER_UNROLL + u, 0).wait()
            return carry
        lax.fori_loop(0, half // GATHER_UNROLL, body, 0)

    def combine_half(slot, first_row):
        rows = slice(first_row, first_row + half)
        w1 = route_ref[rows, LANE_WT:LANE_WT + 1]
        w2 = route_ref[rows, LANE_WT + 1:LANE_WT + 2]
        y0 = buf[slot, 0].reshape(half, D_MODEL).astype(F32)
        y1 = buf[slot, 1].reshape(half, D_MODEL).astype(F32)
        h = h_ref[rows, :] + gt_ref[0] * (w1 * y0 + w2 * y1)
        o_ref[rows, :] = h * lax.rsqrt(jnp.mean(h * h, axis=-1, keepdims=True) + NORM_EPS) * g_ref[...]

    @pl.when(i == 0)
    def _():
        start_half(idx_ref, 0, 0)

    start_half(idx_ref, 1, half)
    wait_half(0)
    combine_half(0, 0)

    @pl.when(i + 1 < pl.num_programs(0))
    def _():
        start_half(idx_next_ref, 0, 0)

    wait_half(1)
    combine_half(1, half)


def _final(h, ye3, pos, route, gt, mod_row, g_final):
    t = h.shape[0]
    tm = FINAL_ROWS
    n = t // tm
    _, s_dim, lanes = ye3.shape
    idx_blocks = pos.reshape(TOP_K, n, tm).transpose(1, 0, 2)
    return pl.pallas_call(
        _final_kernel,
        out_shape=jax.ShapeDtypeStruct((t, D_MODEL), F32),
        grid=(n,),
        in_specs=[
            pl.BlockSpec((1, TOP_K, tm), lambda i: (i, 0, 0), memory_space=pltpu.SMEM),
            pl.BlockSpec((1, TOP_K, tm), lambda i: (jnp.minimum(i + 1, n - 1), 0, 0), memory_space=pltpu.SMEM),
            pl.BlockSpec(memory_space=pl.ANY),
            pl.BlockSpec((tm, D_MODEL), lambda i: (i, 0)),
            pl.BlockSpec((tm, ROUTE_LANES), lambda i: (i, 0)),
            pl.BlockSpec((1, 1, D_MODEL), lambda i: (mod_row(i, tm), 0, 0)),
            pl.BlockSpec((1, D_MODEL), lambda i: (0, 0)),
        ],
        out_specs=pl.BlockSpec((tm, D_MODEL), lambda i: (i, 0)),
        scratch_shapes=[pltpu.VMEM((2, TOP_K, tm // 2, s_dim, lanes), ye3.dtype),
                        pltpu.SemaphoreType.DMA((2,))],
        compiler_params=_cparams(("arbitrary",), 48),
        name="final_combine_norm",
    )(idx_blocks, idx_blocks, ye3, h, route, gt, g_final)


def _rope_tables():
    pos = jnp.arange(SEQ)
    f = HEAD_DIM // 4
    inv = ROPE_BASE ** (-jnp.arange(f, dtype=F32) / f)
    ang_row = (pos // GRID_W).astype(F32)[:, None] * inv[None, :]
    ang_col = (pos % GRID_W).astype(F32)[:, None] * inv[None, :]
    ang = jnp.concatenate([ang_row, ang_row, ang_col, ang_col], axis=1)
    cos, sin = jnp.cos(ang), jnp.sin(ang)
    first = jnp.asarray(((np.arange(HEAD_DIM) // f) % 2 == 0)[None, :])
    return cos, jnp.where(first, -sin, 0.0), jnp.where(first, 0.0, sin)


def kernel(x, c, ctx, c_ctx, w_ada, b_ada, g_mix, g_ffn, w_in, a_norm_g, a_ws, a_bs, b_sink, c_conv,
           d_rpb, w_branch, w_out, ffn_w1, ffn_w3, ffn_w2, w_router, moe_w1, moe_w3, moe_w2, g_final):
    batch, seq, _ = x.shape
    depth = w_in.shape[0]
    assert seq == SEQ and ctx.shape[1] == CTX_LEN and batch + 1 <= MOD_ROWS
    assert depth % 2 == 0, "the fused residual + final-norm epilogue lives in the MoE (odd, last) layer"
    t = batch * seq
    tc = batch * CTX_LEN

    cond = jnp.zeros((MOD_ROWS, D_MODEL), F32).at[:batch].set(c).at[batch].set(c_ctx)
    mods = _ada(cond, w_ada, b_ada)
    rope_tabs = _rope_tables()
    conv_w8 = jnp.zeros((depth, 8, C_W), F32).at[:, :3].set(c_conv)

    w_in, w_branch, w_out = w_in.astype(BF16), w_branch.astype(BF16), w_out.astype(BF16)
    ffn_w1, ffn_w3, ffn_w2 = ffn_w1.astype(BF16), ffn_w3.astype(BF16), ffn_w2.astype(BF16)

    def lat_row(i, tm):
        return (i * tm) // SEQ

    def ctx_row(i, tm):
        return batch

    h = x.reshape(t, D_MODEL)
    hc = ctx.reshape(tc, D_MODEL)
    out = None
    for layer in range(depth):
        last = layer == depth - 1
        sh1, sc1, gt1, sh2, sc2, gt2 = [m.reshape(MOD_ROWS, 1, D_MODEL)
                                        for m in jnp.split(mods[layer], 6, axis=-1)]
        g_mix_l = g_mix[layer].reshape(1, D_MODEL)
        g_ffn_l = g_ffn[layer].reshape(1, D_MODEL)
        bias_tab = _na_bias_table(d_rpb[layer])

        if last:
            pc = _norm_mm(hc, g_mix_l, sc1, sh1, ctx_row, (w_in,), layer,
                          lambda j: jnp.where(j == 0, OFF_BK // TN_KV, OFF_DK // TN_KV - 1 + j),
                          3 * TN_KV, TM_PROJ, TN_KV, "ctx_kv_proj")
            kcb, vcb, kcd, vcd = 0, 1, 1, 2
        else:
            pc = _norm_mm(hc, g_mix_l, sc1, sh1, ctx_row, (w_in,), layer, lambda j: j,
                          IN_W, TM_PROJ, TN_IN, "ctx_in_proj")
            kcb, vcb, kcd, vcd = OFF_BK // B_KV_W, OFF_BV // B_KV_W, OFF_DK // D_W, OFF_DV // D_W

        px = _norm_mm(h, g_mix_l, sc1, sh1, lat_row, (w_in,), layer, lambda j: j,
                      IN_W, TM_PROJ, TN_IN, "in_proj")
        ys = (
            _gmlp(px, a_norm_g, a_ws, a_bs, layer),
            _win_attn(px, pc, kcb, vcb, b_sink[layer], rope_tabs, batch),
            _short_conv(px, conv_w8, layer, SEQ),
            _na_attn(px, pc, kcd, vcd, bias_tab, batch),
        )
        z = _merge(px, ys, w_branch, layer)
        h = _res_mm(z, w_out, layer, h, gt1, lat_row, TM_OUT, TN_OUT, "out_proj")

        if not last:
            ysc = (
                _gmlp(pc, a_norm_g, a_ws, a_bs, layer),
                _ctx_gqa(pc, b_sink[layer], batch),
                _short_conv(pc, conv_w8, layer, CTX_LEN),
                _ctx_mha(pc, batch),
            )
            zc = _merge(pc, ysc, w_branch, layer)
            hc = _res_mm(zc, w_out, layer, hc, gt1, ctx_row, TM_OUT, TN_OUT, "ctx_out_proj")

        j = layer // 2
        if layer % 2 == 0:
            hm = _norm_mm(h, g_ffn_l, sc2, sh2, lat_row, (ffn_w1, ffn_w3), j, lambda n: n,
                          FFN_DIM, TM_PROJ, TN_FFN, "ffn_up")
            h = _res_mm(hm, ffn_w2, j, h, gt2, lat_row, TM_DOWN, TN_DOWN, "ffn_down")
            if not last:
                hmc = _norm_mm(hc, g_ffn_l, sc2, sh2, ctx_row, (ffn_w1, ffn_w3), j, lambda n: n,
                               FFN_DIM, TM_PROJ, TN_FFN, "ctx_ffn_up")
                hc = _res_mm(hmc, ffn_w2, j, hc, gt2, ctx_row, TM_DOWN, TN_DOWN, "ctx_ffn_down")
        else:
            assert last, "the context MoE path would only be needed for an odd layer that is not the last"
            wr_pad = jnp.zeros((D_MODEL, ROUTE_LANES), F32).at[:, :N_EXPERTS].set(w_router[j])
            xn, route = _router(h, g_ffn_l, sc2, sh2, lat_row, wr_pad)
            pos, row_token, tile_expert, n_used = _route_plan(route, TM_MOE)
            xs = _gather_rows(xn, row_token[None, :], 1, flat=True)
            hm, w2b = _moe_up(xs, moe_w1, moe_w3, moe_w2, j, tile_expert, n_used)
            ye = _moe_down(hm, w2b, 0, tile_expert, n_used)
            out = _final(h, ye.reshape(-1, D_MODEL // LANES, LANES), pos, route, gt2, lat_row,
                         g_final.reshape(1, D_MODEL))

    return out.reshape(batch, seq, D_MODEL)
```

```python
import functools

import jax
import jax.numpy as jnp
import numpy as np
from jax import lax
from jax.experimental import pallas as pl
from jax.experimental.pallas import tpu as pltpu

F32 = jnp.float32
BF16 = jnp.bfloat16

D_MODEL = 2048
SEQ = 2048
CTX_LEN = 256
GRID_W = 64
HEAD_DIM = 128
ROPE_BASE = 10000.0
NORM_EPS = 1e-6
NEG_INF = -1e30

CHUNK = 128
A_GROUPS = 4
A_CH = 512
B_HEADS = 8
B_KV_HEADS = 2
B_GROUP = B_HEADS // B_KV_HEADS
B_WINDOW = 128
B_Q_W = B_HEADS * HEAD_DIM
B_KV_W = B_KV_HEADS * HEAD_DIM
C_W = 512
D_HEADS = 4
D_W = D_HEADS * HEAD_DIM
NA_KH = 8
NA_KW = 16

OFF_AU = 0
OFF_AV = OFF_AU + A_CH
OFF_BQ = OFF_AV + A_CH
OFF_BK = OFF_BQ + B_Q_W
OFF_BV = OFF_BK + B_KV_W
OFF_CB = OFF_BV + B_KV_W
OFF_CC = OFF_CB + C_W
OFF_CH = OFF_CC + C_W
OFF_DQ = OFF_CH + C_W
OFF_DK = OFF_DQ + D_W
OFF_DV = OFF_DK + D_W
OFF_G = OFF_DV + D_W
IN_W = OFF_G + 4 * D_MODEL
MIX_W = A_CH + B_Q_W + C_W + D_W

FFN_DIM = 7168
N_EXPERTS = 8
TOP_K = 2

LANES = 128
BF16_SUBLANES = 16
MIB = 2**20

TM_PROJ = 1024
TN_KV = 512
TN_IN = 1536
TN_FFN = 1024
TM_OUT, TN_OUT = 1024, 1024
TM_DOWN, TN_DOWN = 512, 1024
TM_MERGE, TN_MERGE = 1024, 512
TM_MOE = 512
TN_MOE_UP = 1024
TN_MOE_DOWN = 1024
TG_ROWS = 512
GATHER_UNROLL = 8
MOD_ROWS = 16


def _cparams(sem, vmem_mib):
    return pltpu.CompilerParams(dimension_semantics=sem, vmem_limit_bytes=vmem_mib * MIB)


def _silu(a):
    return a * jax.nn.sigmoid(a)


def _ada_kernel(c_ref, w_ref, b_ref, o_ref):
    c = c_ref[...]
    s = _silu(c).astype(BF16)
    o_ref[...] = jnp.dot(s, w_ref[...].astype(BF16), preferred_element_type=F32) + b_ref[...]


def _ada(cond, w_ada, b_ada):
    depth, _, n = w_ada.shape
    tn = 1024
    return pl.pallas_call(
        _ada_kernel,
        out_shape=jax.ShapeDtypeStruct((depth, MOD_ROWS, n), F32),
        grid=(depth, n // tn),
        in_specs=[
            pl.BlockSpec((MOD_ROWS, D_MODEL), lambda l, j: (0, 0)),
            pl.BlockSpec((None, D_MODEL, tn), lambda l, j: (l, 0, j)),
            pl.BlockSpec((None, 1, tn), lambda l, j: (l, 0, j)),
        ],
        out_specs=pl.BlockSpec((None, MOD_ROWS, tn), lambda l, j: (l, 0, j)),
        compiler_params=_cparams(("arbitrary", "arbitrary"), 40),
        name="ada_modulation",
    )(cond, w_ada, b_ada.reshape(depth, 1, n))


def _norm_mod(x, g, sc, sh):
    y = x * lax.rsqrt(jnp.mean(x * x, axis=-1, keepdims=True) + NORM_EPS) * g
    return y * (1.0 + sc) + sh


MXU_COLS = 256
NORM_CHUNK = 256


def _swiglu_store(x, w1_ref, w3_ref, o_ref):
    for c0 in range(0, o_ref.shape[1], MXU_COLS):
        cs = slice(c0, c0 + MXU_COLS)
        a = jnp.dot(x, w1_ref[:, cs].astype(BF16), preferred_element_type=F32)
        b = jnp.dot(x, w3_ref[:, cs].astype(BF16), preferred_element_type=F32)
        o_ref[:, cs] = (_silu(a) * b).astype(o_ref.dtype)


def _norm_mm_kernel(x_ref, g_ref, sc_ref, sh_ref, *rest, swiglu):
    n_w = 2 if swiglu else 1
    w_refs, o_ref, xn_ref = rest[:n_w], rest[n_w], rest[n_w + 1]

    @pl.when(pl.program_id(1) == 0)
    def _():
        def chunk(c, carry):
            rows = pl.ds(pl.multiple_of(c * NORM_CHUNK, NORM_CHUNK), NORM_CHUNK)
            xn_ref[rows, :] = _norm_mod(x_ref[rows, :], g_ref[...], sc_ref[0], sh_ref[0]).astype(BF16)
            return carry
        lax.fori_loop(0, x_ref.shape[0] // NORM_CHUNK, chunk, 0)

    xn = xn_ref[...]
    if swiglu:
        _swiglu_store(xn, w_refs[0], w_refs[1], o_ref)
    else:
        o_ref[...] = jnp.dot(xn, w_refs[0][...].astype(BF16), preferred_element_type=F32).astype(o_ref.dtype)


def _norm_mm(x, g, sc, sh, mod_row, weights, layer, col_block, n_out, tm, tn, name):
    m = x.shape[0]
    swiglu = len(weights) == 2
    w_spec = pl.BlockSpec((None, D_MODEL, tn), lambda i, j: (layer, 0, col_block(j)))
    return pl.pallas_call(
        functools.partial(_norm_mm_kernel, swiglu=swiglu),
        out_shape=jax.ShapeDtypeStruct((m, n_out), BF16),
        grid=(m // tm, n_out // tn),
        in_specs=[
            pl.BlockSpec((tm, D_MODEL), lambda i, j: (i, 0)),
            pl.BlockSpec((1, D_MODEL), lambda i, j: (0, 0)),
            pl.BlockSpec((1, 1, D_MODEL), lambda i, j: (mod_row(i, tm), 0, 0)),
            pl.BlockSpec((1, 1, D_MODEL), lambda i, j: (mod_row(i, tm), 0, 0)),
        ] + [w_spec] * len(weights),
        out_specs=pl.BlockSpec((tm, tn), lambda i, j: (i, j)),
        scratch_shapes=[pltpu.VMEM((tm, D_MODEL), BF16)],
        compiler_params=_cparams(("arbitrary", "arbitrary"), 56),
        name=name,
    )(x, g, sc, sh, *weights)


def _res_mm_kernel(x_ref, w_ref, res_ref, gt_ref, o_ref):
    x = x_ref[...]
    for c0 in range(0, o_ref.shape[1], MXU_COLS):
        cs = slice(c0, c0 + MXU_COLS)
        acc = jnp.dot(x, w_ref[:, cs].astype(BF16), preferred_element_type=F32)
        o_ref[:, cs] = res_ref[:, cs] + gt_ref[0, :, cs] * acc


def _res_mm(x, w, layer, res, gt, mod_row, tm, tn, name, weight_stationary=False):
    m, k = x.shape
    n = res.shape[1]
    if weight_stationary:
        ij = lambda a, b: (b, a)
        grid = (n // tn, m // tm)
        w_mode = dict(pipeline_mode=pl.Buffered(1))
    else:
        ij = lambda a, b: (a, b)
        grid = (m // tm, n // tn)
        w_mode = {}
    return pl.pallas_call(
        _res_mm_kernel,
        out_shape=jax.ShapeDtypeStruct((m, n), F32),
        grid=grid,
        in_specs=[
            pl.BlockSpec((tm, k), lambda a, b: (ij(a, b)[0], 0)),
            pl.BlockSpec((None, k, tn), lambda a, b: (layer, 0, ij(a, b)[1]), **w_mode),
            pl.BlockSpec((tm, tn), lambda a, b: ij(a, b)),
            pl.BlockSpec((1, 1, tn), lambda a, b: (mod_row(ij(a, b)[0], tm), 0, ij(a, b)[1])),
        ],
        out_specs=pl.BlockSpec((tm, tn), lambda a, b: ij(a, b)),
        compiler_params=_cparams(("arbitrary", "arbitrary"), 56),
        name=name,
    )(x, w, res, gt)


def _gmlp_kernel(u_ref, v_ref, g_ref, ws_ref, bs_ref, o_ref):
    v = v_ref[...].astype(F32)
    vn = (v * lax.rsqrt(jnp.mean(v * v, axis=-1, keepdims=True) + NORM_EPS) * g_ref[...]).astype(BF16)
    rows = v.shape[0]
    for gi in range(A_GROUPS):
        w = ws_ref[gi].astype(BF16)
        bias = bs_ref[gi]
        cs = slice(gi * LANES, (gi + 1) * LANES)
        for c in range(rows // CHUNK):
            rs = slice(c * CHUNK, (c + 1) * CHUNK)
            s = jnp.dot(w, vn[rs, cs], preferred_element_type=F32) + bias
            o_ref[rs, cs] = (u_ref[rs, cs].astype(F32) * s).astype(o_ref.dtype)


def _gmlp(p, norm_g, ws, bs, layer):
    m = p.shape[0]
    tm = 512
    return pl.pallas_call(
        _gmlp_kernel,
        out_shape=jax.ShapeDtypeStruct((m, A_CH), BF16),
        grid=(m // tm,),
        in_specs=[
            pl.BlockSpec((tm, A_CH), lambda i: (i, OFF_AU // A_CH)),
            pl.BlockSpec((tm, A_CH), lambda i: (i, OFF_AV // A_CH)),
            pl.BlockSpec((1, A_CH), lambda i: (0, 0)),
            pl.BlockSpec((None, A_GROUPS, CHUNK, CHUNK), lambda i: (layer, 0, 0, 0)),
            pl.BlockSpec((None, A_GROUPS, CHUNK, 1), lambda i: (layer, 0, 0, 0)),
        ],
        out_specs=pl.BlockSpec((tm, A_CH), lambda i: (i, 0)),
        compiler_params=_cparams(("arbitrary",), 32),
        name="mixer_a_gmlp",
    )(p, p, norm_g[layer].reshape(1, A_CH), ws, bs.reshape(bs.shape + (1,)))


CONV_ROWS = 256
HALO = BF16_SUBLANES


def _conv_kernel(b_ref, c_ref, h_ref, cp_ref, hp_ref, cn_ref, hn_ref, w_ref, o_ref, *, tiles_per_seq):
    i = pl.program_id(0)
    z = c_ref[...].astype(F32) * h_ref[...].astype(F32)
    n = z.shape[0]
    first = (i % tiles_per_seq) == 0
    last = (i % tiles_per_seq) == tiles_per_seq - 1
    zp = jnp.where(first, 0.0, cp_ref[HALO - 1:HALO, :].astype(F32) * hp_ref[HALO - 1:HALO, :].astype(F32))
    zn = jnp.where(last, 0.0, cn_ref[0:1, :].astype(F32) * hn_ref[0:1, :].astype(F32))
    pos = lax.broadcasted_iota(jnp.int32, z.shape, 0)
    z_prev = jnp.where(pos == 0, zp, pltpu.roll(z, 1, axis=0))
    z_next = jnp.where(pos == n - 1, zn, pltpu.roll(z, n - 1, axis=0))
    w = w_ref[...]
    conv = z_prev * w[0:1] + z * w[1:2] + z_next * w[2:3]
    o_ref[...] = (b_ref[...].astype(F32) * conv).astype(o_ref.dtype)


def _short_conv(p, conv_w8, layer, seq_len):
    m = p.shape[0]
    tr = CONV_ROWS
    per = tr // HALO
    n_halo = m // HALO

    def prev_blk(col):
        return lambda i: (jnp.maximum(i * per - 1, 0), col)

    def next_blk(col):
        return lambda i: (jnp.minimum((i + 1) * per, n_halo - 1), col)

    cc, ch = OFF_CC // C_W, OFF_CH // C_W
    return pl.pallas_call(
        functools.partial(_conv_kernel, tiles_per_seq=seq_len // tr),
        out_shape=jax.ShapeDtypeStruct((m, C_W), BF16),
        grid=(m // tr,),
        in_specs=[
            pl.BlockSpec((tr, C_W), lambda i: (i, OFF_CB // C_W)),
            pl.BlockSpec((tr, C_W), lambda i: (i, cc)),
            pl.BlockSpec((tr, C_W), lambda i: (i, ch)),
            pl.BlockSpec((HALO, C_W), prev_blk(cc)),
            pl.BlockSpec((HALO, C_W), prev_blk(ch)),
            pl.BlockSpec((HALO, C_W), next_blk(cc)),
            pl.BlockSpec((HALO, C_W), next_blk(ch)),
            pl.BlockSpec((None, 8, C_W), lambda i: (layer, 0, 0)),
        ],
        out_specs=pl.BlockSpec((tr, C_W), lambda i: (i, 0)),
        compiler_params=_cparams(("arbitrary",), 32),
        name="mixer_c_conv",
    )(p, p, p, p, p, p, p, conv_w8)


def _rope(x, cos, sin_lo, sin_hi):
    x = x.astype(F32)
    return x * cos + pltpu.roll(x, LANES - 32, axis=1) * sin_lo + pltpu.roll(x, 32, axis=1) * sin_hi


def _softmax_av(s_parts, v_parts, extra_logit=None):
    m = s_parts[0].max(axis=-1, keepdims=True)
    for s in s_parts[1:]:
        m = jnp.maximum(m, s.max(axis=-1, keepdims=True))
    if extra_logit is not None:
        m = jnp.maximum(m, extra_logit)
    denom = jnp.exp(extra_logit - m) if extra_logit is not None else 0.0
    acc = None
    for s, v in zip(s_parts, v_parts):
        e = jnp.exp(s - m)
        denom = denom + e.sum(axis=-1, keepdims=True)
        pv = jnp.dot(e.astype(BF16), v, preferred_element_type=F32)
        acc = pv if acc is None else acc + pv
    return acc / denom


def _qk(q, k):
    return lax.dot_general(q, k, (((1,), (1,)), ((), ())), preferred_element_type=F32)


def _sink_column(sink_ref, h, rows_per_head):
    rid = lax.broadcasted_iota(jnp.int32, (B_GROUP * rows_per_head, 1), 0) // rows_per_head
    col = jnp.zeros((B_GROUP * rows_per_head, 1), F32)
    for g in range(B_GROUP):
        col = jnp.where(rid == g, sink_ref[h * B_GROUP + g], col)
    return col


def _win_attn_kernel(sink_ref, q_ref, k_ref, v_ref, kc_ref, vc_ref, cos_ref, slo_ref, shi_ref, mask_ref, o_ref):
    n = pl.program_id(1)
    blk = B_WINDOW
    band = 3 * blk
    kblk = jnp.clip(n - 1, 0, SEQ // blk - 3)
    q0 = pl.multiple_of(n * blk, blk)
    k0 = pl.multiple_of(kblk * blk, blk)
    scale = HEAD_DIM ** -0.5

    cos_q, slo_q, shi_q = cos_ref[pl.ds(q0, blk), :], slo_ref[pl.ds(q0, blk), :], shi_ref[pl.ds(q0, blk), :]
    cos_k, slo_k, shi_k = cos_ref[pl.ds(k0, band), :], slo_ref[pl.ds(k0, band), :], shi_ref[pl.ds(k0, band), :]
    window_bias = mask_ref[n - kblk]

    for h in range(B_KV_HEADS):
        hs = slice(h * HEAD_DIM, (h + 1) * HEAD_DIM)
        k_loc = _rope(k_ref[pl.ds(k0, band), hs], cos_k, slo_k, shi_k).astype(BF16)
        v_loc = v_ref[pl.ds(k0, band), hs]
        k_ctx = kc_ref[:, hs]
        v_ctx = vc_ref[:, hs]
        qs = []
        for g in range(B_GROUP):
            c0 = (h * B_GROUP + g) * HEAD_DIM
            qs.append(_rope(q_ref[:, c0:c0 + HEAD_DIM], cos_q, slo_q, shi_q).astype(BF16))
        qst = jnp.concatenate(qs, axis=0)
        s_loc = _qk(qst, k_loc) * scale + window_bias
        s_ctx = _qk(qst, k_ctx) * scale
        out = _softmax_av([s_ctx, s_loc], [v_ctx, v_loc], _sink_column(sink_ref, h, blk))
        for g in range(B_GROUP):
            c0 = (h * B_GROUP + g) * HEAD_DIM
            o_ref[:, c0:c0 + HEAD_DIM] = out[g * blk:(g + 1) * blk].astype(o_ref.dtype)


def _window_bias_table():
    i = np.arange(B_WINDOW)[:, None]
    j = np.arange(3 * B_WINDOW)[None, :]
    tabs = []
    for c in range(3):
        rel = j - c * B_WINDOW - i
        one = np.where(np.abs(rel) <= B_WINDOW, 0.0, NEG_INF).astype(np.float32)
        tabs.append(np.tile(one, (B_GROUP, 1)))
    return jnp.asarray(np.stack(tabs))


def _win_attn(p, pc, kc_blk, vc_blk, sink, rope_tabs, batch):
    nb = SEQ // B_WINDOW
    kvw = B_KV_W
    mask_tab = _window_bias_table()
    return pl.pallas_call(
        _win_attn_kernel,
        out_shape=jax.ShapeDtypeStruct((batch * SEQ, B_Q_W), BF16),
        grid=(batch, nb),
        in_specs=[
            pl.BlockSpec(memory_space=pltpu.SMEM),
            pl.BlockSpec((B_WINDOW, B_Q_W), lambda b, n: (b * nb + n, OFF_BQ // B_Q_W)),
            pl.BlockSpec((SEQ, kvw), lambda b, n: (b, OFF_BK // kvw)),
            pl.BlockSpec((SEQ, kvw), lambda b, n: (b, OFF_BV // kvw)),
            pl.BlockSpec((CTX_LEN, kvw), lambda b, n: (b, kc_blk)),
            pl.BlockSpec((CTX_LEN, kvw), lambda b, n: (b, vc_blk)),
            pl.BlockSpec((SEQ, HEAD_DIM), lambda b, n: (0, 0)),
            pl.BlockSpec((SEQ, HEAD_DIM), lambda b, n: (0, 0)),
            pl.BlockSpec((SEQ, HEAD_DIM), lambda b, n: (0, 0)),
            pl.BlockSpec(mask_tab.shape, lambda b, n: (0, 0, 0)),
        ],
        out_specs=pl.BlockSpec((B_WINDOW, B_Q_W), lambda b, n: (b * nb + n, 0)),
        compiler_params=_cparams(("arbitrary", "arbitrary"), 40),
        name="mixer_b_window_attn",
    )(sink, p, p, p, pc, pc, *rope_tabs, mask_tab)


def _ctx_gqa_kernel(sink_ref, q_ref, kc_ref, vc_ref, o_ref):
    scale = HEAD_DIM ** -0.5
    lc = q_ref.shape[0]
    for h in range(B_KV_HEADS):
        hs = slice(h * HEAD_DIM, (h + 1) * HEAD_DIM)
        qst = jnp.concatenate(
            [q_ref[:, (h * B_GROUP + g) * HEAD_DIM:(h * B_GROUP + g + 1) * HEAD_DIM] for g in range(B_GROUP)],
            axis=0)
        out = _softmax_av([_qk(qst, kc_ref[:, hs]) * scale], [vc_ref[:, hs]], _sink_column(sink_ref, h, lc))
        for g in range(B_GROUP):
            c0 = (h * B_GROUP + g) * HEAD_DIM
            o_ref[:, c0:c0 + HEAD_DIM] = out[g * lc:(g + 1) * lc].astype(o_ref.dtype)


def _ctx_gqa(pc, sink, batch):
    kvw = B_KV_W
    return pl.pallas_call(
        _ctx_gqa_kernel,
        out_shape=jax.ShapeDtypeStruct((batch * CTX_LEN, B_Q_W), BF16),
        grid=(batch,),
        in_specs=[
            pl.BlockSpec(memory_space=pltpu.SMEM),
            pl.BlockSpec((CTX_LEN, B_Q_W), lambda b: (b, OFF_BQ // B_Q_W)),
            pl.BlockSpec((CTX_LEN, kvw), lambda b: (b, OFF_BK // kvw)),
            pl.BlockSpec((CTX_LEN, kvw), lambda b: (b, OFF_BV // kvw)),
        ],
        out_specs=pl.BlockSpec((CTX_LEN, B_Q_W), lambda b: (b, 0)),
        compiler_params=_cparams(("arbitrary",), 32),
        name="mixer_b_ctx_attn",
    )(sink, pc, pc, pc)


NA_ROWS_PER_STEP = 4
GRID_H = SEQ // GRID_W
NA_LOC = NA_KH * GRID_W


def _na_kernel(q_ref, k_ref, v_ref, kc_ref, vc_ref, bias_ref, o_ref):
    step = pl.program_id(1)
    scale = HEAD_DIM ** -0.5
    blocks, s_loc, s_ctx = [], [], []
    for rr in range(NA_ROWS_PER_STEP):
        r = step * NA_ROWS_PER_STEP + rr
        rs = jnp.clip(r - NA_KH // 2, 0, GRID_H - NA_KH)
        k0 = pl.multiple_of(rs * GRID_W, GRID_W)
        d = r - rs
        qrows = slice(rr * GRID_W, (rr + 1) * GRID_W)
        for h in range(D_HEADS):
            hs = slice(h * HEAD_DIM, (h + 1) * HEAD_DIM)
            q = q_ref[qrows, hs]
            s_loc.append(_qk(q, k_ref[pl.ds(k0, NA_LOC), hs]) * scale + bias_ref[h, d])
            s_ctx.append(_qk(q, kc_ref[:, hs]) * scale)
            blocks.append((qrows, hs, k0))
    s_loc = jnp.concatenate(s_loc, axis=0)
    s_ctx = jnp.concatenate(s_ctx, axis=0)
    m = jnp.maximum(s_loc.max(axis=-1, keepdims=True), s_ctx.max(axis=-1, keepdims=True))
    e_loc = jnp.exp(s_loc - m)
    e_ctx = jnp.exp(s_ctx - m)
    denom = e_loc.sum(axis=-1, keepdims=True) + e_ctx.sum(axis=-1, keepdims=True)
    e_loc = e_loc.astype(BF16)
    e_ctx = e_ctx.astype(BF16)
    for i, (qrows, hs, k0) in enumerate(blocks):
        rows = slice(i * GRID_W, (i + 1) * GRID_W)
        acc = jnp.dot(e_ctx[rows], vc_ref[:, hs], preferred_element_type=F32)
        acc = acc + jnp.dot(e_loc[rows], v_ref[pl.ds(k0, NA_LOC), hs], preferred_element_type=F32)
        o_ref[qrows, hs] = (acc / denom[rows]).astype(o_ref.dtype)


def _na_bias_table(rpb):
    rpb = rpb.astype(F32)
    rows = []
    for c in range(GRID_W):
        cstart = min(max(c - NA_KW // 2, 0), GRID_W - NA_KW)
        first_dc = cstart - c + NA_KW - 1
        win = rpb[:, :, first_dc:first_dc + NA_KW]
        rows.append(jnp.pad(win, ((0, 0), (0, 0), (cstart, GRID_W - NA_KW - cstart)), constant_values=NEG_INF))
    tab0 = jnp.stack(rows, axis=2)
    slabs = []
    for d in range(NA_KH):
        sl = tab0[:, NA_KH - 1 - d:2 * NA_KH - 1 - d]
        slabs.append(jnp.transpose(sl, (0, 2, 1, 3)).reshape(D_HEADS, GRID_W, NA_LOC))
    return jnp.stack(slabs, axis=1)


def _na_attn(p, pc, kc_blk, vc_blk, bias_tab, batch):
    rows_q = NA_ROWS_PER_STEP * GRID_W
    steps = SEQ // rows_q
    return pl.pallas_call(
        _na_kernel,
        out_shape=jax.ShapeDtypeStruct((batch * SEQ, D_W), BF16),
        grid=(batch, steps),
        in_specs=[
            pl.BlockSpec((rows_q, D_W), lambda b, s: (b * steps + s, OFF_DQ // D_W)),
            pl.BlockSpec((SEQ, D_W), lambda b, s: (b, OFF_DK // D_W)),
            pl.BlockSpec((SEQ, D_W), lambda b, s: (b, OFF_DV // D_W)),
            pl.BlockSpec((CTX_LEN, D_W), lambda b, s: (b, kc_blk)),
            pl.BlockSpec((CTX_LEN, D_W), lambda b, s: (b, vc_blk)),
            pl.BlockSpec((D_HEADS, NA_KH, GRID_W, NA_LOC), lambda b, s: (0, 0, 0, 0)),
        ],
        out_specs=pl.BlockSpec((rows_q, D_W), lambda b, s: (b * steps + s, 0)),
        compiler_params=_cparams(("arbitrary", "arbitrary"), 48),
        name="mixer_d_neighborhood_attn",
    )(p, p, p, pc, pc, bias_tab)


def _ctx_mha_kernel(q_ref, kc_ref, vc_ref, o_ref):
    scale = HEAD_DIM ** -0.5
    for h in range(D_HEADS):
        hs = slice(h * HEAD_DIM, (h + 1) * HEAD_DIM)
        s = _qk(q_ref[:, hs], kc_ref[:, hs]) * scale
        o_ref[:, hs] = _softmax_av([s], [vc_ref[:, hs]]).astype(o_ref.dtype)


def _ctx_mha(pc, batch):
    return pl.pallas_call(
        _ctx_mha_kernel,
        out_shape=jax.ShapeDtypeStruct((batch * CTX_LEN, D_W), BF16),
        grid=(batch,),
        in_specs=[
            pl.BlockSpec((CTX_LEN, D_W), lambda b: (b, OFF_DQ // D_W)),
            pl.BlockSpec((CTX_LEN, D_W), lambda b: (b, OFF_DK // D_W)),
            pl.BlockSpec((CTX_LEN, D_W), lambda b: (b, OFF_DV // D_W)),
        ],
        out_specs=pl.BlockSpec((CTX_LEN, D_W), lambda b: (b, 0)),
        compiler_params=_cparams(("arbitrary",), 32),
        name="mixer_d_ctx_attn",
    )(pc, pc, pc)


BR_W = (A_CH, B_Q_W, C_W, D_W)
BR_OFF = (0, A_CH, A_CH + B_Q_W, A_CH + B_Q_W + C_W)


def _merge_kernel(ya_ref, yb_ref, yc_ref, yd_ref, w_ref, g0_ref, g1_ref, g2_ref, g3_ref, o_ref):
    z = None
    for y_ref, g_ref, off, width in zip((ya_ref, yb_ref, yc_ref, yd_ref),
                                        (g0_ref, g1_ref, g2_ref, g3_ref), BR_OFF, BR_W):
        t = jnp.dot(y_ref[...], w_ref[off:off + width, :].astype(BF16), preferred_element_type=F32)
        t = jax.nn.sigmoid(g_ref[...].astype(F32)) * t
        z = t if z is None else z + t
    o_ref[...] = z.astype(o_ref.dtype)


def _merge(p, ys, w_branch, layer):
    m = p.shape[0]
    tm, tn = TM_MERGE, TN_MERGE
    gate_specs = [pl.BlockSpec((tm, tn), functools.partial(
        lambda i, j, b: (i, (OFF_G + b * D_MODEL) // tn + j), b=b)) for b in range(4)]
    return pl.pallas_call(
        _merge_kernel,
        out_shape=jax.ShapeDtypeStruct((m, D_MODEL), BF16),
        grid=(m // tm, D_MODEL // tn),
        in_specs=[pl.BlockSpec((tm, w), lambda i, j: (i, 0)) for w in BR_W]
        + [pl.BlockSpec((None, MIX_W, tn), lambda i, j: (layer, 0, j))] + gate_specs,
        out_specs=pl.BlockSpec((tm, tn), lambda i, j: (i, j)),
        compiler_params=_cparams(("arbitrary", "arbitrary"), 48),
        name="branch_merge",
    )(*ys, w_branch, p, p, p, p)


ROUTE_LANES = LANES
LANE_IDX = N_EXPERTS
LANE_WT = N_EXPERTS + TOP_K


def _router_kernel(x_ref, g_ref, sc_ref, sh_ref, wr_ref, xn_ref, route_ref):
    xn = _norm_mod(x_ref[...], g_ref[...], sc_ref[0], sh_ref[0])
    xn_ref[...] = xn.astype(BF16).reshape(xn_ref.shape)
    x_hi = xn.astype(BF16)
    x_lo = (xn - x_hi.astype(F32)).astype(BF16)
    wr = wr_ref[...]
    w_hi = wr.astype(BF16)
    w_lo = (wr - w_hi.astype(F32)).astype(BF16)
    logits = (jnp.dot(x_hi, w_hi, preferred_element_type=F32)
              + (jnp.dot(x_hi, w_lo, preferred_element_type=F32) + jnp.dot(x_lo, w_hi, preferred_element_type=F32)))
    lane = lax.broadcasted_iota(jnp.int32, logits.shape, 1).astype(F32)
    logits = jnp.where(lane < N_EXPERTS, logits, -jnp.inf)
    m1 = logits.max(axis=-1, keepdims=True)
    i1 = jnp.where(logits == m1, lane, float(ROUTE_LANES)).min(axis=-1, keepdims=True)
    rest = jnp.where(lane == i1, -jnp.inf, logits)
    m2 = rest.max(axis=-1, keepdims=True)
    i2 = jnp.where(rest == m2, lane, float(ROUTE_LANES)).min(axis=-1, keepdims=True)
    e2 = jnp.exp(m2 - m1)
    w1 = 1.0 / (1.0 + e2)
    w2 = e2 / (1.0 + e2)
    out = jnp.where(lane == LANE_IDX, i1, 0.0)
    out = jnp.where(lane == LANE_IDX + 1, i2, out)
    out = jnp.where(lane == LANE_WT, w1, out)
    out = jnp.where(lane == LANE_WT + 1, w2, out)
    route_ref[...] = out


def _router(h, g, sc, sh, mod_row, w_router_pad):
    t = h.shape[0]
    tm = 512
    return pl.pallas_call(
        _router_kernel,
        out_shape=(jax.ShapeDtypeStruct((t, D_MODEL // LANES, LANES), BF16),
                   jax.ShapeDtypeStruct((t, ROUTE_LANES), F32)),
        grid=(t // tm,),
        in_specs=[
            pl.BlockSpec((tm, D_MODEL), lambda i: (i, 0)),
            pl.BlockSpec((1, D_MODEL), lambda i: (0, 0)),
            pl.BlockSpec((1, 1, D_MODEL), lambda i: (mod_row(i, tm), 0, 0)),
            pl.BlockSpec((1, 1, D_MODEL), lambda i: (mod_row(i, tm), 0, 0)),
            pl.BlockSpec((D_MODEL, ROUTE_LANES), lambda i: (0, 0)),
        ],
        out_specs=(pl.BlockSpec((tm, D_MODEL // LANES, LANES), lambda i: (i, 0, 0)),
                   pl.BlockSpec((tm, ROUTE_LANES), lambda i: (i, 0))),
        compiler_params=_cparams(("arbitrary",), 40),
        name="moe_router",
    )(h, g, sc, sh, w_router_pad)


def _gather_rows_kernel(idx_ref, src_ref, o_ref, *scratch, n_sources, flat):
    sem = scratch[-1]
    dst_ref = scratch[0] if flat else o_ref
    rows = dst_ref.shape[-3]

    def row_copy(s, r, src_row):
        return pltpu.make_async_copy(src_ref.at[src_row], dst_ref.at[s, r], sem)

    def issue(i, carry):
        for u in range(GATHER_UNROLL):
            r = i * GATHER_UNROLL + u
            for s in range(n_sources):
                row_copy(s, r, idx_ref[0, s, r]).start(priority=(u * n_sources + s) % 2)
        return carry

    lax.fori_loop(0, rows // GATHER_UNROLL, issue, 0)

    def drain(i, carry):
        for u in range(GATHER_UNROLL):
            for s in range(n_sources):
                row_copy(s, i * GATHER_UNROLL + u, 0).wait()
        return carry

    lax.fori_loop(0, rows // GATHER_UNROLL, drain, 0)
    if flat:
        o_ref[...] = dst_ref[0].reshape(o_ref.shape)


def _gather_rows(src3, idx, n_sources, flat=False):
    _, s_dim, lanes = src3.shape
    r = idx.shape[1]
    tg = TG_ROWS
    idx_blocks = idx.reshape(n_sources, r // tg, tg).transpose(1, 0, 2)
    if flat:
        assert n_sources == 1
        out_shape = jax.ShapeDtypeStruct((r, s_dim * lanes), src3.dtype)
        out_spec = pl.BlockSpec((tg, s_dim * lanes), lambda i: (i, 0))
        scratch = [pltpu.VMEM((1, tg, s_dim, lanes), src3.dtype), pltpu.SemaphoreType.DMA]
    else:
        out_shape = jax.ShapeDtypeStruct((n_sources, r, s_dim, lanes), src3.dtype)
        out_spec = pl.BlockSpec((n_sources, tg, s_dim, lanes), lambda i: (0, i, 0, 0))
        scratch = [pltpu.SemaphoreType.DMA]
    return pl.pallas_call(
        functools.partial(_gather_rows_kernel, n_sources=n_sources, flat=flat),
        out_shape=out_shape,
        grid=(r // tg,),
        in_specs=[
            pl.BlockSpec((1, n_sources, tg), lambda i: (i, 0, 0), memory_space=pltpu.SMEM),
            pl.BlockSpec(memory_space=pl.ANY),
        ],
        out_specs=out_spec,
        scratch_shapes=scratch,
        compiler_params=_cparams(("arbitrary",), 32),
        name="row_gather",
    )(idx_blocks, src3)


def _moe_up_kernel(te_ref, nu_ref, x_ref, w1_ref, w3_ref, w2_ref, o_ref, w2b_ref):
    used = pl.program_id(1) < nu_ref[0]
    w2b_ref[...] = w2_ref[...].astype(BF16)

    @pl.when(used)
    def _():
        _swiglu_store(x_ref[...], w1_ref, w3_ref, o_ref)

    @pl.when(jnp.logical_not(used))
    def _():
        o_ref[...] = jnp.zeros_like(o_ref)


def _moe_up(xs, w1, w3, w2, layer, tile_expert, n_used):
    r = xs.shape[0]
    tm, tn = TM_MOE, TN_MOE_UP
    n_t = r // tm
    w2_rows = N_EXPERTS * FFN_DIM
    steps = (FFN_DIM // tn) * n_t
    cast_rows = next(d for d in range(BF16_SUBLANES, w2_rows + 1, BF16_SUBLANES)
                     if w2_rows % d == 0 and w2_rows // d <= steps)
    cast_blocks = w2_rows // cast_rows
    w2_flat = w2.reshape(-1, D_MODEL)

    def cast_blk(j, t, te, nu):
        return (layer * cast_blocks + jnp.minimum(j * n_t + t, cast_blocks - 1), 0)

    def cast_out_blk(j, t, te, nu):
        return (jnp.minimum(j * n_t + t, cast_blocks - 1), 0)

    w_spec = pl.BlockSpec((None, None, D_MODEL, tn), lambda j, t, te, nu: (layer, te[t], 0, j))
    grid_spec = pltpu.PrefetchScalarGridSpec(
        num_scalar_prefetch=2,
        grid=(FFN_DIM // tn, n_t),
        in_specs=[pl.BlockSpec((tm, D_MODEL), lambda j, t, te, nu: (jnp.minimum(t, nu[0] - 1), 0)),
                  w_spec, w_spec,
                  pl.BlockSpec((cast_rows, D_MODEL), cast_blk)],
        out_specs=(pl.BlockSpec((tm, tn), lambda j, t, te, nu: (t, j)),
                   pl.BlockSpec((cast_rows, D_MODEL), cast_out_blk)),
    )
    hm, w2b = pl.pallas_call(
        _moe_up_kernel,
        out_shape=(jax.ShapeDtypeStruct((r, FFN_DIM), BF16),
                   jax.ShapeDtypeStruct((w2_rows, D_MODEL), BF16)),
        grid_spec=grid_spec,
        compiler_params=_cparams(("arbitrary", "arbitrary"), 56),
        name="moe_up",
    )(tile_expert, n_used, xs, w1, w3, w2_flat)
    return hm, w2b.reshape(1, N_EXPERTS, FFN_DIM, D_MODEL)


def _moe_down_kernel(te_ref, nu_ref, x_ref, w_ref, o_ref):
    used = pl.program_id(1) < nu_ref[0]

    @pl.when(used)
    def _():
        x = x_ref[...]
        for c0 in range(0, o_ref.shape[1], MXU_COLS):
            cs = slice(c0, c0 + MXU_COLS)
            o_ref[:, cs] = jnp.dot(x, w_ref[:, cs], preferred_element_type=F32).astype(o_ref.dtype)

    @pl.when(jnp.logical_not(used))
    def _():
        o_ref[...] = jnp.zeros_like(o_ref)


def _moe_down(hmid, w2, layer, tile_expert, n_used):
    r = hmid.shape[0]
    tm, tn = TM_MOE, TN_MOE_DOWN
    grid_spec = pltpu.PrefetchScalarGridSpec(
        num_scalar_prefetch=2,
        grid=(D_MODEL // tn, r // tm),
        in_specs=[
            pl.BlockSpec((tm, FFN_DIM), lambda j, t, te, nu: (jnp.minimum(t, nu[0] - 1), 0)),
            pl.BlockSpec((None, None, FFN_DIM, tn), lambda j, t, te, nu: (layer, te[t], 0, j)),
        ],
        out_specs=pl.BlockSpec((tm, tn), lambda j, t, te, nu: (t, j)),
    )
    return pl.pallas_call(
        _moe_down_kernel,
        out_shape=jax.ShapeDtypeStruct((r, D_MODEL), BF16),
        grid_spec=grid_spec,
        compiler_params=_cparams(("arbitrary", "arbitrary"), 60),
        name="moe_down",
    )(tile_expert, n_used, hmid, w2)


def _route_plan(route, tm):
    t = route.shape[0]
    idx = route[:, LANE_IDX:LANE_IDX + TOP_K].astype(jnp.int32)
    e_flat = idx.T.reshape(-1)
    onehot = (e_flat[:, None] == jnp.arange(N_EXPERTS)[None, :]).astype(jnp.int32)
    counts = onehot.sum(axis=0)
    rank = (onehot * (jnp.cumsum(onehot, axis=0) - onehot)).sum(axis=1)
    tiles_e = (counts + tm - 1) // tm
    tile_end = jnp.cumsum(tiles_e)
    tile_start = tile_end - tiles_e
    pos = (onehot * tile_start[None, :]).sum(axis=1) * tm + rank
    n_tiles = TOP_K * t // tm + N_EXPERTS
    n_used = tile_end[-1]
    tile_ids = jnp.minimum(jnp.arange(n_tiles), n_used - 1)
    tile_expert = jnp.minimum((tile_ids[:, None] >= tile_end[None, :]).sum(axis=1), N_EXPERTS - 1)
    row_token = (jnp.arange(n_tiles * tm, dtype=jnp.int32) % t).at[pos].set(
        jnp.tile(jnp.arange(t, dtype=jnp.int32), TOP_K))
    return (pos.reshape(TOP_K, t).astype(jnp.int32), row_token,
            tile_expert.astype(jnp.int32), n_used.reshape(1).astype(jnp.int32))


FINAL_ROWS = 512


def _final_kernel(idx_ref, idx_next_ref, y_ref, h_ref, route_ref, gt_ref, g_ref, o_ref, buf, sems):
    i = pl.program_id(0)
    half = FINAL_ROWS // 2

    def row_copy(slot, k, r, src_row):
        return pltpu.make_async_copy(y_ref.at[src_row], buf.at[slot, k, r], sems.at[slot])

    def start_half(ids_ref, slot, first_row):
        def body(g, carry):
            for u in range(GATHER_UNROLL):
                r = g * GATHER_UNROLL + u
                for k in range(TOP_K):
                    row_copy(slot, k, r, ids_ref[0, k, first_row + r]).start(priority=(u * TOP_K + k) % 2)
            return carry
        lax.fori_loop(0, half // GATHER_UNROLL, body, 0)

    def wait_half(slot):
        def body(g, carry):
            for u in range(GATHER_UNROLL):
                for k in range(TOP_K):
                    row_copy(slot, k, g * GATHER_UNROLL + u, 0).wait()
            return carry
        lax.fori_loop(0, half // GATHER_UNROLL, body, 0)

    def combine_half(slot, first_row):
        rows = slice(first_row, first_row + half)
        w1 = route_ref[rows, LANE_WT:LANE_WT + 1]
        w2 = route_ref[rows, LANE_WT + 1:LANE_WT + 2]
        y0 = buf[slot, 0].reshape(half, D_MODEL).astype(F32)
        y1 = buf[slot, 1].reshape(half, D_MODEL).astype(F32)
        h = h_ref[rows, :] + gt_ref[0] * (w1 * y0 + w2 * y1)
        o_ref[rows, :] = h * lax.rsqrt(jnp.mean(h * h, axis=-1, keepdims=True) + NORM_EPS) * g_ref[...]

    @pl.when(i == 0)
    def _():
        start_half(idx_ref, 0, 0)

    start_half(idx_ref, 1, half)
    wait_half(0)
    combine_half(0, 0)

    @pl.when(i + 1 < pl.num_programs(0))
    def _():
        start_half(idx_next_ref, 0, 0)

    wait_half(1)
    combine_half(1, half)


def _final(h, ye3, pos, route, gt, mod_row, g_final):
    t = h.shape[0]
    tm = FINAL_ROWS
    n = t // tm
    _, s_dim, lanes = ye3.shape
    idx_blocks = pos.reshape(TOP_K, n, tm).transpose(1, 0, 2)
    return pl.pallas_call(
        _final_kernel,
        out_shape=jax.ShapeDtypeStruct((t, D_MODEL), F32),
        grid=(n,),
        in_specs=[
            pl.BlockSpec((1, TOP_K, tm), lambda i: (i, 0, 0), memory_space=pltpu.SMEM),
            pl.BlockSpec((1, TOP_K, tm), lambda i: (jnp.minimum(i + 1, n - 1), 0, 0), memory_space=pltpu.SMEM),
            pl.BlockSpec(memory_space=pl.ANY),
            pl.BlockSpec((tm, D_MODEL), lambda i: (i, 0)),
            pl.BlockSpec((tm, ROUTE_LANES), lambda i: (i, 0)),
            pl.BlockSpec((1, 1, D_MODEL), lambda i: (mod_row(i, tm), 0, 0)),
            pl.BlockSpec((1, D_MODEL), lambda i: (0, 0)),
        ],
        out_specs=pl.BlockSpec((tm, D_MODEL), lambda i: (i, 0)),
        scratch_shapes=[pltpu.VMEM((2, TOP_K, tm // 2, s_dim, lanes), ye3.dtype),
                        pltpu.SemaphoreType.DMA((2,))],
        compiler_params=_cparams(("arbitrary",), 48),
        name="final_combine_norm",
    )(idx_blocks, idx_blocks, ye3, h, route, gt, g_final)


def _rope_tables():
    pos = jnp.arange(SEQ)
    f = HEAD_DIM // 4
    inv = ROPE_BASE ** (-jnp.arange(f, dtype=F32) / f)
    ang_row = (pos // GRID_W).astype(F32)[:, None] * inv[None, :]
    ang_col = (pos % GRID_W).astype(F32)[:, None] * inv[None, :]
    ang = jnp.concatenate([ang_row, ang_row, ang_col, ang_col], axis=1)
    cos, sin = jnp.cos(ang), jnp.sin(ang)
    first = jnp.asarray(((np.arange(HEAD_DIM) // f) % 2 == 0)[None, :])
    return cos, jnp.where(first, -sin, 0.0), jnp.where(first, 0.0, sin)


def kernel(x, c, ctx, c_ctx, w_ada, b_ada, g_mix, g_ffn, w_in, a_norm_g, a_ws, a_bs, b_sink, c_conv,
           d_rpb, w_branch, w_out, ffn_w1, ffn_w3, ffn_w2, w_router, moe_w1, moe_w3, moe_w2, g_final):
    batch, seq, _ = x.shape
    depth = w_in.shape[0]
    assert seq == SEQ and ctx.shape[1] == CTX_LEN and batch + 1 <= MOD_ROWS
    assert depth % 2 == 0, "the fused residual + final-norm epilogue lives in the MoE (odd, last) layer"
    t = batch * seq
    tc = batch * CTX_LEN

    cond = jnp.zeros((MOD_ROWS, D_MODEL), F32).at[:batch].set(c).at[batch].set(c_ctx)
    mods = _ada(cond, w_ada, b_ada)
    rope_tabs = _rope_tables()
    conv_w8 = jnp.zeros((depth, 8, C_W), F32).at[:, :3].set(c_conv)

    w_in, w_branch, w_out = w_in.astype(BF16), w_branch.astype(BF16), w_out.astype(BF16)
    ffn_w1, ffn_w3, ffn_w2 = ffn_w1.astype(BF16), ffn_w3.astype(BF16), ffn_w2.astype(BF16)

    def lat_row(i, tm):
        return (i * tm) // SEQ

    def ctx_row(i, tm):
        return batch

    h = x.reshape(t, D_MODEL)
    hc = ctx.reshape(tc, D_MODEL)
    out = None
    for layer in range(depth):
        last = layer == depth - 1
        sh1, sc1, gt1, sh2, sc2, gt2 = [m.reshape(MOD_ROWS, 1, D_MODEL)
                                        for m in jnp.split(mods[layer], 6, axis=-1)]
        g_mix_l = g_mix[layer].reshape(1, D_MODEL)
        g_ffn_l = g_ffn[layer].reshape(1, D_MODEL)
        bias_tab = _na_bias_table(d_rpb[layer])

        if last:
            pc = _norm_mm(hc, g_mix_l, sc1, sh1, ctx_row, (w_in,), layer,
                          lambda j: jnp.where(j == 0, OFF_BK // TN_KV, OFF_DK // TN_KV - 1 + j),
                          3 * TN_KV, TM_PROJ, TN_KV, "ctx_kv_proj")
            kcb, vcb, kcd, vcd = 0, 1, 1, 2
        else:
            pc = _norm_mm(hc, g_mix_l, sc1, sh1, ctx_row, (w_in,), layer, lambda j: j,
                          IN_W, TM_PROJ, TN_IN, "ctx_in_proj")
            kcb, vcb, kcd, vcd = OFF_BK // B_KV_W, OFF_BV // B_KV_W, OFF_DK // D_W, OFF_DV // D_W

        px = _norm_mm(h, g_mix_l, sc1, sh1, lat_row, (w_in,), layer, lambda j: j,
                      IN_W, TM_PROJ, TN_IN, "in_proj")
        ys = (
            _gmlp(px, a_norm_g, a_ws, a_bs, layer),
            _win_attn(px, pc, kcb, vcb, b_sink[layer], rope_tabs, batch),
            _short_conv(px, conv_w8, layer, SEQ),
            _na_attn(px, pc, kcd, vcd, bias_tab, batch),
        )
        z = _merge(px, ys, w_branch, layer)
        h = _res_mm(z, w_out, layer, h, gt1, lat_row, TM_OUT, TN_OUT, "out_proj")

        if not last:
            ysc = (
                _gmlp(pc, a_norm_g, a_ws, a_bs, layer),
                _ctx_gqa(pc, b_sink[layer], batch),
                _short_conv(pc, conv_w8, layer, CTX_LEN),
                _ctx_mha(pc, batch),
            )
            zc = _merge(pc, ysc, w_branch, layer)
            hc = _res_mm(zc, w_out, layer, hc, gt1, ctx_row, TM_OUT, TN_OUT, "ctx_out_proj")

        j = layer // 2
        if layer % 2 == 0:
            hm = _norm_mm(h, g_ffn_l, sc2, sh2, lat_row, (ffn_w1, ffn_w3), j, lambda n: n,
                          FFN_DIM, TM_PROJ, TN_FFN, "ffn_up")
            h = _res_mm(hm, ffn_w2, j, h, gt2, lat_row, TM_DOWN, TN_DOWN, "ffn_down", weight_stationary=True)
            if not last:
                hmc = _norm_mm(hc, g_ffn_l, sc2, sh2, ctx_row, (ffn_w1, ffn_w3), j, lambda n: n,
                               FFN_DIM, TM_PROJ, TN_FFN, "ctx_ffn_up")
                hc = _res_mm(hmc, ffn_w2, j, hc, gt2, ctx_row, TM_DOWN, TN_DOWN, "ctx_ffn_down",
                             weight_stationary=True)
        else:
            assert last, "the context MoE path would only be needed for an odd layer that is not the last"
            wr_pad = jnp.zeros((D_MODEL, ROUTE_LANES), F32).at[:, :N_EXPERTS].set(w_router[j])
            xn, route = _router(h, g_ffn_l, sc2, sh2, lat_row, wr_pad)
            pos, row_token, tile_expert, n_used = _route_plan(route, TM_MOE)
            xs = _gather_rows(xn, row_token[None, :], 1, flat=True)
            hm, w2b = _moe_up(xs, moe_w1, moe_w3, moe_w2, j, tile_expert, n_used)
            ye = _moe_down(hm, w2b, 0, tile_expert, n_used)
            out = _final(h, ye.reshape(-1, D_MODEL // LANES, LANES), pos, route, gt2, lat_row,
                         g_final.reshape(1, D_MODEL))

    return out.reshape(batch, seq, D_MODEL)
```

```python
import functools

import jax
import jax.numpy as jnp
import numpy as np
from jax import lax
from jax.experimental import pallas as pl
from jax.experimental.pallas import tpu as pltpu

F32 = jnp.float32
BF16 = jnp.bfloat16

D_MODEL = 2048
SEQ = 2048
CTX_LEN = 256
GRID_W = 64
HEAD_DIM = 128
ROPE_BASE = 10000.0
NORM_EPS = 1e-6
NEG_INF = -1e30

CHUNK = 128
A_GROUPS = 4
A_CH = 512
B_HEADS = 8
B_KV_HEADS = 2
B_GROUP = B_HEADS // B_KV_HEADS
B_WINDOW = 128
B_Q_W = B_HEADS * HEAD_DIM
B_KV_W = B_KV_HEADS * HEAD_DIM
C_W = 512
D_HEADS = 4
D_W = D_HEADS * HEAD_DIM
NA_KH = 8
NA_KW = 16

OFF_AU = 0
OFF_AV = OFF_AU + A_CH
OFF_BQ = OFF_AV + A_CH
OFF_BK = OFF_BQ + B_Q_W
OFF_BV = OFF_BK + B_KV_W
OFF_CB = OFF_BV + B_KV_W
OFF_CC = OFF_CB + C_W
OFF_CH = OFF_CC + C_W
OFF_DQ = OFF_CH + C_W
OFF_DK = OFF_DQ + D_W
OFF_DV = OFF_DK + D_W
OFF_G = OFF_DV + D_W
IN_W = OFF_G + 4 * D_MODEL
MIX_W = A_CH + B_Q_W + C_W + D_W

FFN_DIM = 7168
N_EXPERTS = 8
TOP_K = 2

LANES = 128
BF16_SUBLANES = 16
MIB = 2**20

TM_PROJ = 1024
TN_KV = 512
TN_IN = 1536
TN_FFN = 1024
TM_OUT, TN_OUT = 1024, 1024
TM_DOWN, TN_DOWN = 512, 1024
TM_MERGE, TN_MERGE = 1024, 512
TM_MOE = 512
TN_MOE_UP = 1024
TN_MOE_DOWN = 1024
TG_ROWS = 512
GATHER_UNROLL = 8
MOD_ROWS = 16


def _cparams(sem, vmem_mib):
    return pltpu.CompilerParams(dimension_semantics=sem, vmem_limit_bytes=vmem_mib * MIB)


def _silu(a):
    return a * jax.nn.sigmoid(a)


def _ada_kernel(c_ref, w_ref, b_ref, o_ref):
    c = c_ref[...]
    s = _silu(c).astype(BF16)
    o_ref[...] = jnp.dot(s, w_ref[...].astype(BF16), preferred_element_type=F32) + b_ref[...]


def _ada(cond, w_ada, b_ada):
    depth, _, n = w_ada.shape
    tn = 1024
    return pl.pallas_call(
        _ada_kernel,
        out_shape=jax.ShapeDtypeStruct((depth, MOD_ROWS, n), F32),
        grid=(depth, n // tn),
        in_specs=[
            pl.BlockSpec((MOD_ROWS, D_MODEL), lambda l, j: (0, 0)),
            pl.BlockSpec((None, D_MODEL, tn), lambda l, j: (l, 0, j)),
            pl.BlockSpec((None, 1, tn), lambda l, j: (l, 0, j)),
        ],
        out_specs=pl.BlockSpec((None, MOD_ROWS, tn), lambda l, j: (l, 0, j)),
        compiler_params=_cparams(("arbitrary", "arbitrary"), 40),
        name="ada_modulation",
    )(cond, w_ada, b_ada.reshape(depth, 1, n))


def _norm_mod(x, g, sc, sh):
    y = x * lax.rsqrt(jnp.mean(x * x, axis=-1, keepdims=True) + NORM_EPS) * g
    return y * (1.0 + sc) + sh


MXU_COLS = 256
NORM_CHUNK = 256


def _swiglu_store(x, w1_ref, w3_ref, o_ref):
    for c0 in range(0, o_ref.shape[1], MXU_COLS):
        cs = slice(c0, c0 + MXU_COLS)
        a = jnp.dot(x, w1_ref[:, cs].astype(BF16), preferred_element_type=F32)
        b = jnp.dot(x, w3_ref[:, cs].astype(BF16), preferred_element_type=F32)
        o_ref[:, cs] = (_silu(a) * b).astype(o_ref.dtype)


def _cast_block_rows(rows, steps):
    return next(d for d in range(BF16_SUBLANES, rows + 1, BF16_SUBLANES) if rows % d == 0 and rows // d <= steps)


def _norm_mm_kernel(x_ref, g_ref, sc_ref, sh_ref, *rest, swiglu, n_cast):
    n_w = 2 if swiglu else 1
    w_refs, cast_in = rest[:n_w], rest[n_w:n_w + n_cast]
    o_ref, cast_out, xn_ref = rest[n_w + n_cast], rest[n_w + n_cast + 1:n_w + 2 * n_cast + 1], rest[-1]

    for src, dst in zip(cast_in, cast_out):
        dst[...] = src[...].astype(BF16)

    @pl.when(pl.program_id(1) == 0)
    def _():
        def chunk(c, carry):
            rows = pl.ds(pl.multiple_of(c * NORM_CHUNK, NORM_CHUNK), NORM_CHUNK)
            xn_ref[rows, :] = _norm_mod(x_ref[rows, :], g_ref[...], sc_ref[0], sh_ref[0]).astype(BF16)
            return carry
        lax.fori_loop(0, x_ref.shape[0] // NORM_CHUNK, chunk, 0)

    xn = xn_ref[...]
    if swiglu:
        _swiglu_store(xn, w_refs[0], w_refs[1], o_ref)
    else:
        o_ref[...] = jnp.dot(xn, w_refs[0][...].astype(BF16), preferred_element_type=F32).astype(o_ref.dtype)


def _norm_mm(x, g, sc, sh, mod_row, weights, layer, col_block, n_out, tm, tn, name, cast=()):
    m = x.shape[0]
    swiglu = len(weights) == 2
    n_j = n_out // tn
    steps = (m // tm) * n_j
    w_spec = pl.BlockSpec((None, D_MODEL, tn), lambda i, j: (layer, 0, col_block(j)))
    cast_in_specs, cast_out_specs, cast_shapes = [], [], []
    for a, row_start, rows in cast:
        d = _cast_block_rows(rows, steps)
        assert row_start % d == 0
        blk = functools.partial(lambda i, j, nb, b0: (b0 + jnp.minimum(i * n_j + j, nb - 1), 0), nb=rows // d)
        cast_in_specs.append(pl.BlockSpec((d, a.shape[1]), functools.partial(blk, b0=row_start // d)))
        cast_out_specs.append(pl.BlockSpec((d, a.shape[1]), functools.partial(blk, b0=0)))
        cast_shapes.append(jax.ShapeDtypeStruct((rows, a.shape[1]), BF16))
    outs = pl.pallas_call(
        functools.partial(_norm_mm_kernel, swiglu=swiglu, n_cast=len(cast)),
        out_shape=[jax.ShapeDtypeStruct((m, n_out), BF16)] + cast_shapes,
        grid=(m // tm, n_j),
        in_specs=[
            pl.BlockSpec((tm, D_MODEL), lambda i, j: (i, 0)),
            pl.BlockSpec((1, D_MODEL), lambda i, j: (0, 0)),
            pl.BlockSpec((1, 1, D_MODEL), lambda i, j: (mod_row(i, tm), 0, 0)),
            pl.BlockSpec((1, 1, D_MODEL), lambda i, j: (mod_row(i, tm), 0, 0)),
        ] + [w_spec] * len(weights) + cast_in_specs,
        out_specs=[pl.BlockSpec((tm, tn), lambda i, j: (i, j))] + cast_out_specs,
        scratch_shapes=[pltpu.VMEM((tm, D_MODEL), BF16)],
        compiler_params=_cparams(("arbitrary", "arbitrary"), 56),
        name=name,
    )(x, g, sc, sh, *weights, *[a for a, _, _ in cast])
    return (outs[0], list(outs[1:])) if cast else outs[0]


def _res_mm_kernel(x_ref, w_ref, res_ref, gt_ref, o_ref):
    x = x_ref[...]
    for c0 in range(0, o_ref.shape[1], MXU_COLS):
        cs = slice(c0, c0 + MXU_COLS)
        acc = jnp.dot(x, w_ref[:, cs].astype(BF16), preferred_element_type=F32)
        o_ref[:, cs] = res_ref[:, cs] + gt_ref[0, :, cs] * acc


def _res_mm(x, w, layer, res, gt, mod_row, tm, tn, name, weight_stationary=False):
    m, k = x.shape
    n = res.shape[1]
    if weight_stationary:
        ij = lambda a, b: (b, a)
        grid = (n // tn, m // tm)
        w_mode = dict(pipeline_mode=pl.Buffered(1))
    else:
        ij = lambda a, b: (a, b)
        grid = (m // tm, n // tn)
        w_mode = {}
    return pl.pallas_call(
        _res_mm_kernel,
        out_shape=jax.ShapeDtypeStruct((m, n), F32),
        grid=grid,
        in_specs=[
            pl.BlockSpec((tm, k), lambda a, b: (ij(a, b)[0], 0)),
            pl.BlockSpec((None, k, tn), lambda a, b: (layer, 0, ij(a, b)[1]), **w_mode),
            pl.BlockSpec((tm, tn), lambda a, b: ij(a, b)),
            pl.BlockSpec((1, 1, tn), lambda a, b: (mod_row(ij(a, b)[0], tm), 0, ij(a, b)[1])),
        ],
        out_specs=pl.BlockSpec((tm, tn), lambda a, b: ij(a, b)),
        compiler_params=_cparams(("arbitrary", "arbitrary"), 56),
        name=name,
    )(x, w, res, gt)


def _gmlp_kernel(u_ref, v_ref, g_ref, ws_ref, bs_ref, o_ref):
    v = v_ref[...].astype(F32)
    vn = (v * lax.rsqrt(jnp.mean(v * v, axis=-1, keepdims=True) + NORM_EPS) * g_ref[...]).astype(BF16)
    rows = v.shape[0]
    for gi in range(A_GROUPS):
        w = ws_ref[gi].astype(BF16)
        bias = bs_ref[gi]
        cs = slice(gi * LANES, (gi + 1) * LANES)
        for c in range(rows // CHUNK):
            rs = slice(c * CHUNK, (c + 1) * CHUNK)
            s = jnp.dot(w, vn[rs, cs], preferred_element_type=F32) + bias
            o_ref[rs, cs] = (u_ref[rs, cs].astype(F32) * s).astype(o_ref.dtype)


def _gmlp(p, norm_g, ws, bs, layer):
    m = p.shape[0]
    tm = 512
    return pl.pallas_call(
        _gmlp_kernel,
        out_shape=jax.ShapeDtypeStruct((m, A_CH), BF16),
        grid=(m // tm,),
        in_specs=[
            pl.BlockSpec((tm, A_CH), lambda i: (i, OFF_AU // A_CH)),
            pl.BlockSpec((tm, A_CH), lambda i: (i, OFF_AV // A_CH)),
            pl.BlockSpec((1, A_CH), lambda i: (0, 0)),
            pl.BlockSpec((None, A_GROUPS, CHUNK, CHUNK), lambda i: (layer, 0, 0, 0)),
            pl.BlockSpec((None, A_GROUPS, CHUNK, 1), lambda i: (layer, 0, 0, 0)),
        ],
        out_specs=pl.BlockSpec((tm, A_CH), lambda i: (i, 0)),
        compiler_params=_cparams(("arbitrary",), 32),
        name="mixer_a_gmlp",
    )(p, p, norm_g[layer].reshape(1, A_CH), ws, bs.reshape(bs.shape + (1,)))


CONV_ROWS = 256
HALO = BF16_SUBLANES


def _conv_kernel(b_ref, c_ref, h_ref, cp_ref, hp_ref, cn_ref, hn_ref, w_ref, o_ref, *, tiles_per_seq):
    i = pl.program_id(0)
    z = c_ref[...].astype(F32) * h_ref[...].astype(F32)
    n = z.shape[0]
    first = (i % tiles_per_seq) == 0
    last = (i % tiles_per_seq) == tiles_per_seq - 1
    zp = jnp.where(first, 0.0, cp_ref[HALO - 1:HALO, :].astype(F32) * hp_ref[HALO - 1:HALO, :].astype(F32))
    zn = jnp.where(last, 0.0, cn_ref[0:1, :].astype(F32) * hn_ref[0:1, :].astype(F32))
    pos = lax.broadcasted_iota(jnp.int32, z.shape, 0)
    z_prev = jnp.where(pos == 0, zp, pltpu.roll(z, 1, axis=0))
    z_next = jnp.where(pos == n - 1, zn, pltpu.roll(z, n - 1, axis=0))
    w = w_ref[...]
    conv = z_prev * w[0:1] + z * w[1:2] + z_next * w[2:3]
    o_ref[...] = (b_ref[...].astype(F32) * conv).astype(o_ref.dtype)


def _short_conv(p, conv_w8, layer, seq_len):
    m = p.shape[0]
    tr = CONV_ROWS
    per = tr // HALO
    n_halo = m // HALO

    def prev_blk(col):
        return lambda i: (jnp.maximum(i * per - 1, 0), col)

    def next_blk(col):
        return lambda i: (jnp.minimum((i + 1) * per, n_halo - 1), col)

    cc, ch = OFF_CC // C_W, OFF_CH // C_W
    return pl.pallas_call(
        functools.partial(_conv_kernel, tiles_per_seq=seq_len // tr),
        out_shape=jax.ShapeDtypeStruct((m, C_W), BF16),
        grid=(m // tr,),
        in_specs=[
            pl.BlockSpec((tr, C_W), lambda i: (i, OFF_CB // C_W)),
            pl.BlockSpec((tr, C_W), lambda i: (i, cc)),
            pl.BlockSpec((tr, C_W), lambda i: (i, ch)),
            pl.BlockSpec((HALO, C_W), prev_blk(cc)),
            pl.BlockSpec((HALO, C_W), prev_blk(ch)),
            pl.BlockSpec((HALO, C_W), next_blk(cc)),
            pl.BlockSpec((HALO, C_W), next_blk(ch)),
            pl.BlockSpec((None, 8, C_W), lambda i: (layer, 0, 0)),
        ],
        out_specs=pl.BlockSpec((tr, C_W), lambda i: (i, 0)),
        compiler_params=_cparams(("arbitrary",), 32),
        name="mixer_c_conv",
    )(p, p, p, p, p, p, p, conv_w8)


def _rope(x, cos, sin_lo, sin_hi):
    x = x.astype(F32)
    return x * cos + pltpu.roll(x, LANES - 32, axis=1) * sin_lo + pltpu.roll(x, 32, axis=1) * sin_hi


def _softmax_av(s_parts, v_parts, extra_logit=None):
    m = s_parts[0].max(axis=-1, keepdims=True)
    for s in s_parts[1:]:
        m = jnp.maximum(m, s.max(axis=-1, keepdims=True))
    if extra_logit is not None:
        m = jnp.maximum(m, extra_logit)
    denom = jnp.exp(extra_logit - m) if extra_logit is not None else 0.0
    acc = None
    for s, v in zip(s_parts, v_parts):
        e = jnp.exp(s - m)
        denom = denom + e.sum(axis=-1, keepdims=True)
        pv = jnp.dot(e.astype(BF16), v, preferred_element_type=F32)
        acc = pv if acc is None else acc + pv
    return acc / denom


def _qk(q, k):
    return lax.dot_general(q, k, (((1,), (1,)), ((), ())), preferred_element_type=F32)


def _sink_column(sink_ref, h, rows_per_head):
    rid = lax.broadcasted_iota(jnp.int32, (B_GROUP * rows_per_head, 1), 0) // rows_per_head
    col = jnp.zeros((B_GROUP * rows_per_head, 1), F32)
    for g in range(B_GROUP):
        col = jnp.where(rid == g, sink_ref[h * B_GROUP + g], col)
    return col


def _win_attn_kernel(sink_ref, q_ref, k_ref, v_ref, kc_ref, vc_ref, cos_ref, slo_ref, shi_ref, mask_ref, o_ref):
    n = pl.program_id(1)
    blk = B_WINDOW
    band = 3 * blk
    kblk = jnp.clip(n - 1, 0, SEQ // blk - 3)
    q0 = pl.multiple_of(n * blk, blk)
    k0 = pl.multiple_of(kblk * blk, blk)
    scale = HEAD_DIM ** -0.5

    cos_q, slo_q, shi_q = cos_ref[pl.ds(q0, blk), :], slo_ref[pl.ds(q0, blk), :], shi_ref[pl.ds(q0, blk), :]
    cos_k, slo_k, shi_k = cos_ref[pl.ds(k0, band), :], slo_ref[pl.ds(k0, band), :], shi_ref[pl.ds(k0, band), :]
    window_bias = mask_ref[n - kblk]

    for h in range(B_KV_HEADS):
        hs = slice(h * HEAD_DIM, (h + 1) * HEAD_DIM)
        k_loc = _rope(k_ref[pl.ds(k0, band), hs], cos_k, slo_k, shi_k).astype(BF16)
        v_loc = v_ref[pl.ds(k0, band), hs]
        k_ctx = kc_ref[:, hs]
        v_ctx = vc_ref[:, hs]
        qs = []
        for g in range(B_GROUP):
            c0 = (h * B_GROUP + g) * HEAD_DIM
            qs.append(_rope(q_ref[:, c0:c0 + HEAD_DIM], cos_q, slo_q, shi_q).astype(BF16))
        qst = jnp.concatenate(qs, axis=0)
        s_loc = _qk(qst, k_loc) * scale + window_bias
        s_ctx = _qk(qst, k_ctx) * scale
        out = _softmax_av([s_ctx, s_loc], [v_ctx, v_loc], _sink_column(sink_ref, h, blk))
        for g in range(B_GROUP):
            c0 = (h * B_GROUP + g) * HEAD_DIM
            o_ref[:, c0:c0 + HEAD_DIM] = out[g * blk:(g + 1) * blk].astype(o_ref.dtype)


def _window_bias_table():
    i = np.arange(B_WINDOW)[:, None]
    j = np.arange(3 * B_WINDOW)[None, :]
    tabs = []
    for c in range(3):
        rel = j - c * B_WINDOW - i
        one = np.where(np.abs(rel) <= B_WINDOW, 0.0, NEG_INF).astype(np.float32)
        tabs.append(np.tile(one, (B_GROUP, 1)))
    return jnp.asarray(np.stack(tabs))


def _win_attn(p, pc, kc_blk, vc_blk, sink, rope_tabs, batch):
    nb = SEQ // B_WINDOW
    kvw = B_KV_W
    mask_tab = _window_bias_table()
    return pl.pallas_call(
        _win_attn_kernel,
        out_shape=jax.ShapeDtypeStruct((batch * SEQ, B_Q_W), BF16),
        grid=(batch, nb),
        in_specs=[
            pl.BlockSpec(memory_space=pltpu.SMEM),
            pl.BlockSpec((B_WINDOW, B_Q_W), lambda b, n: (b * nb + n, OFF_BQ // B_Q_W)),
            pl.BlockSpec((SEQ, kvw), lambda b, n: (b, OFF_BK // kvw)),
            pl.BlockSpec((SEQ, kvw), lambda b, n: (b, OFF_BV // kvw)),
            pl.BlockSpec((CTX_LEN, kvw), lambda b, n: (b, kc_blk)),
            pl.BlockSpec((CTX_LEN, kvw), lambda b, n: (b, vc_blk)),
            pl.BlockSpec((SEQ, HEAD_DIM), lambda b, n: (0, 0)),
            pl.BlockSpec((SEQ, HEAD_DIM), lambda b, n: (0, 0)),
            pl.BlockSpec((SEQ, HEAD_DIM), lambda b, n: (0, 0)),
            pl.BlockSpec(mask_tab.shape, lambda b, n: (0, 0, 0)),
        ],
        out_specs=pl.BlockSpec((B_WINDOW, B_Q_W), lambda b, n: (b * nb + n, 0)),
        compiler_params=_cparams(("arbitrary", "arbitrary"), 40),
        name="mixer_b_window_attn",
    )(sink, p, p, p, pc, pc, *rope_tabs, mask_tab)


def _ctx_gqa_kernel(sink_ref, q_ref, kc_ref, vc_ref, o_ref):
    scale = HEAD_DIM ** -0.5
    lc = q_ref.shape[0]
    for h in range(B_KV_HEADS):
        hs = slice(h * HEAD_DIM, (h + 1) * HEAD_DIM)
        qst = jnp.concatenate(
            [q_ref[:, (h * B_GROUP + g) * HEAD_DIM:(h * B_GROUP + g + 1) * HEAD_DIM] for g in range(B_GROUP)],
            axis=0)
        out = _softmax_av([_qk(qst, kc_ref[:, hs]) * scale], [vc_ref[:, hs]], _sink_column(sink_ref, h, lc))
        for g in range(B_GROUP):
            c0 = (h * B_GROUP + g) * HEAD_DIM
            o_ref[:, c0:c0 + HEAD_DIM] = out[g * lc:(g + 1) * lc].astype(o_ref.dtype)


def _ctx_gqa(pc, sink, batch):
    kvw = B_KV_W
    return pl.pallas_call(
        _ctx_gqa_kernel,
        out_shape=jax.ShapeDtypeStruct((batch * CTX_LEN, B_Q_W), BF16),
        grid=(batch,),
        in_specs=[
            pl.BlockSpec(memory_space=pltpu.SMEM),
            pl.BlockSpec((CTX_LEN, B_Q_W), lambda b: (b, OFF_BQ // B_Q_W)),
            pl.BlockSpec((CTX_LEN, kvw), lambda b: (b, OFF_BK // kvw)),
            pl.BlockSpec((CTX_LEN, kvw), lambda b: (b, OFF_BV // kvw)),
        ],
        out_specs=pl.BlockSpec((CTX_LEN, B_Q_W), lambda b: (b, 0)),
        compiler_params=_cparams(("arbitrary",), 32),
        name="mixer_b_ctx_attn",
    )(sink, pc, pc, pc)


NA_ROWS_PER_STEP = 4
GRID_H = SEQ // GRID_W
NA_LOC = NA_KH * GRID_W


def _na_kernel(q_ref, k_ref, v_ref, kc_ref, vc_ref, bias_ref, o_ref):
    step = pl.program_id(1)
    scale = HEAD_DIM ** -0.5
    blocks, s_loc, s_ctx = [], [], []
    for rr in range(NA_ROWS_PER_STEP):
        r = step * NA_ROWS_PER_STEP + rr
        rs = jnp.clip(r - NA_KH // 2, 0, GRID_H - NA_KH)
        k0 = pl.multiple_of(rs * GRID_W, GRID_W)
        d = r - rs
        qrows = slice(rr * GRID_W, (rr + 1) * GRID_W)
        for h in range(D_HEADS):
            hs = slice(h * HEAD_DIM, (h + 1) * HEAD_DIM)
            q = q_ref[qrows, hs]
            s_loc.append(_qk(q, k_ref[pl.ds(k0, NA_LOC), hs]) * scale + bias_ref[h, d])
            s_ctx.append(_qk(q, kc_ref[:, hs]) * scale)
            blocks.append((qrows, hs, k0))
    s_loc = jnp.concatenate(s_loc, axis=0)
    s_ctx = jnp.concatenate(s_ctx, axis=0)
    m = jnp.maximum(s_loc.max(axis=-1, keepdims=True), s_ctx.max(axis=-1, keepdims=True))
    e_loc = jnp.exp(s_loc - m)
    e_ctx = jnp.exp(s_ctx - m)
    denom = e_loc.sum(axis=-1, keepdims=True) + e_ctx.sum(axis=-1, keepdims=True)
    e_loc = e_loc.astype(BF16)
    e_ctx = e_ctx.astype(BF16)
    for i, (qrows, hs, k0) in enumerate(blocks):
        rows = slice(i * GRID_W, (i + 1) * GRID_W)
        acc = jnp.dot(e_ctx[rows], vc_ref[:, hs], preferred_element_type=F32)
        acc = acc + jnp.dot(e_loc[rows], v_ref[pl.ds(k0, NA_LOC), hs], preferred_element_type=F32)
        o_ref[qrows, hs] = (acc / denom[rows]).astype(o_ref.dtype)


def _na_bias_table(rpb):
    rpb = rpb.astype(F32)
    rows = []
    for c in range(GRID_W):
        cstart = min(max(c - NA_KW // 2, 0), GRID_W - NA_KW)
        first_dc = cstart - c + NA_KW - 1
        win = rpb[:, :, first_dc:first_dc + NA_KW]
        rows.append(jnp.pad(win, ((0, 0), (0, 0), (cstart, GRID_W - NA_KW - cstart)), constant_values=NEG_INF))
    tab0 = jnp.stack(rows, axis=2)
    slabs = []
    for d in range(NA_KH):
        sl = tab0[:, NA_KH - 1 - d:2 * NA_KH - 1 - d]
        slabs.append(jnp.transpose(sl, (0, 2, 1, 3)).reshape(D_HEADS, GRID_W, NA_LOC))
    return jnp.stack(slabs, axis=1)


def _na_attn(p, pc, kc_blk, vc_blk, bias_tab, batch):
    rows_q = NA_ROWS_PER_STEP * GRID_W
    steps = SEQ // rows_q
    return pl.pallas_call(
        _na_kernel,
        out_shape=jax.ShapeDtypeStruct((batch * SEQ, D_W), BF16),
        grid=(batch, steps),
        in_specs=[
            pl.BlockSpec((rows_q, D_W), lambda b, s: (b * steps + s, OFF_DQ // D_W)),
            pl.BlockSpec((SEQ, D_W), lambda b, s: (b, OFF_DK // D_W)),
            pl.BlockSpec((SEQ, D_W), lambda b, s: (b, OFF_DV // D_W)),
            pl.BlockSpec((CTX_LEN, D_W), lambda b, s: (b, kc_blk)),
            pl.BlockSpec((CTX_LEN, D_W), lambda b, s: (b, vc_blk)),
            pl.BlockSpec((D_HEADS, NA_KH, GRID_W, NA_LOC), lambda b, s: (0, 0, 0, 0)),
        ],
        out_specs=pl.BlockSpec((rows_q, D_W), lambda b, s: (b * steps + s, 0)),
        compiler_params=_cparams(("arbitrary", "arbitrary"), 48),
        name="mixer_d_neighborhood_attn",
    )(p, p, p, pc, pc, bias_tab)


def _ctx_mha_kernel(q_ref, kc_ref, vc_ref, o_ref):
    scale = HEAD_DIM ** -0.5
    for h in range(D_HEADS):
        hs = slice(h * HEAD_DIM, (h + 1) * HEAD_DIM)
        s = _qk(q_ref[:, hs], kc_ref[:, hs]) * scale
        o_ref[:, hs] = _softmax_av([s], [vc_ref[:, hs]]).astype(o_ref.dtype)


def _ctx_mha(pc, batch):
    return pl.pallas_call(
        _ctx_mha_kernel,
        out_shape=jax.ShapeDtypeStruct((batch * CTX_LEN, D_W), BF16),
        grid=(batch,),
        in_specs=[
            pl.BlockSpec((CTX_LEN, D_W), lambda b: (b, OFF_DQ // D_W)),
            pl.BlockSpec((CTX_LEN, D_W), lambda b: (b, OFF_DK // D_W)),
            pl.BlockSpec((CTX_LEN, D_W), lambda b: (b, OFF_DV // D_W)),
        ],
        out_specs=pl.BlockSpec((CTX_LEN, D_W), lambda b: (b, 0)),
        compiler_params=_cparams(("arbitrary",), 32),
        name="mixer_d_ctx_attn",
    )(pc, pc, pc)


BR_W = (A_CH, B_Q_W, C_W, D_W)
BR_OFF = (0, A_CH, A_CH + B_Q_W, A_CH + B_Q_W + C_W)


def _merge_kernel(ya_ref, yb_ref, yc_ref, yd_ref, w_ref, g0_ref, g1_ref, g2_ref, g3_ref, o_ref):
    z = None
    for y_ref, g_ref, off, width in zip((ya_ref, yb_ref, yc_ref, yd_ref),
                                        (g0_ref, g1_ref, g2_ref, g3_ref), BR_OFF, BR_W):
        t = jnp.dot(y_ref[...], w_ref[off:off + width, :].astype(BF16), preferred_element_type=F32)
        t = jax.nn.sigmoid(g_ref[...].astype(F32)) * t
        z = t if z is None else z + t
    o_ref[...] = z.astype(o_ref.dtype)


def _merge(p, ys, w_branch, layer):
    m = p.shape[0]
    tm, tn = TM_MERGE, TN_MERGE
    gate_specs = [pl.BlockSpec((tm, tn), functools.partial(
        lambda i, j, b: (i, (OFF_G + b * D_MODEL) // tn + j), b=b)) for b in range(4)]
    return pl.pallas_call(
        _merge_kernel,
        out_shape=jax.ShapeDtypeStruct((m, D_MODEL), BF16),
        grid=(m // tm, D_MODEL // tn),
        in_specs=[pl.BlockSpec((tm, w), lambda i, j: (i, 0)) for w in BR_W]
        + [pl.BlockSpec((None, MIX_W, tn), lambda i, j: (layer, 0, j))] + gate_specs,
        out_specs=pl.BlockSpec((tm, tn), lambda i, j: (i, j)),
        compiler_params=_cparams(("arbitrary", "arbitrary"), 48),
        name="branch_merge",
    )(*ys, w_branch, p, p, p, p)


ROUTE_LANES = LANES
LANE_IDX = N_EXPERTS
LANE_WT = N_EXPERTS + TOP_K


def _router_kernel(x_ref, g_ref, sc_ref, sh_ref, wr_ref, xn_ref, route_ref):
    xn = _norm_mod(x_ref[...], g_ref[...], sc_ref[0], sh_ref[0])
    xn_ref[...] = xn.astype(BF16).reshape(xn_ref.shape)
    x_hi = xn.astype(BF16)
    x_lo = (xn - x_hi.astype(F32)).astype(BF16)
    wr = wr_ref[...]
    w_hi = wr.astype(BF16)
    w_lo = (wr - w_hi.astype(F32)).astype(BF16)
    logits = (jnp.dot(x_hi, w_hi, preferred_element_type=F32)
              + (jnp.dot(x_hi, w_lo, preferred_element_type=F32) + jnp.dot(x_lo, w_hi, preferred_element_type=F32)))
    lane = lax.broadcasted_iota(jnp.int32, logits.shape, 1).astype(F32)
    logits = jnp.where(lane < N_EXPERTS, logits, -jnp.inf)
    m1 = logits.max(axis=-1, keepdims=True)
    i1 = jnp.where(logits == m1, lane, float(ROUTE_LANES)).min(axis=-1, keepdims=True)
    rest = jnp.where(lane == i1, -jnp.inf, logits)
    m2 = rest.max(axis=-1, keepdims=True)
    i2 = jnp.where(rest == m2, lane, float(ROUTE_LANES)).min(axis=-1, keepdims=True)
    e2 = jnp.exp(m2 - m1)
    w1 = 1.0 / (1.0 + e2)
    w2 = e2 / (1.0 + e2)
    out = jnp.where(lane == LANE_IDX, i1, 0.0)
    out = jnp.where(lane == LANE_IDX + 1, i2, out)
    out = jnp.where(lane == LANE_WT, w1, out)
    out = jnp.where(lane == LANE_WT + 1, w2, out)
    route_ref[...] = out


def _router(h, g, sc, sh, mod_row, w_router_pad):
    t = h.shape[0]
    tm = 512
    return pl.pallas_call(
        _router_kernel,
        out_shape=(jax.ShapeDtypeStruct((t, D_MODEL // LANES, LANES), BF16),
                   jax.ShapeDtypeStruct((t, ROUTE_LANES), F32)),
        grid=(t // tm,),
        in_specs=[
            pl.BlockSpec((tm, D_MODEL), lambda i: (i, 0)),
            pl.BlockSpec((1, D_MODEL), lambda i: (0, 0)),
            pl.BlockSpec((1, 1, D_MODEL), lambda i: (mod_row(i, tm), 0, 0)),
            pl.BlockSpec((1, 1, D_MODEL), lambda i: (mod_row(i, tm), 0, 0)),
            pl.BlockSpec((D_MODEL, ROUTE_LANES), lambda i: (0, 0)),
        ],
        out_specs=(pl.BlockSpec((tm, D_MODEL // LANES, LANES), lambda i: (i, 0, 0)),
                   pl.BlockSpec((tm, ROUTE_LANES), lambda i: (i, 0))),
        compiler_params=_cparams(("arbitrary",), 40),
        name="moe_router",
    )(h, g, sc, sh, w_router_pad)


def _gather_rows_kernel(idx_ref, src_ref, o_ref, *scratch, n_sources, flat):
    sem = scratch[-1]
    dst_ref = scratch[0] if flat else o_ref
    rows = dst_ref.shape[-3]

    def row_copy(s, r, src_row):
        return pltpu.make_async_copy(src_ref.at[src_row], dst_ref.at[s, r], sem)

    def issue(i, carry):
        for u in range(GATHER_UNROLL):
            r = i * GATHER_UNROLL + u
            for s in range(n_sources):
                row_copy(s, r, idx_ref[0, s, r]).start(priority=(u * n_sources + s) % 2)
        return carry

    lax.fori_loop(0, rows // GATHER_UNROLL, issue, 0)

    def drain(i, carry):
        for u in range(GATHER_UNROLL):
            for s in range(n_sources):
                row_copy(s, i * GATHER_UNROLL + u, 0).wait()
        return carry

    lax.fori_loop(0, rows // GATHER_UNROLL, drain, 0)
    if flat:
        o_ref[...] = dst_ref[0].reshape(o_ref.shape)


def _gather_rows(src3, idx, n_sources, flat=False):
    _, s_dim, lanes = src3.shape
    r = idx.shape[1]
    tg = TG_ROWS
    idx_blocks = idx.reshape(n_sources, r // tg, tg).transpose(1, 0, 2)
    if flat:
        assert n_sources == 1
        out_shape = jax.ShapeDtypeStruct((r, s_dim * lanes), src3.dtype)
        out_spec = pl.BlockSpec((tg, s_dim * lanes), lambda i: (i, 0))
        scratch = [pltpu.VMEM((1, tg, s_dim, lanes), src3.dtype), pltpu.SemaphoreType.DMA]
    else:
        out_shape = jax.ShapeDtypeStruct((n_sources, r, s_dim, lanes), src3.dtype)
        out_spec = pl.BlockSpec((n_sources, tg, s_dim, lanes), lambda i: (0, i, 0, 0))
        scratch = [pltpu.SemaphoreType.DMA]
    return pl.pallas_call(
        functools.partial(_gather_rows_kernel, n_sources=n_sources, flat=flat),
        out_shape=out_shape,
        grid=(r // tg,),
        in_specs=[
            pl.BlockSpec((1, n_sources, tg), lambda i: (i, 0, 0), memory_space=pltpu.SMEM),
            pl.BlockSpec(memory_space=pl.ANY),
        ],
        out_specs=out_spec,
        scratch_shapes=scratch,
        compiler_params=_cparams(("arbitrary",), 32),
        name="row_gather",
    )(idx_blocks, src3)


def _moe_up_kernel(te_ref, nu_ref, x_ref, w1_ref, w3_ref, w2_ref, o_ref, w2b_ref):
    used = pl.program_id(1) < nu_ref[0]
    w2b_ref[...] = w2_ref[...].astype(BF16)

    @pl.when(used)
    def _():
        _swiglu_store(x_ref[...], w1_ref, w3_ref, o_ref)

    @pl.when(jnp.logical_not(used))
    def _():
        o_ref[...] = jnp.zeros_like(o_ref)


def _moe_up(xs, w1, w3, w2, layer, tile_expert, n_used):
    r = xs.shape[0]
    tm, tn = TM_MOE, TN_MOE_UP
    n_t = r // tm
    w2_rows = N_EXPERTS * FFN_DIM
    steps = (FFN_DIM // tn) * n_t
    cast_rows = _cast_block_rows(w2_rows, steps)
    cast_blocks = w2_rows // cast_rows
    w2_flat = w2.reshape(-1, D_MODEL)

    def cast_blk(j, t, te, nu):
        return (layer * cast_blocks + jnp.minimum(j * n_t + t, cast_blocks - 1), 0)

    def cast_out_blk(j, t, te, nu):
        return (jnp.minimum(j * n_t + t, cast_blocks - 1), 0)

    w_spec = pl.BlockSpec((None, None, D_MODEL, tn), lambda j, t, te, nu: (layer, te[t], 0, j))
    grid_spec = pltpu.PrefetchScalarGridSpec(
        num_scalar_prefetch=2,
        grid=(FFN_DIM // tn, n_t),
        in_specs=[pl.BlockSpec((tm, D_MODEL), lambda j, t, te, nu: (jnp.minimum(t, nu[0] - 1), 0)),
                  w_spec, w_spec,
                  pl.BlockSpec((cast_rows, D_MODEL), cast_blk)],
        out_specs=(pl.BlockSpec((tm, tn), lambda j, t, te, nu: (t, j)),
                   pl.BlockSpec((cast_rows, D_MODEL), cast_out_blk)),
    )
    hm, w2b = pl.pallas_call(
        _moe_up_kernel,
        out_shape=(jax.ShapeDtypeStruct((r, FFN_DIM), BF16),
                   jax.ShapeDtypeStruct((w2_rows, D_MODEL), BF16)),
        grid_spec=grid_spec,
        compiler_params=_cparams(("arbitrary", "arbitrary"), 56),
        name="moe_up",
    )(tile_expert, n_used, xs, w1, w3, w2_flat)
    return hm, w2b.reshape(1, N_EXPERTS, FFN_DIM, D_MODEL)


def _moe_down_kernel(te_ref, nu_ref, x_ref, w_ref, o_ref):
    used = pl.program_id(1) < nu_ref[0]

    @pl.when(used)
    def _():
        x = x_ref[...]
        for c0 in range(0, o_ref.shape[1], MXU_COLS):
            cs = slice(c0, c0 + MXU_COLS)
            o_ref[:, cs] = jnp.dot(x, w_ref[:, cs], preferred_element_type=F32).astype(o_ref.dtype)

    @pl.when(jnp.logical_not(used))
    def _():
        o_ref[...] = jnp.zeros_like(o_ref)


def _moe_down(hmid, w2, layer, tile_expert, n_used):
    r = hmid.shape[0]
    tm, tn = TM_MOE, TN_MOE_DOWN
    grid_spec = pltpu.PrefetchScalarGridSpec(
        num_scalar_prefetch=2,
        grid=(D_MODEL // tn, r // tm),
        in_specs=[
            pl.BlockSpec((tm, FFN_DIM), lambda j, t, te, nu: (jnp.minimum(t, nu[0] - 1), 0)),
            pl.BlockSpec((None, None, FFN_DIM, tn), lambda j, t, te, nu: (layer, te[t], 0, j)),
        ],
        out_specs=pl.BlockSpec((tm, tn), lambda j, t, te, nu: (t, j)),
    )
    return pl.pallas_call(
        _moe_down_kernel,
        out_shape=jax.ShapeDtypeStruct((r, D_MODEL), BF16),
        grid_spec=grid_spec,
        compiler_params=_cparams(("arbitrary", "arbitrary"), 60),
        name="moe_down",
    )(tile_expert, n_used, hmid, w2)


def _route_plan(route, tm):
    t = route.shape[0]
    idx = route[:, LANE_IDX:LANE_IDX + TOP_K].astype(jnp.int32)
    e_flat = idx.T.reshape(-1)
    onehot = (e_flat[:, None] == jnp.arange(N_EXPERTS)[None, :]).astype(jnp.int32)
    counts = onehot.sum(axis=0)
    rank = (onehot * (jnp.cumsum(onehot, axis=0) - onehot)).sum(axis=1)
    tiles_e = (counts + tm - 1) // tm
    tile_end = jnp.cumsum(tiles_e)
    tile_start = tile_end - tiles_e
    pos = (onehot * tile_start[None, :]).sum(axis=1) * tm + rank
    n_tiles = TOP_K * t // tm + N_EXPERTS
    n_used = tile_end[-1]
    tile_ids = jnp.minimum(jnp.arange(n_tiles), n_used - 1)
    tile_expert = jnp.minimum((tile_ids[:, None] >= tile_end[None, :]).sum(axis=1), N_EXPERTS - 1)
    row_token = (jnp.arange(n_tiles * tm, dtype=jnp.int32) % t).at[pos].set(
        jnp.tile(jnp.arange(t, dtype=jnp.int32), TOP_K))
    return (pos.reshape(TOP_K, t).astype(jnp.int32), row_token,
            tile_expert.astype(jnp.int32), n_used.reshape(1).astype(jnp.int32))


FINAL_ROWS = 512


def _final_kernel(idx_ref, idx_next_ref, y_ref, h_ref, route_ref, gt_ref, g_ref, o_ref, buf, sems):
    i = pl.program_id(0)
    half = FINAL_ROWS // 2

    def row_copy(slot, k, r, src_row):
        return pltpu.make_async_copy(y_ref.at[src_row], buf.at[slot, k, r], sems.at[slot])

    def start_half(ids_ref, slot, first_row):
        def body(g, carry):
            for u in range(GATHER_UNROLL):
                r = g * GATHER_UNROLL + u
                for k in range(TOP_K):
                    row_copy(slot, k, r, ids_ref[0, k, first_row + r]).start(priority=(u * TOP_K + k) % 2)
            return carry
        lax.fori_loop(0, half // GATHER_UNROLL, body, 0)

    def wait_half(slot):
        def body(g, carry):
            for u in range(GATHER_UNROLL):
                for k in range(TOP_K):
                    row_copy(slot, k, g * GATHER_UNROLL + u, 0).wait()
            return carry
        lax.fori_loop(0, half // GATHER_UNROLL, body, 0)

    def combine_half(slot, first_row):
        rows = slice(first_row, first_row + half)
        w1 = route_ref[rows, LANE_WT:LANE_WT + 1]
        w2 = route_ref[rows, LANE_WT + 1:LANE_WT + 2]
        y0 = buf[slot, 0].reshape(half, D_MODEL).astype(F32)
        y1 = buf[slot, 1].reshape(half, D_MODEL).astype(F32)
        h = h_ref[rows, :] + gt_ref[0] * (w1 * y0 + w2 * y1)
        o_ref[rows, :] = h * lax.rsqrt(jnp.mean(h * h, axis=-1, keepdims=True) + NORM_EPS) * g_ref[...]

    @pl.when(i == 0)
    def _():
        start_half(idx_ref, 0, 0)

    start_half(idx_ref, 1, half)
    wait_half(0)
    combine_half(0, 0)

    @pl.when(i + 1 < pl.num_programs(0))
    def _():
        start_half(idx_next_ref, 0, 0)

    wait_half(1)
    combine_half(1, half)


def _final(h, ye3, pos, route, gt, mod_row, g_final):
    t = h.shape[0]
    tm = FINAL_ROWS
    n = t // tm
    _, s_dim, lanes = ye3.shape
    idx_blocks = pos.reshape(TOP_K, n, tm).transpose(1, 0, 2)
    return pl.pallas_call(
        _final_kernel,
        out_shape=jax.ShapeDtypeStruct((t, D_MODEL), F32),
        grid=(n,),
        in_specs=[
            pl.BlockSpec((1, TOP_K, tm), lambda i: (i, 0, 0), memory_space=pltpu.SMEM),
            pl.BlockSpec((1, TOP_K, tm), lambda i: (jnp.minimum(i + 1, n - 1), 0, 0), memory_space=pltpu.SMEM),
            pl.BlockSpec(memory_space=pl.ANY),
            pl.BlockSpec((tm, D_MODEL), lambda i: (i, 0)),
            pl.BlockSpec((tm, ROUTE_LANES), lambda i: (i, 0)),
            pl.BlockSpec((1, 1, D_MODEL), lambda i: (mod_row(i, tm), 0, 0)),
            pl.BlockSpec((1, D_MODEL), lambda i: (0, 0)),
        ],
        out_specs=pl.BlockSpec((tm, D_MODEL), lambda i: (i, 0)),
        scratch_shapes=[pltpu.VMEM((2, TOP_K, tm // 2, s_dim, lanes), ye3.dtype),
                        pltpu.SemaphoreType.DMA((2,))],
        compiler_params=_cparams(("arbitrary",), 48),
        name="final_combine_norm",
    )(idx_blocks, idx_blocks, ye3, h, route, gt, g_final)


def _rope_tables():
    pos = jnp.arange(SEQ)
    f = HEAD_DIM // 4
    inv = ROPE_BASE ** (-jnp.arange(f, dtype=F32) / f)
    ang_row = (pos // GRID_W).astype(F32)[:, None] * inv[None, :]
    ang_col = (pos % GRID_W).astype(F32)[:, None] * inv[None, :]
    ang = jnp.concatenate([ang_row, ang_row, ang_col, ang_col], axis=1)
    cos, sin = jnp.cos(ang), jnp.sin(ang)
    first = jnp.asarray(((np.arange(HEAD_DIM) // f) % 2 == 0)[None, :])
    return cos, jnp.where(first, -sin, 0.0), jnp.where(first, 0.0, sin)


def kernel(x, c, ctx, c_ctx, w_ada, b_ada, g_mix, g_ffn, w_in, a_norm_g, a_ws, a_bs, b_sink, c_conv,
           d_rpb, w_branch, w_out, ffn_w1, ffn_w3, ffn_w2, w_router, moe_w1, moe_w3, moe_w2, g_final):
    batch, seq, _ = x.shape
    depth = w_in.shape[0]
    assert seq == SEQ and ctx.shape[1] == CTX_LEN and batch + 1 <= MOD_ROWS
    assert depth % 2 == 0, "the fused residual + final-norm epilogue lives in the MoE (odd, last) layer"
    t = batch * seq
    tc = batch * CTX_LEN

    cond = jnp.zeros((MOD_ROWS, D_MODEL), F32).at[:batch].set(c).at[batch].set(c_ctx)
    mods = _ada(cond, w_ada, b_ada)
    rope_tabs = _rope_tables()
    conv_w8 = jnp.zeros((depth, 8, C_W), F32).at[:, :3].set(c_conv)

    mix_w = [w[0:1].astype(BF16) for w in (w_in, w_branch, w_out)]

    def lat_row(i, tm):
        return (i * tm) // SEQ

    def ctx_row(i, tm):
        return batch

    h = x.reshape(t, D_MODEL)
    hc = ctx.reshape(tc, D_MODEL)
    out = None
    for layer in range(depth):
        last = layer == depth - 1
        sh1, sc1, gt1, sh2, sc2, gt2 = [m.reshape(MOD_ROWS, 1, D_MODEL)
                                        for m in jnp.split(mods[layer], 6, axis=-1)]
        g_mix_l = g_mix[layer].reshape(1, D_MODEL)
        g_ffn_l = g_ffn[layer].reshape(1, D_MODEL)
        bias_tab = _na_bias_table(d_rpb[layer])
        w_in_l, w_branch_l, w_out_l = mix_w
        j = layer // 2
        dense = layer % 2 == 0

        if last:
            pc = _norm_mm(hc, g_mix_l, sc1, sh1, ctx_row, (w_in_l,), 0,
                          lambda n: jnp.where(n == 0, OFF_BK // TN_KV, OFF_DK // TN_KV - 1 + n),
                          3 * TN_KV, TM_PROJ, TN_KV, "ctx_kv_proj")
            kcb, vcb, kcd, vcd = 0, 1, 1, 2
        else:
            pc = _norm_mm(hc, g_mix_l, sc1, sh1, ctx_row, (w_in_l,), 0, lambda n: n,
                          IN_W, TM_PROJ, TN_IN, "ctx_in_proj")
            kcb, vcb, kcd, vcd = OFF_BK // B_KV_W, OFF_BV // B_KV_W, OFF_DK // D_W, OFF_DV // D_W

        if dense:
            ffn_cast = [(ffn_w1.reshape(-1, FFN_DIM), j * D_MODEL, D_MODEL),
                        (ffn_w3.reshape(-1, FFN_DIM), j * D_MODEL, D_MODEL),
                        (ffn_w2.reshape(-1, D_MODEL), j * FFN_DIM, FFN_DIM)]
            px, (w1b, w3b, w2b) = _norm_mm(h, g_mix_l, sc1, sh1, lat_row, (w_in_l,), 0, lambda n: n,
                                           IN_W, TM_PROJ, TN_IN, "in_proj", cast=ffn_cast)
            w1b, w3b, w2b = w1b[None], w3b[None], w2b[None]
        else:
            px = _norm_mm(h, g_mix_l, sc1, sh1, lat_row, (w_in_l,), 0, lambda n: n,
                          IN_W, TM_PROJ, TN_IN, "in_proj")
        ys = (
            _gmlp(px, a_norm_g, a_ws, a_bs, layer),
            _win_attn(px, pc, kcb, vcb, b_sink[layer], rope_tabs, batch),
            _short_conv(px, conv_w8, layer, SEQ),
            _na_attn(px, pc, kcd, vcd, bias_tab, batch),
        )
        z = _merge(px, ys, w_branch_l, 0)
        h = _res_mm(z, w_out_l, 0, h, gt1, lat_row, TM_OUT, TN_OUT, "out_proj")

        if not last:
            ysc = (
                _gmlp(pc, a_norm_g, a_ws, a_bs, layer),
                _ctx_gqa(pc, b_sink[layer], batch),
                _short_conv(pc, conv_w8, layer, CTX_LEN),
                _ctx_mha(pc, batch),
            )
            zc = _merge(pc, ysc, w_branch_l, 0)
            hc = _res_mm(zc, w_out_l, 0, hc, gt1, ctx_row, TM_OUT, TN_OUT, "ctx_out_proj")

        if dense:
            if last:
                hm = _norm_mm(h, g_ffn_l, sc2, sh2, lat_row, (w1b, w3b), 0, lambda n: n,
                              FFN_DIM, TM_PROJ, TN_FFN, "ffn_up")
            else:
                nxt = layer + 1
                mix_cast = [(w_in.reshape(-1, IN_W), nxt * D_MODEL, D_MODEL),
                            (w_branch.reshape(-1, D_MODEL), nxt * MIX_W, MIX_W),
                            (w_out.reshape(-1, D_MODEL), nxt * D_MODEL, D_MODEL)]
                hm, mix_w = _norm_mm(h, g_ffn_l, sc2, sh2, lat_row, (w1b, w3b), 0, lambda n: n,
                                     FFN_DIM, TM_PROJ, TN_FFN, "ffn_up", cast=mix_cast)
                mix_w = [w[None] for w in mix_w]
            h = _res_mm(hm, w2b, 0, h, gt2, lat_row, TM_DOWN, TN_DOWN, "ffn_down", weight_stationary=True)
            if not last:
                hmc = _norm_mm(hc, g_ffn_l, sc2, sh2, ctx_row, (w1b, w3b), 0, lambda n: n,
                               FFN_DIM, TM_PROJ, TN_FFN, "ctx_ffn_up")
                hc = _res_mm(hmc, w2b, 0, hc, gt2, ctx_row, TM_DOWN, TN_DOWN, "ctx_ffn_down",
                             weight_stationary=True)
        else:
            assert last, "the context MoE path would only be needed for an odd layer that is not the last"
            wr_pad = jnp.zeros((D_MODEL, ROUTE_LANES), F32).at[:, :N_EXPERTS].set(w_router[j])
            xn, route = _router(h, g_ffn_l, sc2, sh2, lat_row, wr_pad)
            pos, row_token, tile_expert, n_used = _route_plan(route, TM_MOE)
            xs = _gather_rows(xn, row_token[None, :], 1, flat=True)
            hm, w2b = _moe_up(xs, moe_w1, moe_w3, moe_w2, j, tile_expert, n_used)
            ye = _moe_down(hm, w2b, 0, tile_expert, n_used)
            out = _final(h, ye.reshape(-1, D_MODEL // LANES, LANES), pos, route, gt2, lat_row,
                         g_final.reshape(1, D_MODEL))

    return out.reshape(batch, seq, D_MODEL)
```

```python
import functools

import jax
import jax.numpy as jnp
import numpy as np
from jax import lax
from jax.experimental import pallas as pl
from jax.experimental.pallas import tpu as pltpu

F32 = jnp.float32
BF16 = jnp.bfloat16

D_MODEL = 2048
SEQ = 2048
CTX_LEN = 256
GRID_W = 64
HEAD_DIM = 128
ROPE_BASE = 10000.0
NORM_EPS = 1e-6
NEG_INF = -1e30

CHUNK = 128
A_GROUPS = 4
A_CH = 512
B_HEADS = 8
B_KV_HEADS = 2
B_GROUP = B_HEADS // B_KV_HEADS
B_WINDOW = 128
B_Q_W = B_HEADS * HEAD_DIM
B_KV_W = B_KV_HEADS * HEAD_DIM
C_W = 512
D_HEADS = 4
D_W = D_HEADS * HEAD_DIM
NA_KH = 8
NA_KW = 16

OFF_AU = 0
OFF_AV = OFF_AU + A_CH
OFF_BQ = OFF_AV + A_CH
OFF_BK = OFF_BQ + B_Q_W
OFF_BV = OFF_BK + B_KV_W
OFF_CB = OFF_BV + B_KV_W
OFF_CC = OFF_CB + C_W
OFF_CH = OFF_CC + C_W
OFF_DQ = OFF_CH + C_W
OFF_DK = OFF_DQ + D_W
OFF_DV = OFF_DK + D_W
OFF_G = OFF_DV + D_W
IN_W = OFF_G + 4 * D_MODEL
MIX_W = A_CH + B_Q_W + C_W + D_W

FFN_DIM = 7168
N_EXPERTS = 8
TOP_K = 2

LANES = 128
BF16_SUBLANES = 16
MIB = 2**20

TM_PROJ = 1024
TN_KV = 512
TN_IN = 1536
TN_FFN = 1024
TM_OUT, TN_OUT = 1024, 1024
TM_DOWN, TN_DOWN = 512, 1024
TM_MERGE, TN_MERGE = 1024, 512
TM_MOE = 512
TN_MOE_UP = 1024
TN_MOE_DOWN = 1024
TG_ROWS = 512
GATHER_UNROLL = 8
MOD_ROWS = 16


def _cparams(sem, vmem_mib):
    return pltpu.CompilerParams(dimension_semantics=sem, vmem_limit_bytes=vmem_mib * MIB)


def _silu(a):
    return a * jax.nn.sigmoid(a)


def _ada_kernel(c_ref, w_ref, b_ref, o_ref):
    c = c_ref[...]
    s = _silu(c).astype(BF16)
    o_ref[...] = jnp.dot(s, w_ref[...].astype(BF16), preferred_element_type=F32) + b_ref[...]


def _ada(cond, w_ada, b_ada):
    depth, _, n = w_ada.shape
    tn = 1024
    return pl.pallas_call(
        _ada_kernel,
        out_shape=jax.ShapeDtypeStruct((depth, MOD_ROWS, n), F32),
        grid=(depth, n // tn),
        in_specs=[
            pl.BlockSpec((MOD_ROWS, D_MODEL), lambda l, j: (0, 0)),
            pl.BlockSpec((None, D_MODEL, tn), lambda l, j: (l, 0, j)),
            pl.BlockSpec((None, 1, tn), lambda l, j: (l, 0, j)),
        ],
        out_specs=pl.BlockSpec((None, MOD_ROWS, tn), lambda l, j: (l, 0, j)),
        compiler_params=_cparams(("arbitrary", "arbitrary"), 40),
        name="ada_modulation",
    )(cond, w_ada, b_ada.reshape(depth, 1, n))


def _norm_mod(x, g, sc, sh):
    y = x * lax.rsqrt(jnp.mean(x * x, axis=-1, keepdims=True) + NORM_EPS) * g
    return y * (1.0 + sc) + sh


MXU_COLS = 256
NORM_CHUNK = 256
GMLP_ROWS = 2048


def _swiglu_store(x, w1_ref, w3_ref, o_ref):
    for c0 in range(0, o_ref.shape[1], MXU_COLS):
        cs = slice(c0, c0 + MXU_COLS)
        a = jnp.dot(x, w1_ref[:, cs].astype(BF16), preferred_element_type=F32)
        b = jnp.dot(x, w3_ref[:, cs].astype(BF16), preferred_element_type=F32)
        o_ref[:, cs] = (_silu(a) * b).astype(o_ref.dtype)


def _cast_block_rows(rows, steps):
    return next(d for d in range(BF16_SUBLANES, rows + 1, BF16_SUBLANES) if rows % d == 0 and rows // d <= steps)


def _norm_mm_kernel(x_ref, g_ref, sc_ref, sh_ref, *rest, swiglu, n_cast):
    n_w = 2 if swiglu else 1
    w_refs, cast_in = rest[:n_w], rest[n_w:n_w + n_cast]
    o_ref, cast_out, xn_ref = rest[n_w + n_cast], rest[n_w + n_cast + 1:n_w + 2 * n_cast + 1], rest[-1]

    for src, dst in zip(cast_in, cast_out):
        dst[...] = src[...].astype(BF16)

    @pl.when(pl.program_id(1) == 0)
    def _():
        def chunk(c, carry):
            rows = pl.ds(pl.multiple_of(c * NORM_CHUNK, NORM_CHUNK), NORM_CHUNK)
            xn_ref[rows, :] = _norm_mod(x_ref[rows, :], g_ref[...], sc_ref[0], sh_ref[0]).astype(BF16)
            return carry
        lax.fori_loop(0, x_ref.shape[0] // NORM_CHUNK, chunk, 0)

    xn = xn_ref[...]
    if swiglu:
        _swiglu_store(xn, w_refs[0], w_refs[1], o_ref)
    else:
        o_ref[...] = jnp.dot(xn, w_refs[0][...].astype(BF16), preferred_element_type=F32).astype(o_ref.dtype)


def _norm_mm(x, g, sc, sh, mod_row, weights, layer, col_block, n_out, tm, tn, name, cast=()):
    m = x.shape[0]
    swiglu = len(weights) == 2
    n_j = n_out // tn
    steps = (m // tm) * n_j
    w_spec = pl.BlockSpec((None, D_MODEL, tn), lambda i, j: (layer, 0, col_block(j)))
    cast_in_specs, cast_out_specs, cast_shapes = [], [], []
    for a, row_start, rows in cast:
        d = _cast_block_rows(rows, steps)
        assert row_start % d == 0
        blk = functools.partial(lambda i, j, nb, b0: (b0 + jnp.minimum(i * n_j + j, nb - 1), 0), nb=rows // d)
        cast_in_specs.append(pl.BlockSpec((d, a.shape[1]), functools.partial(blk, b0=row_start // d)))
        cast_out_specs.append(pl.BlockSpec((d, a.shape[1]), functools.partial(blk, b0=0)))
        cast_shapes.append(jax.ShapeDtypeStruct((rows, a.shape[1]), BF16))
    outs = pl.pallas_call(
        functools.partial(_norm_mm_kernel, swiglu=swiglu, n_cast=len(cast)),
        out_shape=[jax.ShapeDtypeStruct((m, n_out), BF16)] + cast_shapes,
        grid=(m // tm, n_j),
        in_specs=[
            pl.BlockSpec((tm, D_MODEL), lambda i, j: (i, 0)),
            pl.BlockSpec((1, D_MODEL), lambda i, j: (0, 0)),
            pl.BlockSpec((1, 1, D_MODEL), lambda i, j: (mod_row(i, tm), 0, 0)),
            pl.BlockSpec((1, 1, D_MODEL), lambda i, j: (mod_row(i, tm), 0, 0)),
        ] + [w_spec] * len(weights) + cast_in_specs,
        out_specs=[pl.BlockSpec((tm, tn), lambda i, j: (i, j))] + cast_out_specs,
        scratch_shapes=[pltpu.VMEM((tm, D_MODEL), BF16)],
        compiler_params=_cparams(("arbitrary", "arbitrary"), 56),
        name=name,
    )(x, g, sc, sh, *weights, *[a for a, _, _ in cast])
    return (outs[0], list(outs[1:])) if cast else outs[0]


def _res_mm_kernel(x_ref, w_ref, res_ref, gt_ref, o_ref):
    x = x_ref[...]
    for c0 in range(0, o_ref.shape[1], MXU_COLS):
        cs = slice(c0, c0 + MXU_COLS)
        acc = jnp.dot(x, w_ref[:, cs].astype(BF16), preferred_element_type=F32)
        o_ref[:, cs] = res_ref[:, cs] + gt_ref[0, :, cs] * acc


def _res_mm(x, w, layer, res, gt, mod_row, tm, tn, name, weight_stationary=False):
    m, k = x.shape
    n = res.shape[1]
    if weight_stationary:
        ij = lambda a, b: (b, a)
        grid = (n // tn, m // tm)
        w_mode = dict(pipeline_mode=pl.Buffered(1))
    else:
        ij = lambda a, b: (a, b)
        grid = (m // tm, n // tn)
        w_mode = {}
    return pl.pallas_call(
        _res_mm_kernel,
        out_shape=jax.ShapeDtypeStruct((m, n), F32),
        grid=grid,
        in_specs=[
            pl.BlockSpec((tm, k), lambda a, b: (ij(a, b)[0], 0)),
            pl.BlockSpec((None, k, tn), lambda a, b: (layer, 0, ij(a, b)[1]), **w_mode),
            pl.BlockSpec((tm, tn), lambda a, b: ij(a, b)),
            pl.BlockSpec((1, 1, tn), lambda a, b: (mod_row(ij(a, b)[0], tm), 0, ij(a, b)[1])),
        ],
        out_specs=pl.BlockSpec((tm, tn), lambda a, b: ij(a, b)),
        compiler_params=_cparams(("arbitrary", "arbitrary"), 56),
        name=name,
    )(x, w, res, gt)


def _gmlp_kernel(u_ref, v_ref, g_ref, ws_ref, bs_ref, o_ref):
    v = v_ref[...].astype(F32)
    vn = (v * lax.rsqrt(jnp.mean(v * v, axis=-1, keepdims=True) + NORM_EPS) * g_ref[...]).astype(BF16)
    rows = v.shape[0]
    for gi in range(A_GROUPS):
        w = ws_ref[gi].astype(BF16)
        bias = bs_ref[gi]
        cs = slice(gi * LANES, (gi + 1) * LANES)
        for c in range(rows // CHUNK):
            rs = slice(c * CHUNK, (c + 1) * CHUNK)
            s = jnp.dot(w, vn[rs, cs], preferred_element_type=F32) + bias
            o_ref[rs, cs] = (u_ref[rs, cs].astype(F32) * s).astype(o_ref.dtype)


def _gmlp(p, norm_g, ws, bs, layer):
    m = p.shape[0]
    tm = min(m, GMLP_ROWS)
    return pl.pallas_call(
        _gmlp_kernel,
        out_shape=jax.ShapeDtypeStruct((m, A_CH), BF16),
        grid=(m // tm,),
        in_specs=[
            pl.BlockSpec((tm, A_CH), lambda i: (i, OFF_AU // A_CH)),
            pl.BlockSpec((tm, A_CH), lambda i: (i, OFF_AV // A_CH)),
            pl.BlockSpec((1, A_CH), lambda i: (0, 0)),
            pl.BlockSpec((None, A_GROUPS, CHUNK, CHUNK), lambda i: (layer, 0, 0, 0)),
            pl.BlockSpec((None, A_GROUPS, CHUNK, 1), lambda i: (layer, 0, 0, 0)),
        ],
        out_specs=pl.BlockSpec((tm, A_CH), lambda i: (i, 0)),
        compiler_params=_cparams(("arbitrary",), 32),
        name="mixer_a_gmlp",
    )(p, p, norm_g[layer].reshape(1, A_CH), ws, bs.reshape(bs.shape + (1,)))


CONV_ROWS = 1024
HALO = BF16_SUBLANES


def _conv_kernel(b_ref, c_ref, h_ref, cp_ref, hp_ref, cn_ref, hn_ref, w_ref, o_ref, *, tiles_per_seq):
    i = pl.program_id(0)
    z = c_ref[...].astype(F32) * h_ref[...].astype(F32)
    n = z.shape[0]
    first = (i % tiles_per_seq) == 0
    last = (i % tiles_per_seq) == tiles_per_seq - 1
    zp = jnp.where(first, 0.0, cp_ref[HALO - 1:HALO, :].astype(F32) * hp_ref[HALO - 1:HALO, :].astype(F32))
    zn = jnp.where(last, 0.0, cn_ref[0:1, :].astype(F32) * hn_ref[0:1, :].astype(F32))
    pos = lax.broadcasted_iota(jnp.int32, z.shape, 0)
    z_prev = jnp.where(pos == 0, zp, pltpu.roll(z, 1, axis=0))
    z_next = jnp.where(pos == n - 1, zn, pltpu.roll(z, n - 1, axis=0))
    w = w_ref[...]
    conv = z_prev * w[0:1] + z * w[1:2] + z_next * w[2:3]
    o_ref[...] = (b_ref[...].astype(F32) * conv).astype(o_ref.dtype)


def _short_conv(p, conv_w8, layer, seq_len):
    m = p.shape[0]
    tr = min(seq_len, CONV_ROWS)
    per = tr // HALO
    n_halo = m // HALO

    def prev_blk(col):
        return lambda i: (jnp.maximum(i * per - 1, 0), col)

    def next_blk(col):
        return lambda i: (jnp.minimum((i + 1) * per, n_halo - 1), col)

    cc, ch = OFF_CC // C_W, OFF_CH // C_W
    return pl.pallas_call(
        functools.partial(_conv_kernel, tiles_per_seq=seq_len // tr),
        out_shape=jax.ShapeDtypeStruct((m, C_W), BF16),
        grid=(m // tr,),
        in_specs=[
            pl.BlockSpec((tr, C_W), lambda i: (i, OFF_CB // C_W)),
            pl.BlockSpec((tr, C_W), lambda i: (i, cc)),
            pl.BlockSpec((tr, C_W), lambda i: (i, ch)),
            pl.BlockSpec((HALO, C_W), prev_blk(cc)),
            pl.BlockSpec((HALO, C_W), prev_blk(ch)),
            pl.BlockSpec((HALO, C_W), next_blk(cc)),
            pl.BlockSpec((HALO, C_W), next_blk(ch)),
            pl.BlockSpec((None, 8, C_W), lambda i: (layer, 0, 0)),
        ],
        out_specs=pl.BlockSpec((tr, C_W), lambda i: (i, 0)),
        compiler_params=_cparams(("arbitrary",), 32),
        name="mixer_c_conv",
    )(p, p, p, p, p, p, p, conv_w8)


def _rope(x, cos, sin_lo, sin_hi):
    x = x.astype(F32)
    return x * cos + pltpu.roll(x, LANES - 32, axis=1) * sin_lo + pltpu.roll(x, 32, axis=1) * sin_hi


def _softmax_av(s_parts, v_parts, extra_logit=None):
    m = s_parts[0].max(axis=-1, keepdims=True)
    for s in s_parts[1:]:
        m = jnp.maximum(m, s.max(axis=-1, keepdims=True))
    if extra_logit is not None:
        m = jnp.maximum(m, extra_logit)
    denom = jnp.exp(extra_logit - m) if extra_logit is not None else 0.0
    acc = None
    for s, v in zip(s_parts, v_parts):
        e = jnp.exp(s - m)
        denom = denom + e.sum(axis=-1, keepdims=True)
        pv = jnp.dot(e.astype(BF16), v, preferred_element_type=F32)
        acc = pv if acc is None else acc + pv
    return acc / denom


def _qk(q, k):
    return lax.dot_general(q, k, (((1,), (1,)), ((), ())), preferred_element_type=F32)


def _sink_column(sink_ref, h, rows_per_head):
    rid = lax.broadcasted_iota(jnp.int32, (B_GROUP * rows_per_head, 1), 0) // rows_per_head
    col = jnp.zeros((B_GROUP * rows_per_head, 1), F32)
    for g in range(B_GROUP):
        col = jnp.where(rid == g, sink_ref[h * B_GROUP + g], col)
    return col


def _win_attn_kernel(sink_ref, q_ref, k_ref, v_ref, kc_ref, vc_ref, cos_ref, slo_ref, shi_ref, mask_ref, o_ref):
    n = pl.program_id(1)
    blk = B_WINDOW
    band = 3 * blk
    kblk = jnp.clip(n - 1, 0, SEQ // blk - 3)
    q0 = pl.multiple_of(n * blk, blk)
    k0 = pl.multiple_of(kblk * blk, blk)
    scale = HEAD_DIM ** -0.5

    cos_q, slo_q, shi_q = cos_ref[pl.ds(q0, blk), :], slo_ref[pl.ds(q0, blk), :], shi_ref[pl.ds(q0, blk), :]
    cos_k, slo_k, shi_k = cos_ref[pl.ds(k0, band), :], slo_ref[pl.ds(k0, band), :], shi_ref[pl.ds(k0, band), :]
    window_bias = mask_ref[n - kblk]

    for h in range(B_KV_HEADS):
        hs = slice(h * HEAD_DIM, (h + 1) * HEAD_DIM)
        k_loc = _rope(k_ref[pl.ds(k0, band), hs], cos_k, slo_k, shi_k).astype(BF16)
        v_loc = v_ref[pl.ds(k0, band), hs]
        k_ctx = kc_ref[:, hs]
        v_ctx = vc_ref[:, hs]
        qs = []
        for g in range(B_GROUP):
            c0 = (h * B_GROUP + g) * HEAD_DIM
            qs.append(_rope(q_ref[:, c0:c0 + HEAD_DIM], cos_q, slo_q, shi_q).astype(BF16))
        qst = jnp.concatenate(qs, axis=0)
        s_loc = _qk(qst, k_loc) * scale + window_bias
        s_ctx = _qk(qst, k_ctx) * scale
        out = _softmax_av([s_ctx, s_loc], [v_ctx, v_loc], _sink_column(sink_ref, h, blk))
        for g in range(B_GROUP):
            c0 = (h * B_GROUP + g) * HEAD_DIM
            o_ref[:, c0:c0 + HEAD_DIM] = out[g * blk:(g + 1) * blk].astype(o_ref.dtype)


def _window_bias_table():
    i = np.arange(B_WINDOW)[:, None]
    j = np.arange(3 * B_WINDOW)[None, :]
    tabs = []
    for c in range(3):
        rel = j - c * B_WINDOW - i
        one = np.where(np.abs(rel) <= B_WINDOW, 0.0, NEG_INF).astype(np.float32)
        tabs.append(np.tile(one, (B_GROUP, 1)))
    return jnp.asarray(np.stack(tabs))


def _win_attn(p, pc, kc_blk, vc_blk, sink, rope_tabs, batch):
    nb = SEQ // B_WINDOW
    kvw = B_KV_W
    mask_tab = _window_bias_table()
    return pl.pallas_call(
        _win_attn_kernel,
        out_shape=jax.ShapeDtypeStruct((batch * SEQ, B_Q_W), BF16),
        grid=(batch, nb),
        in_specs=[
            pl.BlockSpec(memory_space=pltpu.SMEM),
            pl.BlockSpec((B_WINDOW, B_Q_W), lambda b, n: (b * nb + n, OFF_BQ // B_Q_W)),
            pl.BlockSpec((SEQ, kvw), lambda b, n: (b, OFF_BK // kvw)),
            pl.BlockSpec((SEQ, kvw), lambda b, n: (b, OFF_BV // kvw)),
            pl.BlockSpec((CTX_LEN, kvw), lambda b, n: (b, kc_blk)),
            pl.BlockSpec((CTX_LEN, kvw), lambda b, n: (b, vc_blk)),
            pl.BlockSpec((SEQ, HEAD_DIM), lambda b, n: (0, 0)),
            pl.BlockSpec((SEQ, HEAD_DIM), lambda b, n: (0, 0)),
            pl.BlockSpec((SEQ, HEAD_DIM), lambda b, n: (0, 0)),
            pl.BlockSpec(mask_tab.shape, lambda b, n: (0, 0, 0)),
        ],
        out_specs=pl.BlockSpec((B_WINDOW, B_Q_W), lambda b, n: (b * nb + n, 0)),
        compiler_params=_cparams(("arbitrary", "arbitrary"), 40),
        name="mixer_b_window_attn",
    )(sink, p, p, p, pc, pc, *rope_tabs, mask_tab)


def _ctx_gqa_kernel(sink_ref, q_ref, kc_ref, vc_ref, o_ref):
    scale = HEAD_DIM ** -0.5
    lc = q_ref.shape[0]
    for h in range(B_KV_HEADS):
        hs = slice(h * HEAD_DIM, (h + 1) * HEAD_DIM)
        qst = jnp.concatenate(
            [q_ref[:, (h * B_GROUP + g) * HEAD_DIM:(h * B_GROUP + g + 1) * HEAD_DIM] for g in range(B_GROUP)],
            axis=0)
        out = _softmax_av([_qk(qst, kc_ref[:, hs]) * scale], [vc_ref[:, hs]], _sink_column(sink_ref, h, lc))
        for g in range(B_GROUP):
            c0 = (h * B_GROUP + g) * HEAD_DIM
            o_ref[:, c0:c0 + HEAD_DIM] = out[g * lc:(g + 1) * lc].astype(o_ref.dtype)


def _ctx_gqa(pc, sink, batch):
    kvw = B_KV_W
    return pl.pallas_call(
        _ctx_gqa_kernel,
        out_shape=jax.ShapeDtypeStruct((batch * CTX_LEN, B_Q_W), BF16),
        grid=(batch,),
        in_specs=[
            pl.BlockSpec(memory_space=pltpu.SMEM),
            pl.BlockSpec((CTX_LEN, B_Q_W), lambda b: (b, OFF_BQ // B_Q_W)),
            pl.BlockSpec((CTX_LEN, kvw), lambda b: (b, OFF_BK // kvw)),
            pl.BlockSpec((CTX_LEN, kvw), lambda b: (b, OFF_BV // kvw)),
        ],
        out_specs=pl.BlockSpec((CTX_LEN, B_Q_W), lambda b: (b, 0)),
        compiler_params=_cparams(("arbitrary",), 32),
        name="mixer_b_ctx_attn",
    )(sink, pc, pc, pc)


NA_ROWS_PER_STEP = 4
GRID_H = SEQ // GRID_W
NA_LOC = NA_KH * GRID_W


def _na_kernel(q_ref, k_ref, v_ref, kc_ref, vc_ref, bias_ref, o_ref):
    step = pl.program_id(1)
    scale = HEAD_DIM ** -0.5
    blocks, s_loc, s_ctx = [], [], []
    for rr in range(NA_ROWS_PER_STEP):
        r = step * NA_ROWS_PER_STEP + rr
        rs = jnp.clip(r - NA_KH // 2, 0, GRID_H - NA_KH)
        k0 = pl.multiple_of(rs * GRID_W, GRID_W)
        d = r - rs
        qrows = slice(rr * GRID_W, (rr + 1) * GRID_W)
        for h in range(D_HEADS):
            hs = slice(h * HEAD_DIM, (h + 1) * HEAD_DIM)
            q = q_ref[qrows, hs]
            s_loc.append(_qk(q, k_ref[pl.ds(k0, NA_LOC), hs]) * scale + bias_ref[h, d])
            s_ctx.append(_qk(q, kc_ref[:, hs]) * scale)
            blocks.append((qrows, hs, k0))
    s_loc = jnp.concatenate(s_loc, axis=0)
    s_ctx = jnp.concatenate(s_ctx, axis=0)
    m = jnp.maximum(s_loc.max(axis=-1, keepdims=True), s_ctx.max(axis=-1, keepdims=True))
    e_loc = jnp.exp(s_loc - m)
    e_ctx = jnp.exp(s_ctx - m)
    denom = e_loc.sum(axis=-1, keepdims=True) + e_ctx.sum(axis=-1, keepdims=True)
    e_loc = e_loc.astype(BF16)
    e_ctx = e_ctx.astype(BF16)
    for i, (qrows, hs, k0) in enumerate(blocks):
        rows = slice(i * GRID_W, (i + 1) * GRID_W)
        acc = jnp.dot(e_ctx[rows], vc_ref[:, hs], preferred_element_type=F32)
        acc = acc + jnp.dot(e_loc[rows], v_ref[pl.ds(k0, NA_LOC), hs], preferred_element_type=F32)
        o_ref[qrows, hs] = (acc / denom[rows]).astype(o_ref.dtype)


def _na_bias_table(rpb):
    rpb = rpb.astype(F32)
    rows = []
    for c in range(GRID_W):
        cstart = min(max(c - NA_KW // 2, 0), GRID_W - NA_KW)
        first_dc = cstart - c + NA_KW - 1
        win = rpb[:, :, first_dc:first_dc + NA_KW]
        rows.append(jnp.pad(win, ((0, 0), (0, 0), (cstart, GRID_W - NA_KW - cstart)), constant_values=NEG_INF))
    tab0 = jnp.stack(rows, axis=2)
    slabs = []
    for d in range(NA_KH):
        sl = tab0[:, NA_KH - 1 - d:2 * NA_KH - 1 - d]
        slabs.append(jnp.transpose(sl, (0, 2, 1, 3)).reshape(D_HEADS, GRID_W, NA_LOC))
    return jnp.stack(slabs, axis=1)


def _na_attn(p, pc, kc_blk, vc_blk, bias_tab, batch):
    rows_q = NA_ROWS_PER_STEP * GRID_W
    steps = SEQ // rows_q
    return pl.pallas_call(
        _na_kernel,
        out_shape=jax.ShapeDtypeStruct((batch * SEQ, D_W), BF16),
        grid=(batch, steps),
        in_specs=[
            pl.BlockSpec((rows_q, D_W), lambda b, s: (b * steps + s, OFF_DQ // D_W)),
            pl.BlockSpec((SEQ, D_W), lambda b, s: (b, OFF_DK // D_W)),
            pl.BlockSpec((SEQ, D_W), lambda b, s: (b, OFF_DV // D_W)),
            pl.BlockSpec((CTX_LEN, D_W), lambda b, s: (b, kc_blk)),
            pl.BlockSpec((CTX_LEN, D_W), lambda b, s: (b, vc_blk)),
            pl.BlockSpec((D_HEADS, NA_KH, GRID_W, NA_LOC), lambda b, s: (0, 0, 0, 0)),
        ],
        out_specs=pl.BlockSpec((rows_q, D_W), lambda b, s: (b * steps + s, 0)),
        compiler_params=_cparams(("arbitrary", "arbitrary"), 48),
        name="mixer_d_neighborhood_attn",
    )(p, p, p, pc, pc, bias_tab)


def _ctx_mha_kernel(q_ref, kc_ref, vc_ref, o_ref):
    scale = HEAD_DIM ** -0.5
    for h in range(D_HEADS):
        hs = slice(h * HEAD_DIM, (h + 1) * HEAD_DIM)
        s = _qk(q_ref[:, hs], kc_ref[:, hs]) * scale
        o_ref[:, hs] = _softmax_av([s], [vc_ref[:, hs]]).astype(o_ref.dtype)


def _ctx_mha(pc, batch):
    return pl.pallas_call(
        _ctx_mha_kernel,
        out_shape=jax.ShapeDtypeStruct((batch * CTX_LEN, D_W), BF16),
        grid=(batch,),
        in_specs=[
            pl.BlockSpec((CTX_LEN, D_W), lambda b: (b, OFF_DQ // D_W)),
            pl.BlockSpec((CTX_LEN, D_W), lambda b: (b, OFF_DK // D_W)),
            pl.BlockSpec((CTX_LEN, D_W), lambda b: (b, OFF_DV // D_W)),
        ],
        out_specs=pl.BlockSpec((CTX_LEN, D_W), lambda b: (b, 0)),
        compiler_params=_cparams(("arbitrary",), 32),
        name="mixer_d_ctx_attn",
    )(pc, pc, pc)


BR_W = (A_CH, B_Q_W, C_W, D_W)
BR_OFF = (0, A_CH, A_CH + B_Q_W, A_CH + B_Q_W + C_W)


def _merge_kernel(ya_ref, yb_ref, yc_ref, yd_ref, w_ref, g0_ref, g1_ref, g2_ref, g3_ref, o_ref):
    z = None
    for y_ref, g_ref, off, width in zip((ya_ref, yb_ref, yc_ref, yd_ref),
                                        (g0_ref, g1_ref, g2_ref, g3_ref), BR_OFF, BR_W):
        t = jnp.dot(y_ref[...], w_ref[off:off + width, :].astype(BF16), preferred_element_type=F32)
        t = jax.nn.sigmoid(g_ref[...].astype(F32)) * t
        z = t if z is None else z + t
    o_ref[...] = z.astype(o_ref.dtype)


def _merge(p, ys, w_branch, layer):
    m = p.shape[0]
    tm, tn = TM_MERGE, TN_MERGE
    gate_specs = [pl.BlockSpec((tm, tn), functools.partial(
        lambda i, j, b: (i, (OFF_G + b * D_MODEL) // tn + j), b=b)) for b in range(4)]
    return pl.pallas_call(
        _merge_kernel,
        out_shape=jax.ShapeDtypeStruct((m, D_MODEL), BF16),
        grid=(m // tm, D_MODEL // tn),
        in_specs=[pl.BlockSpec((tm, w), lambda i, j: (i, 0)) for w in BR_W]
        + [pl.BlockSpec((None, MIX_W, tn), lambda i, j: (layer, 0, j))] + gate_specs,
        out_specs=pl.BlockSpec((tm, tn), lambda i, j: (i, j)),
        compiler_params=_cparams(("arbitrary", "arbitrary"), 48),
        name="branch_merge",
    )(*ys, w_branch, p, p, p, p)


ROUTE_LANES = LANES
LANE_IDX = N_EXPERTS
LANE_WT = N_EXPERTS + TOP_K


def _router_kernel(x_ref, g_ref, sc_ref, sh_ref, wr_ref, xn_ref, route_ref):
    xn = _norm_mod(x_ref[...], g_ref[...], sc_ref[0], sh_ref[0])
    xn_ref[...] = xn.astype(BF16).reshape(xn_ref.shape)
    x_hi = xn.astype(BF16)
    x_lo = (xn - x_hi.astype(F32)).astype(BF16)
    wr = wr_ref[...]
    w_hi = wr.astype(BF16)
    w_lo = (wr - w_hi.astype(F32)).astype(BF16)
    logits = (jnp.dot(x_hi, w_hi, preferred_element_type=F32)
              + (jnp.dot(x_hi, w_lo, preferred_element_type=F32) + jnp.dot(x_lo, w_hi, preferred_element_type=F32)))
    lane = lax.broadcasted_iota(jnp.int32, logits.shape, 1).astype(F32)
    logits = jnp.where(lane < N_EXPERTS, logits, -jnp.inf)
    m1 = logits.max(axis=-1, keepdims=True)
    i1 = jnp.where(logits == m1, lane, float(ROUTE_LANES)).min(axis=-1, keepdims=True)
    rest = jnp.where(lane == i1, -jnp.inf, logits)
    m2 = rest.max(axis=-1, keepdims=True)
    i2 = jnp.where(rest == m2, lane, float(ROUTE_LANES)).min(axis=-1, keepdims=True)
    e2 = jnp.exp(m2 - m1)
    w1 = 1.0 / (1.0 + e2)
    w2 = e2 / (1.0 + e2)
    out = jnp.where(lane == LANE_IDX, i1, 0.0)
    out = jnp.where(lane == LANE_IDX + 1, i2, out)
    out = jnp.where(lane == LANE_WT, w1, out)
    out = jnp.where(lane == LANE_WT + 1, w2, out)
    route_ref[...] = out


def _router(h, g, sc, sh, mod_row, w_router_pad):
    t = h.shape[0]
    tm = 512
    return pl.pallas_call(
        _router_kernel,
        out_shape=(jax.ShapeDtypeStruct((t, D_MODEL // LANES, LANES), BF16),
                   jax.ShapeDtypeStruct((t, ROUTE_LANES), F32)),
        grid=(t // tm,),
        in_specs=[
            pl.BlockSpec((tm, D_MODEL), lambda i: (i, 0)),
            pl.BlockSpec((1, D_MODEL), lambda i: (0, 0)),
            pl.BlockSpec((1, 1, D_MODEL), lambda i: (mod_row(i, tm), 0, 0)),
            pl.BlockSpec((1, 1, D_MODEL), lambda i: (mod_row(i, tm), 0, 0)),
            pl.BlockSpec((D_MODEL, ROUTE_LANES), lambda i: (0, 0)),
        ],
        out_specs=(pl.BlockSpec((tm, D_MODEL // LANES, LANES), lambda i: (i, 0, 0)),
                   pl.BlockSpec((tm, ROUTE_LANES), lambda i: (i, 0))),
        compiler_params=_cparams(("arbitrary",), 40),
        name="moe_router",
    )(h, g, sc, sh, w_router_pad)


def _gather_rows_kernel(idx_ref, src_ref, o_ref, *scratch, n_sources, flat):
    sem = scratch[-1]
    dst_ref = scratch[0] if flat else o_ref
    rows = dst_ref.shape[-3]

    def row_copy(s, r, src_row):
        return pltpu.make_async_copy(src_ref.at[src_row], dst_ref.at[s, r], sem)

    def issue(i, carry):
        for u in range(GATHER_UNROLL):
            r = i * GATHER_UNROLL + u
            for s in range(n_sources):
                row_copy(s, r, idx_ref[0, s, r]).start(priority=(u * n_sources + s) % 2)
        return carry

    lax.fori_loop(0, rows // GATHER_UNROLL, issue, 0)

    def drain(i, carry):
        for u in range(GATHER_UNROLL):
            for s in range(n_sources):
                row_copy(s, i * GATHER_UNROLL + u, 0).wait()
        return carry

    lax.fori_loop(0, rows // GATHER_UNROLL, drain, 0)
    if flat:
        o_ref[...] = dst_ref[0].reshape(o_ref.shape)


def _gather_rows(src3, idx, n_sources, flat=False):
    _, s_dim, lanes = src3.shape
    r = idx.shape[1]
    tg = TG_ROWS
    idx_blocks = idx.reshape(n_sources, r // tg, tg).transpose(1, 0, 2)
    if flat:
        assert n_sources == 1
        out_shape = jax.ShapeDtypeStruct((r, s_dim * lanes), src3.dtype)
        out_spec = pl.BlockSpec((tg, s_dim * lanes), lambda i: (i, 0))
        scratch = [pltpu.VMEM((1, tg, s_dim, lanes), src3.dtype), pltpu.SemaphoreType.DMA]
    else:
        out_shape = jax.ShapeDtypeStruct((n_sources, r, s_dim, lanes), src3.dtype)
        out_spec = pl.BlockSpec((n_sources, tg, s_dim, lanes), lambda i: (0, i, 0, 0))
        scratch = [pltpu.SemaphoreType.DMA]
    return pl.pallas_call(
        functools.partial(_gather_rows_kernel, n_sources=n_sources, flat=flat),
        out_shape=out_shape,
        grid=(r // tg,),
        in_specs=[
            pl.BlockSpec((1, n_sources, tg), lambda i: (i, 0, 0), memory_space=pltpu.SMEM),
            pl.BlockSpec(memory_space=pl.ANY),
        ],
        out_specs=out_spec,
        scratch_shapes=scratch,
        compiler_params=_cparams(("arbitrary",), 32),
        name="row_gather",
    )(idx_blocks, src3)


def _moe_up_kernel(te_ref, nu_ref, x_ref, w1_ref, w3_ref, w2_ref, o_ref, w2b_ref):
    used = pl.program_id(1) < nu_ref[0]
    w2b_ref[...] = w2_ref[...].astype(BF16)

    @pl.when(used)
    def _():
        _swiglu_store(x_ref[...], w1_ref, w3_ref, o_ref)

    @pl.when(jnp.logical_not(used))
    def _():
        o_ref[...] = jnp.zeros_like(o_ref)


def _moe_up(xs, w1, w3, w2, layer, tile_expert, n_used):
    r = xs.shape[0]
    tm, tn = TM_MOE, TN_MOE_UP
    n_t = r // tm
    w2_rows = N_EXPERTS * FFN_DIM
    steps = (FFN_DIM // tn) * n_t
    cast_rows = _cast_block_rows(w2_rows, steps)
    cast_blocks = w2_rows // cast_rows
    w2_flat = w2.reshape(-1, D_MODEL)

    def cast_blk(j, t, te, nu):
        return (layer * cast_blocks + jnp.minimum(j * n_t + t, cast_blocks - 1), 0)

    def cast_out_blk(j, t, te, nu):
        return (jnp.minimum(j * n_t + t, cast_blocks - 1), 0)

    w_spec = pl.BlockSpec((None, None, D_MODEL, tn), lambda j, t, te, nu: (layer, te[t], 0, j))
    grid_spec = pltpu.PrefetchScalarGridSpec(
        num_scalar_prefetch=2,
        grid=(FFN_DIM // tn, n_t),
        in_specs=[pl.BlockSpec((tm, D_MODEL), lambda j, t, te, nu: (jnp.minimum(t, nu[0] - 1), 0)),
                  w_spec, w_spec,
                  pl.BlockSpec((cast_rows, D_MODEL), cast_blk)],
        out_specs=(pl.BlockSpec((tm, tn), lambda j, t, te, nu: (t, j)),
                   pl.BlockSpec((cast_rows, D_MODEL), cast_out_blk)),
    )
    hm, w2b = pl.pallas_call(
        _moe_up_kernel,
        out_shape=(jax.ShapeDtypeStruct((r, FFN_DIM), BF16),
                   jax.ShapeDtypeStruct((w2_rows, D_MODEL), BF16)),
        grid_spec=grid_spec,
        compiler_params=_cparams(("arbitrary", "arbitrary"), 56),
        name="moe_up",
    )(tile_expert, n_used, xs, w1, w3, w2_flat)
    return hm, w2b.reshape(1, N_EXPERTS, FFN_DIM, D_MODEL)


def _moe_down_kernel(te_ref, nu_ref, x_ref, w_ref, o_ref):
    used = pl.program_id(1) < nu_ref[0]

    @pl.when(used)
    def _():
        x = x_ref[...]
        for c0 in range(0, o_ref.shape[1], MXU_COLS):
            cs = slice(c0, c0 + MXU_COLS)
            o_ref[:, cs] = jnp.dot(x, w_ref[:, cs], preferred_element_type=F32).astype(o_ref.dtype)

    @pl.when(jnp.logical_not(used))
    def _():
        o_ref[...] = jnp.zeros_like(o_ref)


def _moe_down(hmid, w2, layer, tile_expert, n_used):
    r = hmid.shape[0]
    tm, tn = TM_MOE, TN_MOE_DOWN
    grid_spec = pltpu.PrefetchScalarGridSpec(
        num_scalar_prefetch=2,
        grid=(D_MODEL // tn, r // tm),
        in_specs=[
            pl.BlockSpec((tm, FFN_DIM), lambda j, t, te, nu: (jnp.minimum(t, nu[0] - 1), 0)),
            pl.BlockSpec((None, None, FFN_DIM, tn), lambda j, t, te, nu: (layer, te[t], 0, j)),
        ],
        out_specs=pl.BlockSpec((tm, tn), lambda j, t, te, nu: (t, j)),
    )
    return pl.pallas_call(
        _moe_down_kernel,
        out_shape=jax.ShapeDtypeStruct((r, D_MODEL), BF16),
        grid_spec=grid_spec,
        compiler_params=_cparams(("arbitrary", "arbitrary"), 60),
        name="moe_down",
    )(tile_expert, n_used, hmid, w2)


def _route_plan(route, tm):
    t = route.shape[0]
    idx = route[:, LANE_IDX:LANE_IDX + TOP_K].astype(jnp.int32)
    e_flat = idx.T.reshape(-1)
    onehot = (e_flat[:, None] == jnp.arange(N_EXPERTS)[None, :]).astype(jnp.int32)
    counts = onehot.sum(axis=0)
    rank = (onehot * (jnp.cumsum(onehot, axis=0) - onehot)).sum(axis=1)
    tiles_e = (counts + tm - 1) // tm
    tile_end = jnp.cumsum(tiles_e)
    tile_start = tile_end - tiles_e
    pos = (onehot * tile_start[None, :]).sum(axis=1) * tm + rank
    n_tiles = TOP_K * t // tm + N_EXPERTS
    n_used = tile_end[-1]
    tile_ids = jnp.minimum(jnp.arange(n_tiles), n_used - 1)
    tile_expert = jnp.minimum((tile_ids[:, None] >= tile_end[None, :]).sum(axis=1), N_EXPERTS - 1)
    row_token = (jnp.arange(n_tiles * tm, dtype=jnp.int32) % t).at[pos].set(
        jnp.tile(jnp.arange(t, dtype=jnp.int32), TOP_K))
    return (pos.reshape(TOP_K, t).astype(jnp.int32), row_token,
            tile_expert.astype(jnp.int32), n_used.reshape(1).astype(jnp.int32))


FINAL_ROWS = 512


def _final_kernel(idx_ref, idx_next_ref, y_ref, h_ref, route_ref, gt_ref, g_ref, o_ref, buf, sems):
    i = pl.program_id(0)
    half = FINAL_ROWS // 2

    def row_copy(slot, k, r, src_row):
        return pltpu.make_async_copy(y_ref.at[src_row], buf.at[slot, k, r], sems.at[slot])

    def start_half(ids_ref, slot, first_row):
        def body(g, carry):
            for u in range(GATHER_UNROLL):
                r = g * GATHER_UNROLL + u
                for k in range(TOP_K):
                    row_copy(slot, k, r, ids_ref[0, k, first_row + r]).start(priority=(u * TOP_K + k) % 2)
            return carry
        lax.fori_loop(0, half // GATHER_UNROLL, body, 0)

    def wait_half(slot):
        def body(g, carry):
            for u in range(GATHER_UNROLL):
                for k in range(TOP_K):
                    row_copy(slot, k, g * GATHER_UNROLL + u, 0).wait()
            return carry
        lax.fori_loop(0, half // GATHER_UNROLL, body, 0)

    def combine_half(slot, first_row):
        rows = slice(first_row, first_row + half)
        w1 = route_ref[rows, LANE_WT:LANE_WT + 1]
        w2 = route_ref[rows, LANE_WT + 1:LANE_WT + 2]
        y0 = buf[slot, 0].reshape(half, D_MODEL).astype(F32)
        y1 = buf[slot, 1].reshape(half, D_MODEL).astype(F32)
        h = h_ref[rows, :] + gt_ref[0] * (w1 * y0 + w2 * y1)
        o_ref[rows, :] = h * lax.rsqrt(jnp.mean(h * h, axis=-1, keepdims=True) + NORM_EPS) * g_ref[...]

    @pl.when(i == 0)
    def _():
        start_half(idx_ref, 0, 0)

    start_half(idx_ref, 1, half)
    wait_half(0)
    combine_half(0, 0)

    @pl.when(i + 1 < pl.num_programs(0))
    def _():
        start_half(idx_next_ref, 0, 0)

    wait_half(1)
    combine_half(1, half)


def _final(h, ye3, pos, route, gt, mod_row, g_final):
    t = h.shape[0]
    tm = FINAL_ROWS
    n = t // tm
    _, s_dim, lanes = ye3.shape
    idx_blocks = pos.reshape(TOP_K, n, tm).transpose(1, 0, 2)
    return pl.pallas_call(
        _final_kernel,
        out_shape=jax.ShapeDtypeStruct((t, D_MODEL), F32),
        grid=(n,),
        in_specs=[
            pl.BlockSpec((1, TOP_K, tm), lambda i: (i, 0, 0), memory_space=pltpu.SMEM),
            pl.BlockSpec((1, TOP_K, tm), lambda i: (jnp.minimum(i + 1, n - 1), 0, 0), memory_space=pltpu.SMEM),
            pl.BlockSpec(memory_space=pl.ANY),
            pl.BlockSpec((tm, D_MODEL), lambda i: (i, 0)),
            pl.BlockSpec((tm, ROUTE_LANES), lambda i: (i, 0)),
            pl.BlockSpec((1, 1, D_MODEL), lambda i: (mod_row(i, tm), 0, 0)),
            pl.BlockSpec((1, D_MODEL), lambda i: (0, 0)),
        ],
        out_specs=pl.BlockSpec((tm, D_MODEL), lambda i: (i, 0)),
        scratch_shapes=[pltpu.VMEM((2, TOP_K, tm // 2, s_dim, lanes), ye3.dtype),
                        pltpu.SemaphoreType.DMA((2,))],
        compiler_params=_cparams(("arbitrary",), 48),
        name="final_combine_norm",
    )(idx_blocks, idx_blocks, ye3, h, route, gt, g_final)


def _rope_tables():
    pos = jnp.arange(SEQ)
    f = HEAD_DIM // 4
    inv = ROPE_BASE ** (-jnp.arange(f, dtype=F32) / f)
    ang_row = (pos // GRID_W).astype(F32)[:, None] * inv[None, :]
    ang_col = (pos % GRID_W).astype(F32)[:, None] * inv[None, :]
    ang = jnp.concatenate([ang_row, ang_row, ang_col, ang_col], axis=1)
    cos, sin = jnp.cos(ang), jnp.sin(ang)
    first = jnp.asarray(((np.arange(HEAD_DIM) // f) % 2 == 0)[None, :])
    return cos, jnp.where(first, -sin, 0.0), jnp.where(first, 0.0, sin)


def kernel(x, c, ctx, c_ctx, w_ada, b_ada, g_mix, g_ffn, w_in, a_norm_g, a_ws, a_bs, b_sink, c_conv,
           d_rpb, w_branch, w_out, ffn_w1, ffn_w3, ffn_w2, w_router, moe_w1, moe_w3, moe_w2, g_final):
    batch, seq, _ = x.shape
    depth = w_in.shape[0]
    assert seq == SEQ and ctx.shape[1] == CTX_LEN and batch + 1 <= MOD_ROWS
    assert depth % 2 == 0, "the fused residual + final-norm epilogue lives in the MoE (odd, last) layer"
    t = batch * seq
    tc = batch * CTX_LEN

    cond = jnp.zeros((MOD_ROWS, D_MODEL), F32).at[:batch].set(c).at[batch].set(c_ctx)
    mods = _ada(cond, w_ada, b_ada)
    rope_tabs = _rope_tables()
    conv_w8 = jnp.zeros((depth, 8, C_W), F32).at[:, :3].set(c_conv)

    mix_w = [w[0:1].astype(BF16) for w in (w_in, w_branch, w_out)]

    def lat_row(i, tm):
        return (i * tm) // SEQ

    def ctx_row(i, tm):
        return batch

    h = x.reshape(t, D_MODEL)
    hc = ctx.reshape(tc, D_MODEL)
    out = None
    for layer in range(depth):
        last = layer == depth - 1
        sh1, sc1, gt1, sh2, sc2, gt2 = [m.reshape(MOD_ROWS, 1, D_MODEL)
                                        for m in jnp.split(mods[layer], 6, axis=-1)]
        g_mix_l = g_mix[layer].reshape(1, D_MODEL)
        g_ffn_l = g_ffn[layer].reshape(1, D_MODEL)
        bias_tab = _na_bias_table(d_rpb[layer])
        w_in_l, w_branch_l, w_out_l = mix_w
        j = layer // 2
        dense = layer % 2 == 0

        if last:
            pc = _norm_mm(hc, g_mix_l, sc1, sh1, ctx_row, (w_in_l,), 0,
                          lambda n: jnp.where(n == 0, OFF_BK // TN_KV, OFF_DK // TN_KV - 1 + n),
                          3 * TN_KV, TM_PROJ, TN_KV, "ctx_kv_proj")
            kcb, vcb, kcd, vcd = 0, 1, 1, 2
        else:
            pc = _norm_mm(hc, g_mix_l, sc1, sh1, ctx_row, (w_in_l,), 0, lambda n: n,
                          IN_W, TM_PROJ, TN_IN, "ctx_in_proj")
            kcb, vcb, kcd, vcd = OFF_BK // B_KV_W, OFF_BV // B_KV_W, OFF_DK // D_W, OFF_DV // D_W

        if dense:
            ffn_cast = [(ffn_w1.reshape(-1, FFN_DIM), j * D_MODEL, D_MODEL),
                        (ffn_w3.reshape(-1, FFN_DIM), j * D_MODEL, D_MODEL),
                        (ffn_w2.reshape(-1, D_MODEL), j * FFN_DIM, FFN_DIM)]
            px, (w1b, w3b, w2b) = _norm_mm(h, g_mix_l, sc1, sh1, lat_row, (w_in_l,), 0, lambda n: n,
                                           IN_W, TM_PROJ, TN_IN, "in_proj", cast=ffn_cast)
            w1b, w3b, w2b = w1b[None], w3b[None], w2b[None]
        else:
            px = _norm_mm(h, g_mix_l, sc1, sh1, lat_row, (w_in_l,), 0, lambda n: n,
                          IN_W, TM_PROJ, TN_IN, "in_proj")
        ys = (
            _gmlp(px, a_norm_g, a_ws, a_bs, layer),
            _win_attn(px, pc, kcb, vcb, b_sink[layer], rope_tabs, batch),
            _short_conv(px, conv_w8, layer, SEQ),
            _na_attn(px, pc, kcd, vcd, bias_tab, batch),
        )
        z = _merge(px, ys, w_branch_l, 0)
        h = _res_mm(z, w_out_l, 0, h, gt1, lat_row, TM_OUT, TN_OUT, "out_proj")

        if not last:
            ysc = (
                _gmlp(pc, a_norm_g, a_ws, a_bs, layer),
                _ctx_gqa(pc, b_sink[layer], batch),
                _short_conv(pc, conv_w8, layer, CTX_LEN),
                _ctx_mha(pc, batch),
            )
            zc = _merge(pc, ysc, w_branch_l, 0)
            hc = _res_mm(zc, w_out_l, 0, hc, gt1, ctx_row, TM_OUT, TN_OUT, "ctx_out_proj")

        if dense:
            if last:
                hm = _norm_mm(h, g_ffn_l, sc2, sh2, lat_row, (w1b, w3b), 0, lambda n: n,
                              FFN_DIM, TM_PROJ, TN_FFN, "ffn_up")
            else:
                nxt = layer + 1
                mix_cast = [(w_in.reshape(-1, IN_W), nxt * D_MODEL, D_MODEL),
                            (w_branch.reshape(-1, D_MODEL), nxt * MIX_W, MIX_W),
                            (w_out.reshape(-1, D_MODEL), nxt * D_MODEL, D_MODEL)]
                hm, mix_w = _norm_mm(h, g_ffn_l, sc2, sh2, lat_row, (w1b, w3b), 0, lambda n: n,
                                     FFN_DIM, TM_PROJ, TN_FFN, "ffn_up", cast=mix_cast)
                mix_w = [w[None] for w in mix_w]
            h = _res_mm(hm, w2b, 0, h, gt2, lat_row, TM_DOWN, TN_DOWN, "ffn_down", weight_stationary=True)
            if not last:
                hmc = _norm_mm(hc, g_ffn_l, sc2, sh2, ctx_row, (w1b, w3b), 0, lambda n: n,
                               FFN_DIM, TM_PROJ, TN_FFN, "ctx_ffn_up")
                hc = _res_mm(hmc, w2b, 0, hc, gt2, ctx_row, TM_DOWN, TN_DOWN, "ctx_ffn_down",
                             weight_stationary=True)
        else:
            assert last, "the context MoE path would only be needed for an odd layer that is not the last"
            wr_pad = jnp.zeros((D_MODEL, ROUTE_LANES), F32).at[:, :N_EXPERTS].set(w_router[j])
            xn, route = _router(h, g_ffn_l, sc2, sh2, lat_row, wr_pad)
            pos, row_token, tile_expert, n_used = _route_plan(route, TM_MOE)
            xs = _gather_rows(xn, row_token[None, :], 1, flat=True)
            hm, w2b = _moe_up(xs, moe_w1, moe_w3, moe_w2, j, tile_expert, n_used)
            ye = _moe_down(hm, w2b, 0, tile_expert, n_used)
            out = _final(h, ye.reshape(-1, D_MODEL // LANES, LANES), pos, route, gt2, lat_row,
                         g_final.reshape(1, D_MODEL))

    return out.reshape(batch, seq, D_MODEL)
```

```python
import functools

import jax
import jax.numpy as jnp
import numpy as np
from jax import lax
from jax.experimental import pallas as pl
from jax.experimental.pallas import tpu as pltpu

F32 = jnp.float32
BF16 = jnp.bfloat16

D_MODEL = 2048
SEQ = 2048
CTX_LEN = 256
GRID_W = 64
HEAD_DIM = 128
ROPE_BASE = 10000.0
NORM_EPS = 1e-6
NEG_INF = -1e30

CHUNK = 128
A_GROUPS = 4
A_CH = 512
B_HEADS = 8
B_KV_HEADS = 2
B_GROUP = B_HEADS // B_KV_HEADS
B_WINDOW = 128
B_Q_W = B_HEADS * HEAD_DIM
B_KV_W = B_KV_HEADS * HEAD_DIM
C_W = 512
D_HEADS = 4
D_W = D_HEADS * HEAD_DIM
NA_KH = 8
NA_KW = 16

OFF_AU = 0
OFF_AV = OFF_AU + A_CH
OFF_BQ = OFF_AV + A_CH
OFF_BK = OFF_BQ + B_Q_W
OFF_BV = OFF_BK + B_KV_W
OFF_CB = OFF_BV + B_KV_W
OFF_CC = OFF_CB + C_W
OFF_CH = OFF_CC + C_W
OFF_DQ = OFF_CH + C_W
OFF_DK = OFF_DQ + D_W
OFF_DV = OFF_DK + D_W
OFF_G = OFF_DV + D_W
IN_W = OFF_G + 4 * D_MODEL
MIX_W = A_CH + B_Q_W + C_W + D_W

FFN_DIM = 7168
N_EXPERTS = 8
TOP_K = 2

LANES = 128
BF16_SUBLANES = 16
MIB = 2**20

TM_PROJ = 1024
TN_KV = 512
TN_IN = 1536
TN_FFN = 1024
TM_OUT, TN_OUT = 1024, 1024
TM_DOWN, TN_DOWN = 512, 1024
TM_MERGE, TN_MERGE = 1024, 512
TM_MOE = 512
TN_MOE_UP = 1024
TN_MOE_DOWN = 1024
TG_ROWS = 512
GATHER_UNROLL = 8
MOD_ROWS = 16


def _cparams(sem, vmem_mib):
    return pltpu.CompilerParams(dimension_semantics=sem, vmem_limit_bytes=vmem_mib * MIB)


def _silu(a):
    return a * jax.nn.sigmoid(a)


def _ada_kernel(c_ref, w_ref, b_ref, o_ref):
    c = c_ref[...]
    s = _silu(c).astype(BF16)
    o_ref[...] = jnp.dot(s, w_ref[...].astype(BF16), preferred_element_type=F32) + b_ref[...]


def _ada(cond, w_ada, b_ada):
    depth, _, n = w_ada.shape
    tn = 1024
    return pl.pallas_call(
        _ada_kernel,
        out_shape=jax.ShapeDtypeStruct((depth, MOD_ROWS, n), F32),
        grid=(depth, n // tn),
        in_specs=[
            pl.BlockSpec((MOD_ROWS, D_MODEL), lambda l, j: (0, 0)),
            pl.BlockSpec((None, D_MODEL, tn), lambda l, j: (l, 0, j)),
            pl.BlockSpec((None, 1, tn), lambda l, j: (l, 0, j)),
        ],
        out_specs=pl.BlockSpec((None, MOD_ROWS, tn), lambda l, j: (l, 0, j)),
        compiler_params=_cparams(("arbitrary", "arbitrary"), 40),
        name="ada_modulation",
    )(cond, w_ada, b_ada.reshape(depth, 1, n))


def _norm_mod(x, g, sc, sh):
    y = x * lax.rsqrt(jnp.mean(x * x, axis=-1, keepdims=True) + NORM_EPS) * g
    return y * (1.0 + sc) + sh


MXU_COLS = 256
NORM_CHUNK = 256
GMLP_ROWS = 2048


def _swiglu_store(x, w1_ref, w3_ref, o_ref):
    for c0 in range(0, o_ref.shape[1], MXU_COLS):
        cs = slice(c0, c0 + MXU_COLS)
        a = jnp.dot(x, w1_ref[:, cs].astype(BF16), preferred_element_type=F32)
        b = jnp.dot(x, w3_ref[:, cs].astype(BF16), preferred_element_type=F32)
        o_ref[:, cs] = (_silu(a) * b).astype(o_ref.dtype)


def _cast_block_rows(rows, steps):
    return next(d for d in range(BF16_SUBLANES, rows + 1, BF16_SUBLANES) if rows % d == 0 and rows // d <= steps)


def _norm_mm_kernel(x_ref, g_ref, sc_ref, sh_ref, *rest, swiglu, n_cast):
    n_w = 2 if swiglu else 1
    w_refs, cast_in = rest[:n_w], rest[n_w:n_w + n_cast]
    o_ref, cast_out, xn_ref = rest[n_w + n_cast], rest[n_w + n_cast + 1:n_w + 2 * n_cast + 1], rest[-1]

    for src, dst in zip(cast_in, cast_out):
        dst[...] = src[...].astype(BF16)

    @pl.when(pl.program_id(1) == 0)
    def _():
        def chunk(c, carry):
            rows = pl.ds(pl.multiple_of(c * NORM_CHUNK, NORM_CHUNK), NORM_CHUNK)
            xn_ref[rows, :] = _norm_mod(x_ref[rows, :], g_ref[...], sc_ref[0], sh_ref[0]).astype(BF16)
            return carry
        lax.fori_loop(0, x_ref.shape[0] // NORM_CHUNK, chunk, 0)

    xn = xn_ref[...]
    if swiglu:
        _swiglu_store(xn, w_refs[0], w_refs[1], o_ref)
    else:
        o_ref[...] = jnp.dot(xn, w_refs[0][...].astype(BF16), preferred_element_type=F32).astype(o_ref.dtype)


def _norm_mm(x, g, sc, sh, mod_row, weights, layer, col_block, n_out, tm, tn, name, cast=()):
    m = x.shape[0]
    swiglu = len(weights) == 2
    n_j = n_out // tn
    steps = (m // tm) * n_j
    w_spec = pl.BlockSpec((None, D_MODEL, tn), lambda i, j: (layer, 0, col_block(j)))
    cast_in_specs, cast_out_specs, cast_shapes = [], [], []
    for a, row_start, rows in cast:
        d = _cast_block_rows(rows, steps)
        assert row_start % d == 0
        blk = functools.partial(lambda i, j, nb, b0: (b0 + jnp.minimum(i * n_j + j, nb - 1), 0), nb=rows // d)
        cast_in_specs.append(pl.BlockSpec((d, a.shape[1]), functools.partial(blk, b0=row_start // d)))
        cast_out_specs.append(pl.BlockSpec((d, a.shape[1]), functools.partial(blk, b0=0)))
        cast_shapes.append(jax.ShapeDtypeStruct((rows, a.shape[1]), BF16))
    outs = pl.pallas_call(
        functools.partial(_norm_mm_kernel, swiglu=swiglu, n_cast=len(cast)),
        out_shape=[jax.ShapeDtypeStruct((m, n_out), BF16)] + cast_shapes,
        grid=(m // tm, n_j),
        in_specs=[
            pl.BlockSpec((tm, D_MODEL), lambda i, j: (i, 0)),
            pl.BlockSpec((1, D_MODEL), lambda i, j: (0, 0)),
            pl.BlockSpec((1, 1, D_MODEL), lambda i, j: (mod_row(i, tm), 0, 0)),
            pl.BlockSpec((1, 1, D_MODEL), lambda i, j: (mod_row(i, tm), 0, 0)),
        ] + [w_spec] * len(weights) + cast_in_specs,
        out_specs=[pl.BlockSpec((tm, tn), lambda i, j: (i, j))] + cast_out_specs,
        scratch_shapes=[pltpu.VMEM((tm, D_MODEL), BF16)],
        compiler_params=_cparams(("arbitrary", "arbitrary"), 56),
        name=name,
    )(x, g, sc, sh, *weights, *[a for a, _, _ in cast])
    return (outs[0], list(outs[1:])) if cast else outs[0]


def _res_mm_kernel(x_ref, w_ref, res_ref, gt_ref, o_ref):
    x = x_ref[...]
    for c0 in range(0, o_ref.shape[1], MXU_COLS):
        cs = slice(c0, c0 + MXU_COLS)
        acc = jnp.dot(x, w_ref[:, cs].astype(BF16), preferred_element_type=F32)
        o_ref[:, cs] = res_ref[:, cs] + gt_ref[0, :, cs] * acc


def _res_mm(x, w, layer, res, gt, mod_row, tm, tn, name, weight_stationary=False):
    m, k = x.shape
    n = res.shape[1]
    if weight_stationary:
        ij = lambda a, b: (b, a)
        grid = (n // tn, m // tm)
        w_mode = dict(pipeline_mode=pl.Buffered(1))
    else:
        ij = lambda a, b: (a, b)
        grid = (m // tm, n // tn)
        w_mode = {}
    return pl.pallas_call(
        _res_mm_kernel,
        out_shape=jax.ShapeDtypeStruct((m, n), F32),
        grid=grid,
        in_specs=[
            pl.BlockSpec((tm, k), lambda a, b: (ij(a, b)[0], 0)),
            pl.BlockSpec((None, k, tn), lambda a, b: (layer, 0, ij(a, b)[1]), **w_mode),
            pl.BlockSpec((tm, tn), lambda a, b: ij(a, b)),
            pl.BlockSpec((1, 1, tn), lambda a, b: (mod_row(ij(a, b)[0], tm), 0, ij(a, b)[1])),
        ],
        out_specs=pl.BlockSpec((tm, tn), lambda a, b: ij(a, b)),
        compiler_params=_cparams(("arbitrary", "arbitrary"), 56),
        name=name,
    )(x, w, res, gt)


def _gmlp_kernel(u_ref, v_ref, g_ref, ws_ref, bs_ref, o_ref):
    v = v_ref[...].astype(F32)
    vn = (v * lax.rsqrt(jnp.mean(v * v, axis=-1, keepdims=True) + NORM_EPS) * g_ref[...]).astype(BF16)
    rows = v.shape[0]
    for gi in range(A_GROUPS):
        w = ws_ref[gi].astype(BF16)
        bias = bs_ref[gi]
        cs = slice(gi * LANES, (gi + 1) * LANES)
        for c in range(rows // CHUNK):
            rs = slice(c * CHUNK, (c + 1) * CHUNK)
            s = jnp.dot(w, vn[rs, cs], preferred_element_type=F32) + bias
            o_ref[rs, cs] = (u_ref[rs, cs].astype(F32) * s).astype(o_ref.dtype)


def _gmlp(p, norm_g, ws, bs, layer):
    m = p.shape[0]
    tm = min(m, GMLP_ROWS)
    return pl.pallas_call(
        _gmlp_kernel,
        out_shape=jax.ShapeDtypeStruct((m, A_CH), BF16),
        grid=(m // tm,),
        in_specs=[
            pl.BlockSpec((tm, A_CH), lambda i: (i, OFF_AU // A_CH)),
            pl.BlockSpec((tm, A_CH), lambda i: (i, OFF_AV // A_CH)),
            pl.BlockSpec((1, A_CH), lambda i: (0, 0)),
            pl.BlockSpec((None, A_GROUPS, CHUNK, CHUNK), lambda i: (layer, 0, 0, 0)),
            pl.BlockSpec((None, A_GROUPS, CHUNK, 1), lambda i: (layer, 0, 0, 0)),
        ],
        out_specs=pl.BlockSpec((tm, A_CH), lambda i: (i, 0)),
        compiler_params=_cparams(("arbitrary",), 32),
        name="mixer_a_gmlp",
    )(p, p, norm_g[layer].reshape(1, A_CH), ws, bs.reshape(bs.shape + (1,)))


CONV_ROWS = 1024
HALO = BF16_SUBLANES


def _conv_kernel(b_ref, c_ref, h_ref, cp_ref, hp_ref, cn_ref, hn_ref, w_ref, o_ref, *, tiles_per_seq):
    i = pl.program_id(0)
    z = c_ref[...].astype(F32) * h_ref[...].astype(F32)
    n = z.shape[0]
    first = (i % tiles_per_seq) == 0
    last = (i % tiles_per_seq) == tiles_per_seq - 1
    zp = jnp.where(first, 0.0, cp_ref[HALO - 1:HALO, :].astype(F32) * hp_ref[HALO - 1:HALO, :].astype(F32))
    zn = jnp.where(last, 0.0, cn_ref[0:1, :].astype(F32) * hn_ref[0:1, :].astype(F32))
    pos = lax.broadcasted_iota(jnp.int32, z.shape, 0)
    z_prev = jnp.where(pos == 0, zp, pltpu.roll(z, 1, axis=0))
    z_next = jnp.where(pos == n - 1, zn, pltpu.roll(z, n - 1, axis=0))
    w = w_ref[...]
    conv = z_prev * w[0:1] + z * w[1:2] + z_next * w[2:3]
    o_ref[...] = (b_ref[...].astype(F32) * conv).astype(o_ref.dtype)


def _short_conv(p, conv_w8, layer, seq_len):
    m = p.shape[0]
    tr = min(seq_len, CONV_ROWS)
    per = tr // HALO
    n_halo = m // HALO

    def prev_blk(col):
        return lambda i: (jnp.maximum(i * per - 1, 0), col)

    def next_blk(col):
        return lambda i: (jnp.minimum((i + 1) * per, n_halo - 1), col)

    cc, ch = OFF_CC // C_W, OFF_CH // C_W
    return pl.pallas_call(
        functools.partial(_conv_kernel, tiles_per_seq=seq_len // tr),
        out_shape=jax.ShapeDtypeStruct((m, C_W), BF16),
        grid=(m // tr,),
        in_specs=[
            pl.BlockSpec((tr, C_W), lambda i: (i, OFF_CB // C_W)),
            pl.BlockSpec((tr, C_W), lambda i: (i, cc)),
            pl.BlockSpec((tr, C_W), lambda i: (i, ch)),
            pl.BlockSpec((HALO, C_W), prev_blk(cc)),
            pl.BlockSpec((HALO, C_W), prev_blk(ch)),
            pl.BlockSpec((HALO, C_W), next_blk(cc)),
            pl.BlockSpec((HALO, C_W), next_blk(ch)),
            pl.BlockSpec((None, 8, C_W), lambda i: (layer, 0, 0)),
        ],
        out_specs=pl.BlockSpec((tr, C_W), lambda i: (i, 0)),
        compiler_params=_cparams(("arbitrary",), 32),
        name="mixer_c_conv",
    )(p, p, p, p, p, p, p, conv_w8)


def _rope(x, cos, sin_lo, sin_hi):
    x = x.astype(F32)
    return x * cos + pltpu.roll(x, LANES - 32, axis=1) * sin_lo + pltpu.roll(x, 32, axis=1) * sin_hi


def _softmax_av(s_parts, v_parts, extra_logit=None):
    m = s_parts[0].max(axis=-1, keepdims=True)
    for s in s_parts[1:]:
        m = jnp.maximum(m, s.max(axis=-1, keepdims=True))
    if extra_logit is not None:
        m = jnp.maximum(m, extra_logit)
    denom = jnp.exp(extra_logit - m) if extra_logit is not None else 0.0
    acc = None
    for s, v in zip(s_parts, v_parts):
        e = jnp.exp(s - m)
        denom = denom + e.sum(axis=-1, keepdims=True)
        pv = jnp.dot(e.astype(BF16), v, preferred_element_type=F32)
        acc = pv if acc is None else acc + pv
    return acc / denom


def _qk(q, k):
    return lax.dot_general(q, k, (((1,), (1,)), ((), ())), preferred_element_type=F32)


def _sink_column(sink_ref, h, rows_per_head):
    rid = lax.broadcasted_iota(jnp.int32, (B_GROUP * rows_per_head, 1), 0) // rows_per_head
    col = jnp.zeros((B_GROUP * rows_per_head, 1), F32)
    for g in range(B_GROUP):
        col = jnp.where(rid == g, sink_ref[h * B_GROUP + g], col)
    return col


def _win_attn_kernel(sink_ref, q_ref, k_ref, v_ref, kc_ref, vc_ref, cos_ref, slo_ref, shi_ref, mask_ref, o_ref):
    n = pl.program_id(1)
    blk = B_WINDOW
    band = 3 * blk
    kblk = jnp.clip(n - 1, 0, SEQ // blk - 3)
    q0 = pl.multiple_of(n * blk, blk)
    k0 = pl.multiple_of(kblk * blk, blk)
    scale = HEAD_DIM ** -0.5

    cos_q, slo_q, shi_q = cos_ref[pl.ds(q0, blk), :], slo_ref[pl.ds(q0, blk), :], shi_ref[pl.ds(q0, blk), :]
    cos_k, slo_k, shi_k = cos_ref[pl.ds(k0, band), :], slo_ref[pl.ds(k0, band), :], shi_ref[pl.ds(k0, band), :]
    window_bias = mask_ref[n - kblk]

    for h in range(B_KV_HEADS):
        hs = slice(h * HEAD_DIM, (h + 1) * HEAD_DIM)
        k_loc = _rope(k_ref[pl.ds(k0, band), hs], cos_k, slo_k, shi_k).astype(BF16)
        v_loc = v_ref[pl.ds(k0, band), hs]
        k_ctx = kc_ref[:, hs]
        v_ctx = vc_ref[:, hs]
        qs = []
        for g in range(B_GROUP):
            c0 = (h * B_GROUP + g) * HEAD_DIM
            qs.append(_rope(q_ref[:, c0:c0 + HEAD_DIM], cos_q, slo_q, shi_q).astype(BF16))
        qst = jnp.concatenate(qs, axis=0)
        s_loc = _qk(qst, k_loc) * scale + window_bias
        s_ctx = _qk(qst, k_ctx) * scale
        out = _softmax_av([s_ctx, s_loc], [v_ctx, v_loc], _sink_column(sink_ref, h, blk))
        for g in range(B_GROUP):
            c0 = (h * B_GROUP + g) * HEAD_DIM
            o_ref[:, c0:c0 + HEAD_DIM] = out[g * blk:(g + 1) * blk].astype(o_ref.dtype)


def _window_bias_table():
    i = np.arange(B_WINDOW)[:, None]
    j = np.arange(3 * B_WINDOW)[None, :]
    tabs = []
    for c in range(3):
        rel = j - c * B_WINDOW - i
        one = np.where(np.abs(rel) <= B_WINDOW, 0.0, NEG_INF).astype(np.float32)
        tabs.append(np.tile(one, (B_GROUP, 1)))
    return jnp.asarray(np.stack(tabs))


def _win_attn(p, pc, kc_blk, vc_blk, sink, rope_tabs, batch):
    nb = SEQ // B_WINDOW
    kvw = B_KV_W
    mask_tab = _window_bias_table()
    return pl.pallas_call(
        _win_attn_kernel,
        out_shape=jax.ShapeDtypeStruct((batch * SEQ, B_Q_W), BF16),
        grid=(batch, nb),
        in_specs=[
            pl.BlockSpec(memory_space=pltpu.SMEM),
            pl.BlockSpec((B_WINDOW, B_Q_W), lambda b, n: (b * nb + n, OFF_BQ // B_Q_W)),
            pl.BlockSpec((SEQ, kvw), lambda b, n: (b, OFF_BK // kvw)),
            pl.BlockSpec((SEQ, kvw), lambda b, n: (b, OFF_BV // kvw)),
            pl.BlockSpec((CTX_LEN, kvw), lambda b, n: (b, kc_blk)),
            pl.BlockSpec((CTX_LEN, kvw), lambda b, n: (b, vc_blk)),
            pl.BlockSpec((SEQ, HEAD_DIM), lambda b, n: (0, 0)),
            pl.BlockSpec((SEQ, HEAD_DIM), lambda b, n: (0, 0)),
            pl.BlockSpec((SEQ, HEAD_DIM), lambda b, n: (0, 0)),
            pl.BlockSpec(mask_tab.shape, lambda b, n: (0, 0, 0)),
        ],
        out_specs=pl.BlockSpec((B_WINDOW, B_Q_W), lambda b, n: (b * nb + n, 0)),
        compiler_params=_cparams(("arbitrary", "arbitrary"), 40),
        name="mixer_b_window_attn",
    )(sink, p, p, p, pc, pc, *rope_tabs, mask_tab)


def _ctx_gqa_kernel(sink_ref, q_ref, kc_ref, vc_ref, o_ref):
    scale = HEAD_DIM ** -0.5
    lc = q_ref.shape[0]
    for h in range(B_KV_HEADS):
        hs = slice(h * HEAD_DIM, (h + 1) * HEAD_DIM)
        qst = jnp.concatenate(
            [q_ref[:, (h * B_GROUP + g) * HEAD_DIM:(h * B_GROUP + g + 1) * HEAD_DIM] for g in range(B_GROUP)],
            axis=0)
        out = _softmax_av([_qk(qst, kc_ref[:, hs]) * scale], [vc_ref[:, hs]], _sink_column(sink_ref, h, lc))
        for g in range(B_GROUP):
            c0 = (h * B_GROUP + g) * HEAD_DIM
            o_ref[:, c0:c0 + HEAD_DIM] = out[g * lc:(g + 1) * lc].astype(o_ref.dtype)


def _ctx_gqa(pc, sink, batch):
    kvw = B_KV_W
    return pl.pallas_call(
        _ctx_gqa_kernel,
        out_shape=jax.ShapeDtypeStruct((batch * CTX_LEN, B_Q_W), BF16),
        grid=(batch,),
        in_specs=[
            pl.BlockSpec(memory_space=pltpu.SMEM),
            pl.BlockSpec((CTX_LEN, B_Q_W), lambda b: (b, OFF_BQ // B_Q_W)),
            pl.BlockSpec((CTX_LEN, kvw), lambda b: (b, OFF_BK // kvw)),
            pl.BlockSpec((CTX_LEN, kvw), lambda b: (b, OFF_BV // kvw)),
        ],
        out_specs=pl.BlockSpec((CTX_LEN, B_Q_W), lambda b: (b, 0)),
        compiler_params=_cparams(("arbitrary",), 32),
        name="mixer_b_ctx_attn",
    )(sink, pc, pc, pc)


NA_ROWS_PER_STEP = 8
GRID_H = SEQ // GRID_W
NA_LOC = NA_KH * GRID_W


def _na_kernel(q_ref, k_ref, v_ref, kc_ref, vc_ref, bias_ref, o_ref):
    step = pl.program_id(1)
    scale = HEAD_DIM ** -0.5
    blocks, s_loc, s_ctx = [], [], []
    for rr in range(NA_ROWS_PER_STEP):
        r = step * NA_ROWS_PER_STEP + rr
        rs = jnp.clip(r - NA_KH // 2, 0, GRID_H - NA_KH)
        k0 = pl.multiple_of(rs * GRID_W, GRID_W)
        d = r - rs
        qrows = slice(rr * GRID_W, (rr + 1) * GRID_W)
        for h in range(D_HEADS):
            hs = slice(h * HEAD_DIM, (h + 1) * HEAD_DIM)
            q = q_ref[qrows, hs]
            s_loc.append(_qk(q, k_ref[pl.ds(k0, NA_LOC), hs]) * scale + bias_ref[h, d])
            s_ctx.append(_qk(q, kc_ref[:, hs]) * scale)
            blocks.append((qrows, hs, k0))
    s_loc = jnp.concatenate(s_loc, axis=0)
    s_ctx = jnp.concatenate(s_ctx, axis=0)
    m = jnp.maximum(s_loc.max(axis=-1, keepdims=True), s_ctx.max(axis=-1, keepdims=True))
    e_loc = jnp.exp(s_loc - m)
    e_ctx = jnp.exp(s_ctx - m)
    denom = e_loc.sum(axis=-1, keepdims=True) + e_ctx.sum(axis=-1, keepdims=True)
    e_loc = e_loc.astype(BF16)
    e_ctx = e_ctx.astype(BF16)
    for i, (qrows, hs, k0) in enumerate(blocks):
        rows = slice(i * GRID_W, (i + 1) * GRID_W)
        acc = jnp.dot(e_ctx[rows], vc_ref[:, hs], preferred_element_type=F32)
        acc = acc + jnp.dot(e_loc[rows], v_ref[pl.ds(k0, NA_LOC), hs], preferred_element_type=F32)
        o_ref[qrows, hs] = (acc / denom[rows]).astype(o_ref.dtype)


def _na_bias_table(rpb):
    rpb = rpb.astype(F32)
    rows = []
    for c in range(GRID_W):
        cstart = min(max(c - NA_KW // 2, 0), GRID_W - NA_KW)
        first_dc = cstart - c + NA_KW - 1
        win = rpb[:, :, first_dc:first_dc + NA_KW]
        rows.append(jnp.pad(win, ((0, 0), (0, 0), (cstart, GRID_W - NA_KW - cstart)), constant_values=NEG_INF))
    tab0 = jnp.stack(rows, axis=2)
    slabs = []
    for d in range(NA_KH):
        sl = tab0[:, NA_KH - 1 - d:2 * NA_KH - 1 - d]
        slabs.append(jnp.transpose(sl, (0, 2, 1, 3)).reshape(D_HEADS, GRID_W, NA_LOC))
    return jnp.stack(slabs, axis=1)


def _na_attn(p, pc, kc_blk, vc_blk, bias_tab, batch):
    rows_q = NA_ROWS_PER_STEP * GRID_W
    steps = SEQ // rows_q
    return pl.pallas_call(
        _na_kernel,
        out_shape=jax.ShapeDtypeStruct((batch * SEQ, D_W), BF16),
        grid=(batch, steps),
        in_specs=[
            pl.BlockSpec((rows_q, D_W), lambda b, s: (b * steps + s, OFF_DQ // D_W)),
            pl.BlockSpec((SEQ, D_W), lambda b, s: (b, OFF_DK // D_W)),
            pl.BlockSpec((SEQ, D_W), lambda b, s: (b, OFF_DV // D_W)),
            pl.BlockSpec((CTX_LEN, D_W), lambda b, s: (b, kc_blk)),
            pl.BlockSpec((CTX_LEN, D_W), lambda b, s: (b, vc_blk)),
            pl.BlockSpec((D_HEADS, NA_KH, GRID_W, NA_LOC), lambda b, s: (0, 0, 0, 0)),
        ],
        out_specs=pl.BlockSpec((rows_q, D_W), lambda b, s: (b * steps + s, 0)),
        compiler_params=_cparams(("arbitrary", "arbitrary"), 48),
        name="mixer_d_neighborhood_attn",
    )(p, p, p, pc, pc, bias_tab)


def _ctx_mha_kernel(q_ref, kc_ref, vc_ref, o_ref):
    scale = HEAD_DIM ** -0.5
    for h in range(D_HEADS):
        hs = slice(h * HEAD_DIM, (h + 1) * HEAD_DIM)
        s = _qk(q_ref[:, hs], kc_ref[:, hs]) * scale
        o_ref[:, hs] = _softmax_av([s], [vc_ref[:, hs]]).astype(o_ref.dtype)


def _ctx_mha(pc, batch):
    return pl.pallas_call(
        _ctx_mha_kernel,
        out_shape=jax.ShapeDtypeStruct((batch * CTX_LEN, D_W), BF16),
        grid=(batch,),
        in_specs=[
            pl.BlockSpec((CTX_LEN, D_W), lambda b: (b, OFF_DQ // D_W)),
            pl.BlockSpec((CTX_LEN, D_W), lambda b: (b, OFF_DK // D_W)),
            pl.BlockSpec((CTX_LEN, D_W), lambda b: (b, OFF_DV // D_W)),
        ],
        out_specs=pl.BlockSpec((CTX_LEN, D_W), lambda b: (b, 0)),
        compiler_params=_cparams(("arbitrary",), 32),
        name="mixer_d_ctx_attn",
    )(pc, pc, pc)


BR_W = (A_CH, B_Q_W, C_W, D_W)
BR_OFF = (0, A_CH, A_CH + B_Q_W, A_CH + B_Q_W + C_W)


def _merge_kernel(ya_ref, yb_ref, yc_ref, yd_ref, w_ref, g0_ref, g1_ref, g2_ref, g3_ref, o_ref):
    z = None
    for y_ref, g_ref, off, width in zip((ya_ref, yb_ref, yc_ref, yd_ref),
                                        (g0_ref, g1_ref, g2_ref, g3_ref), BR_OFF, BR_W):
        t = jnp.dot(y_ref[...], w_ref[off:off + width, :].astype(BF16), preferred_element_type=F32)
        t = jax.nn.sigmoid(g_ref[...].astype(F32)) * t
        z = t if z is None else z + t
    o_ref[...] = z.astype(o_ref.dtype)


def _merge(p, ys, w_branch, layer):
    m = p.shape[0]
    tm, tn = TM_MERGE, TN_MERGE
    gate_specs = [pl.BlockSpec((tm, tn), functools.partial(
        lambda i, j, b: (i, (OFF_G + b * D_MODEL) // tn + j), b=b)) for b in range(4)]
    return pl.pallas_call(
        _merge_kernel,
        out_shape=jax.ShapeDtypeStruct((m, D_MODEL), BF16),
        grid=(m // tm, D_MODEL // tn),
        in_specs=[pl.BlockSpec((tm, w), lambda i, j: (i, 0)) for w in BR_W]
        + [pl.BlockSpec((None, MIX_W, tn), lambda i, j: (layer, 0, j))] + gate_specs,
        out_specs=pl.BlockSpec((tm, tn), lambda i, j: (i, j)),
        compiler_params=_cparams(("arbitrary", "arbitrary"), 48),
        name="branch_merge",
    )(*ys, w_branch, p, p, p, p)


ROUTE_LANES = LANES
LANE_IDX = N_EXPERTS
LANE_WT = N_EXPERTS + TOP_K


def _router_kernel(x_ref, g_ref, sc_ref, sh_ref, wr_ref, xn_ref, route_ref):
    xn = _norm_mod(x_ref[...], g_ref[...], sc_ref[0], sh_ref[0])
    xn_ref[...] = xn.astype(BF16).reshape(xn_ref.shape)
    x_hi = xn.astype(BF16)
    x_lo = (xn - x_hi.astype(F32)).astype(BF16)
    wr = wr_ref[...]
    w_hi = wr.astype(BF16)
    w_lo = (wr - w_hi.astype(F32)).astype(BF16)
    logits = (jnp.dot(x_hi, w_hi, preferred_element_type=F32)
              + (jnp.dot(x_hi, w_lo, preferred_element_type=F32) + jnp.dot(x_lo, w_hi, preferred_element_type=F32)))
    lane = lax.broadcasted_iota(jnp.int32, logits.shape, 1).astype(F32)
    logits = jnp.where(lane < N_EXPERTS, logits, -jnp.inf)
    m1 = logits.max(axis=-1, keepdims=True)
    i1 = jnp.where(logits == m1, lane, float(ROUTE_LANES)).min(axis=-1, keepdims=True)
    rest = jnp.where(lane == i1, -jnp.inf, logits)
    m2 = rest.max(axis=-1, keepdims=True)
    i2 = jnp.where(rest == m2, lane, float(ROUTE_LANES)).min(axis=-1, keepdims=True)
    e2 = jnp.exp(m2 - m1)
    w1 = 1.0 / (1.0 + e2)
    w2 = e2 / (1.0 + e2)
    out = jnp.where(lane == LANE_IDX, i1, 0.0)
    out = jnp.where(lane == LANE_IDX + 1, i2, out)
    out = jnp.where(lane == LANE_WT, w1, out)
    out = jnp.where(lane == LANE_WT + 1, w2, out)
    route_ref[...] = out


def _router(h, g, sc, sh, mod_row, w_router_pad):
    t = h.shape[0]
    tm = 512
    return pl.pallas_call(
        _router_kernel,
        out_shape=(jax.ShapeDtypeStruct((t, D_MODEL // LANES, LANES), BF16),
                   jax.ShapeDtypeStruct((t, ROUTE_LANES), F32)),
        grid=(t // tm,),
        in_specs=[
            pl.BlockSpec((tm, D_MODEL), lambda i: (i, 0)),
            pl.BlockSpec((1, D_MODEL), lambda i: (0, 0)),
            pl.BlockSpec((1, 1, D_MODEL), lambda i: (mod_row(i, tm), 0, 0)),
            pl.BlockSpec((1, 1, D_MODEL), lambda i: (mod_row(i, tm), 0, 0)),
            pl.BlockSpec((D_MODEL, ROUTE_LANES), lambda i: (0, 0)),
        ],
        out_specs=(pl.BlockSpec((tm, D_MODEL // LANES, LANES), lambda i: (i, 0, 0)),
                   pl.BlockSpec((tm, ROUTE_LANES), lambda i: (i, 0))),
        compiler_params=_cparams(("arbitrary",), 40),
        name="moe_router",
    )(h, g, sc, sh, w_router_pad)


def _gather_rows_kernel(idx_ref, src_ref, o_ref, *scratch, n_sources, flat):
    sem = scratch[-1]
    dst_ref = scratch[0] if flat else o_ref
    rows = dst_ref.shape[-3]

    def row_copy(s, r, src_row):
        return pltpu.make_async_copy(src_ref.at[src_row], dst_ref.at[s, r], sem)

    def issue(i, carry):
        for u in range(GATHER_UNROLL):
            r = i * GATHER_UNROLL + u
            for s in range(n_sources):
                row_copy(s, r, idx_ref[0, s, r]).start(priority=(u * n_sources + s) % 2)
        return carry

    lax.fori_loop(0, rows // GATHER_UNROLL, issue, 0)

    def drain(i, carry):
        for u in range(GATHER_UNROLL):
            for s in range(n_sources):
                row_copy(s, i * GATHER_UNROLL + u, 0).wait()
        return carry

    lax.fori_loop(0, rows // GATHER_UNROLL, drain, 0)
    if flat:
        o_ref[...] = dst_ref[0].reshape(o_ref.shape)


def _gather_rows(src3, idx, n_sources, flat=False):
    _, s_dim, lanes = src3.shape
    r = idx.shape[1]
    tg = TG_ROWS
    idx_blocks = idx.reshape(n_sources, r // tg, tg).transpose(1, 0, 2)
    if flat:
        assert n_sources == 1
        out_shape = jax.ShapeDtypeStruct((r, s_dim * lanes), src3.dtype)
        out_spec = pl.BlockSpec((tg, s_dim * lanes), lambda i: (i, 0))
        scratch = [pltpu.VMEM((1, tg, s_dim, lanes), src3.dtype), pltpu.SemaphoreType.DMA]
    else:
        out_shape = jax.ShapeDtypeStruct((n_sources, r, s_dim, lanes), src3.dtype)
        out_spec = pl.BlockSpec((n_sources, tg, s_dim, lanes), lambda i: (0, i, 0, 0))
        scratch = [pltpu.SemaphoreType.DMA]
    return pl.pallas_call(
        functools.partial(_gather_rows_kernel, n_sources=n_sources, flat=flat),
        out_shape=out_shape,
        grid=(r // tg,),
        in_specs=[
            pl.BlockSpec((1, n_sources, tg), lambda i: (i, 0, 0), memory_space=pltpu.SMEM),
            pl.BlockSpec(memory_space=pl.ANY),
        ],
        out_specs=out_spec,
        scratch_shapes=scratch,
        compiler_params=_cparams(("arbitrary",), 32),
        name="row_gather",
    )(idx_blocks, src3)


def _moe_up_kernel(te_ref, nu_ref, x_ref, w1_ref, w3_ref, w2_ref, o_ref, w2b_ref):
    used = pl.program_id(1) < nu_ref[0]
    w2b_ref[...] = w2_ref[...].astype(BF16)

    @pl.when(used)
    def _():
        _swiglu_store(x_ref[...], w1_ref, w3_ref, o_ref)

    @pl.when(jnp.logical_not(used))
    def _():
        o_ref[...] = jnp.zeros_like(o_ref)


def _moe_up(xs, w1, w3, w2, layer, tile_expert, n_used):
    r = xs.shape[0]
    tm, tn = TM_MOE, TN_MOE_UP
    n_t = r // tm
    w2_rows = N_EXPERTS * FFN_DIM
    steps = (FFN_DIM // tn) * n_t
    cast_rows = _cast_block_rows(w2_rows, steps)
    cast_blocks = w2_rows // cast_rows
    w2_flat = w2.reshape(-1, D_MODEL)

    def cast_blk(j, t, te, nu):
        return (layer * cast_blocks + jnp.minimum(j * n_t + t, cast_blocks - 1), 0)

    def cast_out_blk(j, t, te, nu):
        return (jnp.minimum(j * n_t + t, cast_blocks - 1), 0)

    w_spec = pl.BlockSpec((None, None, D_MODEL, tn), lambda j, t, te, nu: (layer, te[t], 0, j))
    grid_spec = pltpu.PrefetchScalarGridSpec(
        num_scalar_prefetch=2,
        grid=(FFN_DIM // tn, n_t),
        in_specs=[pl.BlockSpec((tm, D_MODEL), lambda j, t, te, nu: (jnp.minimum(t, nu[0] - 1), 0)),
                  w_spec, w_spec,
                  pl.BlockSpec((cast_rows, D_MODEL), cast_blk)],
        out_specs=(pl.BlockSpec((tm, tn), lambda j, t, te, nu: (t, j)),
                   pl.BlockSpec((cast_rows, D_MODEL), cast_out_blk)),
    )
    hm, w2b = pl.pallas_call(
        _moe_up_kernel,
        out_shape=(jax.ShapeDtypeStruct((r, FFN_DIM), BF16),
                   jax.ShapeDtypeStruct((w2_rows, D_MODEL), BF16)),
        grid_spec=grid_spec,
        compiler_params=_cparams(("arbitrary", "arbitrary"), 56),
        name="moe_up",
    )(tile_expert, n_used, xs, w1, w3, w2_flat)
    return hm, w2b.reshape(1, N_EXPERTS, FFN_DIM, D_MODEL)


def _moe_down_kernel(te_ref, nu_ref, x_ref, w_ref, o_ref):
    used = pl.program_id(1) < nu_ref[0]

    @pl.when(used)
    def _():
        x = x_ref[...]
        for c0 in range(0, o_ref.shape[1], MXU_COLS):
            cs = slice(c0, c0 + MXU_COLS)
            o_ref[:, cs] = jnp.dot(x, w_ref[:, cs], preferred_element_type=F32).astype(o_ref.dtype)

    @pl.when(jnp.logical_not(used))
    def _():
        o_ref[...] = jnp.zeros_like(o_ref)


def _moe_down(hmid, w2, layer, tile_expert, n_used):
    r = hmid.shape[0]
    tm, tn = TM_MOE, TN_MOE_DOWN
    grid_spec = pltpu.PrefetchScalarGridSpec(
        num_scalar_prefetch=2,
        grid=(D_MODEL // tn, r // tm),
        in_specs=[
            pl.BlockSpec((tm, FFN_DIM), lambda j, t, te, nu: (jnp.minimum(t, nu[0] - 1), 0)),
            pl.BlockSpec((None, None, FFN_DIM, tn), lambda j, t, te, nu: (layer, te[t], 0, j)),
        ],
        out_specs=pl.BlockSpec((tm, tn), lambda j, t, te, nu: (t, j)),
    )
    return pl.pallas_call(
        _moe_down_kernel,
        out_shape=jax.ShapeDtypeStruct((r, D_MODEL), BF16),
        grid_spec=grid_spec,
        compiler_params=_cparams(("arbitrary", "arbitrary"), 60),
        name="moe_down",
    )(tile_expert, n_used, hmid, w2)


def _route_plan(route, tm):
    t = route.shape[0]
    idx = route[:, LANE_IDX:LANE_IDX + TOP_K].astype(jnp.int32)
    e_flat = idx.T.reshape(-1)
    onehot = (e_flat[:, None] == jnp.arange(N_EXPERTS)[None, :]).astype(jnp.int32)
    counts = onehot.sum(axis=0)
    rank = (onehot * (jnp.cumsum(onehot, axis=0) - onehot)).sum(axis=1)
    tiles_e = (counts + tm - 1) // tm
    tile_end = jnp.cumsum(tiles_e)
    tile_start = tile_end - tiles_e
    pos = (onehot * tile_start[None, :]).sum(axis=1) * tm + rank
    n_tiles = TOP_K * t // tm + N_EXPERTS
    n_used = tile_end[-1]
    tile_ids = jnp.minimum(jnp.arange(n_tiles), n_used - 1)
    tile_expert = jnp.minimum((tile_ids[:, None] >= tile_end[None, :]).sum(axis=1), N_EXPERTS - 1)
    row_token = (jnp.arange(n_tiles * tm, dtype=jnp.int32) % t).at[pos].set(
        jnp.tile(jnp.arange(t, dtype=jnp.int32), TOP_K))
    return (pos.reshape(TOP_K, t).astype(jnp.int32), row_token,
            tile_expert.astype(jnp.int32), n_used.reshape(1).astype(jnp.int32))


FINAL_ROWS = 512


def _final_kernel(idx_ref, idx_next_ref, y_ref, h_ref, route_ref, gt_ref, g_ref, o_ref, buf, sems):
    i = pl.program_id(0)
    half = FINAL_ROWS // 2

    def row_copy(slot, k, r, src_row):
        return pltpu.make_async_copy(y_ref.at[src_row], buf.at[slot, k, r], sems.at[slot])

    def start_half(ids_ref, slot, first_row):
        def body(g, carry):
            for u in range(GATHER_UNROLL):
                r = g * GATHER_UNROLL + u
                for k in range(TOP_K):
                    row_copy(slot, k, r, ids_ref[0, k, first_row + r]).start(priority=(u * TOP_K + k) % 2)
            return carry
        lax.fori_loop(0, half // GATHER_UNROLL, body, 0)

    def wait_half(slot):
        def body(g, carry):
            for u in range(GATHER_UNROLL):
                for k in range(TOP_K):
                    row_copy(slot, k, g * GATHER_UNROLL + u, 0).wait()
            return carry
        lax.fori_loop(0, half // GATHER_UNROLL, body, 0)

    def combine_half(slot, first_row):
        rows = slice(first_row, first_row + half)
        w1 = route_ref[rows, LANE_WT:LANE_WT + 1]
        w2 = route_ref[rows, LANE_WT + 1:LANE_WT + 2]
        y0 = buf[slot, 0].reshape(half, D_MODEL).astype(F32)
        y1 = buf[slot, 1].reshape(half, D_MODEL).astype(F32)
        h = h_ref[rows, :] + gt_ref[0] * (w1 * y0 + w2 * y1)
        o_ref[rows, :] = h * lax.rsqrt(jnp.mean(h * h, axis=-1, keepdims=True) + NORM_EPS) * g_ref[...]

    @pl.when(i == 0)
    def _():
        start_half(idx_ref, 0, 0)

    start_half(idx_ref, 1, half)
    wait_half(0)
    combine_half(0, 0)

    @pl.when(i + 1 < pl.num_programs(0))
    def _():
        start_half(idx_next_ref, 0, 0)

    wait_half(1)
    combine_half(1, half)


def _final(h, ye3, pos, route, gt, mod_row, g_final):
    t = h.shape[0]
    tm = FINAL_ROWS
    n = t // tm
    _, s_dim, lanes = ye3.shape
    idx_blocks = pos.reshape(TOP_K, n, tm).transpose(1, 0, 2)
    return pl.pallas_call(
        _final_kernel,
        out_shape=jax.ShapeDtypeStruct((t, D_MODEL), F32),
        grid=(n,),
        in_specs=[
            pl.BlockSpec((1, TOP_K, tm), lambda i: (i, 0, 0), memory_space=pltpu.SMEM),
            pl.BlockSpec((1, TOP_K, tm), lambda i: (jnp.minimum(i + 1, n - 1), 0, 0), memory_space=pltpu.SMEM),
            pl.BlockSpec(memory_space=pl.ANY),
            pl.BlockSpec((tm, D_MODEL), lambda i: (i, 0)),
            pl.BlockSpec((tm, ROUTE_LANES), lambda i: (i, 0)),
            pl.BlockSpec((1, 1, D_MODEL), lambda i: (mod_row(i, tm), 0, 0)),
            pl.BlockSpec((1, D_MODEL), lambda i: (0, 0)),
        ],
        out_specs=pl.BlockSpec((tm, D_MODEL), lambda i: (i, 0)),
        scratch_shapes=[pltpu.VMEM((2, TOP_K, tm // 2, s_dim, lanes), ye3.dtype),
                        pltpu.SemaphoreType.DMA((2,))],
        compiler_params=_cparams(("arbitrary",), 48),
        name="final_combine_norm",
    )(idx_blocks, idx_blocks, ye3, h, route, gt, g_final)


def _rope_tables():
    pos = jnp.arange(SEQ)
    f = HEAD_DIM // 4
    inv = ROPE_BASE ** (-jnp.arange(f, dtype=F32) / f)
    ang_row = (pos // GRID_W).astype(F32)[:, None] * inv[None, :]
    ang_col = (pos % GRID_W).astype(F32)[:, None] * inv[None, :]
    ang = jnp.concatenate([ang_row, ang_row, ang_col, ang_col], axis=1)
    cos, sin = jnp.cos(ang), jnp.sin(ang)
    first = jnp.asarray(((np.arange(HEAD_DIM) // f) % 2 == 0)[None, :])
    return cos, jnp.where(first, -sin, 0.0), jnp.where(first, 0.0, sin)


def kernel(x, c, ctx, c_ctx, w_ada, b_ada, g_mix, g_ffn, w_in, a_norm_g, a_ws, a_bs, b_sink, c_conv,
           d_rpb, w_branch, w_out, ffn_w1, ffn_w3, ffn_w2, w_router, moe_w1, moe_w3, moe_w2, g_final):
    batch, seq, _ = x.shape
    depth = w_in.shape[0]
    assert seq == SEQ and ctx.shape[1] == CTX_LEN and batch + 1 <= MOD_ROWS
    assert depth % 2 == 0, "the fused residual + final-norm epilogue lives in the MoE (odd, last) layer"
    t = batch * seq
    tc = batch * CTX_LEN

    cond = jnp.zeros((MOD_ROWS, D_MODEL), F32).at[:batch].set(c).at[batch].set(c_ctx)
    mods = _ada(cond, w_ada, b_ada)
    rope_tabs = _rope_tables()
    conv_w8 = jnp.zeros((depth, 8, C_W), F32).at[:, :3].set(c_conv)

    mix_w = [w[0:1].astype(BF16) for w in (w_in, w_branch, w_out)]

    def lat_row(i, tm):
        return (i * tm) // SEQ

    def ctx_row(i, tm):
        return batch

    h = x.reshape(t, D_MODEL)
    hc = ctx.reshape(tc, D_MODEL)
    out = None
    for layer in range(depth):
        last = layer == depth - 1
        sh1, sc1, gt1, sh2, sc2, gt2 = [m.reshape(MOD_ROWS, 1, D_MODEL)
                                        for m in jnp.split(mods[layer], 6, axis=-1)]
        g_mix_l = g_mix[layer].reshape(1, D_MODEL)
        g_ffn_l = g_ffn[layer].reshape(1, D_MODEL)
        bias_tab = _na_bias_table(d_rpb[layer])
        w_in_l, w_branch_l, w_out_l = mix_w
        j = layer // 2
        dense = layer % 2 == 0

        if last:
            pc = _norm_mm(hc, g_mix_l, sc1, sh1, ctx_row, (w_in_l,), 0,
                          lambda n: jnp.where(n == 0, OFF_BK // TN_KV, OFF_DK // TN_KV - 1 + n),
                          3 * TN_KV, TM_PROJ, TN_KV, "ctx_kv_proj")
            kcb, vcb, kcd, vcd = 0, 1, 1, 2
        else:
            pc = _norm_mm(hc, g_mix_l, sc1, sh1, ctx_row, (w_in_l,), 0, lambda n: n,
                          IN_W, TM_PROJ, TN_IN, "ctx_in_proj")
            kcb, vcb, kcd, vcd = OFF_BK // B_KV_W, OFF_BV // B_KV_W, OFF_DK // D_W, OFF_DV // D_W

        if dense:
            ffn_cast = [(ffn_w1.reshape(-1, FFN_DIM), j * D_MODEL, D_MODEL),
                        (ffn_w3.reshape(-1, FFN_DIM), j * D_MODEL, D_MODEL),
                        (ffn_w2.reshape(-1, D_MODEL), j * FFN_DIM, FFN_DIM)]
            px, (w1b, w3b, w2b) = _norm_mm(h, g_mix_l, sc1, sh1, lat_row, (w_in_l,), 0, lambda n: n,
                                           IN_W, TM_PROJ, TN_IN, "in_proj", cast=ffn_cast)
            w1b, w3b, w2b = w1b[None], w3b[None], w2b[None]
        else:
            px = _norm_mm(h, g_mix_l, sc1, sh1, lat_row, (w_in_l,), 0, lambda n: n,
                          IN_W, TM_PROJ, TN_IN, "in_proj")
        ys = (
            _gmlp(px, a_norm_g, a_ws, a_bs, layer),
            _win_attn(px, pc, kcb, vcb, b_sink[layer], rope_tabs, batch),
            _short_conv(px, conv_w8, layer, SEQ),
            _na_attn(px, pc, kcd, vcd, bias_tab, batch),
        )
        z = _merge(px, ys, w_branch_l, 0)
        h = _res_mm(z, w_out_l, 0, h, gt1, lat_row, TM_OUT, TN_OUT, "out_proj")

        if not last:
            ysc = (
                _gmlp(pc, a_norm_g, a_ws, a_bs, layer),
                _ctx_gqa(pc, b_sink[layer], batch),
                _short_conv(pc, conv_w8, layer, CTX_LEN),
                _ctx_mha(pc, batch),
            )
            zc = _merge(pc, ysc, w_branch_l, 0)
            hc = _res_mm(zc, w_out_l, 0, hc, gt1, ctx_row, TM_OUT, TN_OUT, "ctx_out_proj")

        if dense:
            if last:
                hm = _norm_mm(h, g_ffn_l, sc2, sh2, lat_row, (w1b, w3b), 0, lambda n: n,
                              FFN_DIM, TM_PROJ, TN_FFN, "ffn_up")
            else:
                nxt = layer + 1
                mix_cast = [(w_in.reshape(-1, IN_W), nxt * D_MODEL, D_MODEL),
                            (w_branch.reshape(-1, D_MODEL), nxt * MIX_W, MIX_W),
                            (w_out.reshape(-1, D_MODEL), nxt * D_MODEL, D_MODEL)]
                hm, mix_w = _norm_mm(h, g_ffn_l, sc2, sh2, lat_row, (w1b, w3b), 0, lambda n: n,
                                     FFN_DIM, TM_PROJ, TN_FFN, "ffn_up", cast=mix_cast)
                mix_w = [w[None] for w in mix_w]
            h = _res_mm(hm, w2b, 0, h, gt2, lat_row, TM_DOWN, TN_DOWN, "ffn_down", weight_stationary=True)
            if not last:
                hmc = _norm_mm(hc, g_ffn_l, sc2, sh2, ctx_row, (w1b, w3b), 0, lambda n: n,
                               FFN_DIM, TM_PROJ, TN_FFN, "ctx_ffn_up")
                hc = _res_mm(hmc, w2b, 0, hc, gt2, ctx_row, TM_DOWN, TN_DOWN, "ctx_ffn_down",
                             weight_stationary=True)
        else:
            assert last, "the context MoE path would only be needed for an odd layer that is not the last"
            wr_pad = jnp.zeros((D_MODEL, ROUTE_LANES), F32).at[:, :N_EXPERTS].set(w_router[j])
            xn, route = _router(h, g_ffn_l, sc2, sh2, lat_row, wr_pad)
            pos, row_token, tile_expert, n_used = _route_plan(route, TM_MOE)
            xs = _gather_rows(xn, row_token[None, :], 1, flat=True)
            hm, w2b = _moe_up(xs, moe_w1, moe_w3, moe_w2, j, tile_expert, n_used)
            ye = _moe_down(hm, w2b, 0, tile_expert, n_used)
            out = _final(h, ye.reshape(-1, D_MODEL // LANES, LANES), pos, route, gt2, lat_row,
                         g_final.reshape(1, D_MODEL))

    return out.reshape(batch, seq, D_MODEL)
```
